```python
import jax, jax.numpy as jnp
from jax import lax
import numpy as np

D_MODEL = 2048
BATCH = 16
SEQ = 2048
DEPTH = 4

GLA_HEADS = 4
GLA_DK = D_MODEL // 2
GLA_DV = D_MODEL
GLA_HEAD_K = GLA_DK // GLA_HEADS
GLA_HEAD_V = GLA_DV // GLA_HEADS
GLA_GATE_RANK = 16
GLA_GATE_NORMALIZER = 16.0
GLA_CHUNK = 64
GLA_SUBCHUNK = 16
GLA_SPLITS = (GLA_DK, 2 * GLA_DK, 2 * GLA_DK + GLA_DV, 2 * GLA_DK + 2 * GLA_DV)
GLA_IN = 2 * GLA_DK + 2 * GLA_DV + GLA_GATE_RANK

DIL_PATTERNS = ((128, 1), (512, 4), (2048, 16))
DIL_GROUPS = len(DIL_PATTERNS)
DIL_HEADS = 8
DIL_HEAD_DIM = 128
DIL_WIDTH = DIL_HEADS * DIL_HEAD_DIM
DIL_BLOCK = 128
DIL_IN = 3 * DIL_GROUPS * DIL_WIDTH

D_FF = 5504
CONV_WIDTH = 3

N_GLA_LAYERS = (DEPTH + 1) // 2
N_DIL_LAYERS = DEPTH // 2

DEEPNORM_ALPHA = (2 * DEPTH) ** 0.25
DEEPNORM_BETA = (8 * DEPTH) ** -0.25
LN_EPS = 1e-5
RMS_EPS = 1e-6

kernel_name = "hybrid_gla_dilated_convffn_deepnorm"


def layer_norm(x, g, b):
    xf = x.astype(jnp.float32)
    mu = xf.mean(-1, keepdims=True)
    var = jnp.square(xf - mu).mean(-1, keepdims=True)
    return ((xf - mu) * lax.rsqrt(var + LN_EPS)).astype(x.dtype) * g + b


def gla_chunk_step(state, inp):
    q, k, v, g = inp
    Bb, H, C, dk = q.shape
    c = GLA_SUBCHUNK
    n = C // c
    b = jnp.cumsum(g, axis=2)
    b_last = b[:, :, -1:, :]
    o_inter = jnp.einsum('bhcd,bhde->bhce', q * jnp.exp(b), state)
    qs = q.reshape(Bb, H, n, c, dk)
    ks = k.reshape(Bb, H, n, c, dk)
    vs = v.reshape(Bb, H, n, c, -1)
    bs = b.reshape(Bb, H, n, c, dk)
    b_ref = jnp.concatenate([jnp.zeros_like(bs[:, :, :1, -1]), bs[:, :, :-1, -1]], axis=2)
    q_ref = qs * jnp.exp(bs - b_ref[:, :, :, None, :])
    earlier = jnp.arange(C)[None, :] < (jnp.arange(n) * c)[:, None]
    k_exp = jnp.where(earlier[None, None, :, :, None],
                      b_ref[:, :, :, None, :] - b[:, :, None, :, :], -jnp.inf)
    k_ref = k[:, :, None] * jnp.exp(k_exp)
    a_inter = jnp.einsum('bhsid,bhsjd->bhsij', q_ref, k_ref)
    causal = jnp.tril(jnp.ones((c, c), dtype=bool))
    d_exp = jnp.where(causal[:, :, None],
                      bs[:, :, :, :, None, :] - bs[:, :, :, None, :, :], -jnp.inf)
    a_intra = jnp.einsum('bhsid,bhsjd,bhsijd->bhsij', qs, ks, jnp.exp(d_exp))
    o_intra = (jnp.einsum('bhsij,bhje->bhsie', a_inter, v)
               + jnp.einsum('bhsij,bhsje->bhsie', a_intra, vs)).reshape(Bb, H, C, -1)
    new_state = (jnp.exp(b_last[:, :, 0, :])[..., None] * state
                 + jnp.einsum('bhcd,bhce->bhde', k * jnp.exp(b_last - b), v))
    return new_state, o_inter + o_intra


def gla_mixer(x, w_in, w_gate_up, gate_bias, norm_g, w_out):
    B, S, _ = x.shape
    H, dk, dv, C = GLA_HEADS, GLA_HEAD_K, GLA_HEAD_V, GLA_CHUNK
    proj = x @ w_in
    q, k, v, r, g_low = jnp.split(proj, list(GLA_SPLITS), axis=-1)
    log_gate = jax.nn.log_sigmoid((g_low @ w_gate_up + gate_bias).astype(jnp.float32)) / GLA_GATE_NORMALIZER
    q = q.astype(jnp.float32) * dk ** -0.5

    def to_chunks(t, d):
        return t.astype(jnp.float32).reshape(B, S // C, C, H, d).transpose(1, 0, 3, 2, 4)

    xs = (to_chunks(q, dk), to_chunks(k, dk), to_chunks(v, dv), to_chunks(log_gate, dk))
    state0 = jnp.zeros((B, H, dk, dv), jnp.float32)
    _, o = lax.scan(gla_chunk_step, state0, xs)
    o = o.transpose(1, 0, 3, 2, 4).reshape(B, S, H, dv)
    o = o * lax.rsqrt(jnp.mean(jnp.square(o), -1, keepdims=True) + RMS_EPS) * norm_g
    o = o.reshape(B, S, GLA_DV).astype(x.dtype) * jax.nn.silu(r)
    return o @ w_out


def banded_attention(q, k, v, steps):
    N, L, H, dh = q.shape
    P = DIL_BLOCK
    nb = -(-L // P)
    Lp = nb * P
    pad = ((0, 0), (0, Lp - L), (0, 0), (0, 0))
    q, k, v = (jnp.pad(t, pad).reshape(N, nb, P, H, dh) for t in (q, k, v))

    def with_prev(t):
        prev = jnp.concatenate([jnp.zeros_like(t[:, :1]), t[:, :-1]], axis=1)
        return jnp.concatenate([prev, t], axis=2)

    kw, vw = with_prev(k), with_prev(v)
    s = jnp.einsum('nbqhd,nbkhd->nbhqk', q, kw).astype(jnp.float32) * dh ** -0.5
    qi = jnp.arange(P)[:, None] + P
    kj = jnp.arange(2 * P)[None, :]
    dist = qi - kj
    band = (dist >= 0) & (dist <= steps)
    real_key = (jnp.arange(nb)[:, None, None] > 0) | (kj[None] >= P)
    mask = band[None] & real_key
    s = jnp.where(mask[None, :, None], s, -jnp.inf)
    m = s.max(-1, keepdims=True)
    p = jnp.exp(s - m)
    l = p.sum(-1, keepdims=True)
    o = jnp.einsum('nbhqk,nbkhd->nbqhd', (p / l).astype(v.dtype), vw)
    lse = (m + jnp.log(l))[..., 0]
    o = o.reshape(N, Lp, H, dh)[:, :L]
    lse = lse.transpose(0, 1, 3, 2).reshape(N, Lp, H)[:, :L]
    return o, lse


def dilated_group(q, k, v, window, dilation):
    B, S, H, dh = q.shape
    L = S // dilation

    def to_strided(t):
        return t.reshape(B, L, dilation, H, dh).transpose(0, 2, 1, 3, 4).reshape(B * dilation, L, H, dh)

    o, lse = banded_attention(to_strided(q), to_strided(k), to_strided(v), window // dilation)
    o = o.reshape(B, dilation, L, H, dh).transpose(0, 2, 1, 3, 4).reshape(B, S, H, dh)
    lse = lse.reshape(B, dilation, L, H).transpose(0, 2, 1, 3).reshape(B, S, H)
    return o, lse


def dilated_mixer(x, w_in, w_out):
    B, S, _ = x.shape
    proj = (x @ w_in).reshape(B, S, DIL_GROUPS, 3, DIL_HEADS, DIL_HEAD_DIM)
    outs, lses = [], []
    for gi, (window, dilation) in enumerate(DIL_PATTERNS):
        o, lse = dilated_group(proj[:, :, gi, 0], proj[:, :, gi, 1], proj[:, :, gi, 2], window, dilation)
        outs.append(o)
        lses.append(lse)
    wts = jax.nn.softmax(jnp.stack(lses, 0), axis=0)
    o = jnp.einsum('gbsh,gbshd->bshd', wts.astype(x.dtype), jnp.stack(outs, 0))
    return o.reshape(B, S, DIL_WIDTH) @ w_out


def conv_ffn(x, w_up, conv_w, conv_b, w_down):
    S = x.shape[1]
    h = x @ w_up
    hp = jnp.pad(h, ((0, 0), (CONV_WIDTH - 1, 0), (0, 0)))
    h = sum((conv_w[j] * hp[:, j:j + S] for j in range(CONV_WIDTH)), conv_b)
    gate, up = jnp.split(h, 2, axis=-1)
    return (jax.nn.silu(gate) * up) @ w_down


def setup_inputs(seed: int = 0) -> dict:
    key = jax.random.key(seed)
    ks = jax.random.split(key, 16)
    nrm = lambda k, shape, scale: jax.random.normal(k, shape, jnp.float32) * scale
    return {
        "x": nrm(ks[0], (BATCH, SEQ, D_MODEL), 1.0),
        "gla_w_in": nrm(ks[1], (N_GLA_LAYERS, D_MODEL, GLA_IN), D_MODEL ** -0.5),
        "gla_w_gate_up": nrm(ks[2], (N_GLA_LAYERS, GLA_GATE_RANK, GLA_DK), GLA_GATE_RANK ** -0.5),
        "gla_gate_bias": nrm(ks[3], (N_GLA_LAYERS, GLA_DK), 0.1),
        "gla_norm_g": 1.0 + nrm(ks[4], (N_GLA_LAYERS, GLA_HEAD_V), 0.02),
        "gla_w_out": nrm(ks[5], (N_GLA_LAYERS, GLA_DV, D_MODEL), GLA_DV ** -0.5 * DEEPNORM_BETA),
        "dil_w_in": nrm(ks[6], (N_DIL_LAYERS, D_MODEL, DIL_IN), D_MODEL ** -0.5),
        "dil_w_out": nrm(ks[7], (N_DIL_LAYERS, DIL_WIDTH, D_MODEL), DIL_WIDTH ** -0.5 * DEEPNORM_BETA),
        "ffn_w_up": nrm(ks[8], (DEPTH, D_MODEL, 2 * D_FF), D_MODEL ** -0.5),
        "ffn_conv_w": nrm(ks[9], (DEPTH, CONV_WIDTH, 2 * D_FF), CONV_WIDTH ** -0.5),
        "ffn_conv_b": nrm(ks[10], (DEPTH, 2 * D_FF), 0.02),
        "ffn_w_down": nrm(ks[11], (DEPTH, D_FF, D_MODEL), D_FF ** -0.5 * DEEPNORM_BETA),
        "ln_g": 1.0 + nrm(ks[12], (DEPTH, 2, D_MODEL), 0.02),
        "ln_b": nrm(ks[13], (DEPTH, 2, D_MODEL), 0.02),
    }


def reference(x, gla_w_in, gla_w_gate_up, gla_gate_bias, gla_norm_g, gla_w_out,
              dil_w_in, dil_w_out, ffn_w_up, ffn_conv_w, ffn_conv_b, ffn_w_down, ln_g, ln_b):
    for i in range(DEPTH):
        j = i // 2
        if i % 2 == 0:
            mix = gla_mixer(x, gla_w_in[j], gla_w_gate_up[j], gla_gate_bias[j], gla_norm_g[j], gla_w_out[j])
        else:
            mix = dilated_mixer(x, dil_w_in[j], dil_w_out[j])
        x = layer_norm(DEEPNORM_ALPHA * x + mix, ln_g[i, 0], ln_b[i, 0])
        ffn = conv_ffn(x, ffn_w_up[i], ffn_conv_w[i], ffn_conv_b[i], ffn_w_down[i])
        x = layer_norm(DEEPNORM_ALPHA * x + ffn, ln_g[i, 1], ln_b[i, 1])
    return x
```

```python
import functools

import jax
import jax.numpy as jnp
from jax import lax
from jax.experimental import pallas as pl
from jax.experimental.pallas import tpu as pltpu

F32 = jnp.float32
BF16 = jnp.bfloat16

D_MODEL = 2048
DEPTH = 4

GLA_HEADS = 4
GLA_DK = D_MODEL // 2
GLA_DV = D_MODEL
GLA_HEAD_K = GLA_DK // GLA_HEADS
GLA_HEAD_V = GLA_DV // GLA_HEADS
GLA_GATE_RANK = 16
GLA_GATE_NORMALIZER = 16.0
GLA_CHUNK = 64
GLA_SUBCHUNK = 16
GLA_MAIN = 2 * GLA_DK + 2 * GLA_DV

DIL_PATTERNS = ((128, 1), (512, 4), (2048, 16))
DIL_GROUPS = len(DIL_PATTERNS)
DIL_HEADS = 8
DIL_HEAD_DIM = 128
DIL_WIDTH = DIL_HEADS * DIL_HEAD_DIM
DIL_BLOCK = 128
DIL_IN = 3 * DIL_GROUPS * DIL_WIDTH

D_FF = 5504
CONV_WIDTH = 3

DEEPNORM_ALPHA = (2 * DEPTH) ** 0.25
LN_EPS = 1e-5
RMS_EPS = 1e-6

LANES = 128
FF_TILE = 512
D_FF_PAD = -(-D_FF // FF_TILE) * FF_TILE
VMEM_LIMIT = 56 * 1024 * 1024


def _params(*sem):
    return pltpu.CompilerParams(dimension_semantics=sem, vmem_limit_bytes=VMEM_LIMIT)


def _matmul_kernel(x_ref, w_ref, o_ref):
    o_ref[...] = jnp.dot(x_ref[...], w_ref[...], preferred_element_type=F32).astype(o_ref.dtype)


def _matmul(x, w, *, tm=1024, tn=1024):
    m, k = x.shape
    n = w.shape[1]
    tm = min(tm, m)
    return pl.pallas_call(
        _matmul_kernel,
        out_shape=jax.ShapeDtypeStruct((m, n), BF16),
        grid=(m // tm, n // tn),
        in_specs=[pl.BlockSpec((tm, k), lambda i, j: (i, 0)),
                  pl.BlockSpec((k, tn), lambda i, j: (0, j))],
        out_specs=pl.BlockSpec((tm, tn), lambda i, j: (i, j)),
        compiler_params=_params("parallel", "arbitrary"),
        name="proj_matmul",
    )(x, w)


def _gla_proj_kernel(x_ref, w_ref, wg_ref, o_ref, g_ref):
    x = x_ref[...]
    o_ref[...] = jnp.dot(x, w_ref[...], preferred_element_type=F32).astype(o_ref.dtype)

    @pl.when(pl.program_id(1) == 0)
    def _():
        g_ref[...] = jnp.dot(x, wg_ref[...], preferred_element_type=F32)


def _gla_proj(x, w_main, w_gate, *, tm=1024, tn=1024):
    m, k = x.shape
    n = w_main.shape[1]
    tm = min(tm, m)
    return pl.pallas_call(
        _gla_proj_kernel,
        out_shape=(jax.ShapeDtypeStruct((m, n), BF16), jax.ShapeDtypeStruct((m, LANES), F32)),
        grid=(m // tm, n // tn),
        in_specs=[pl.BlockSpec((tm, k), lambda i, j: (i, 0)),
                  pl.BlockSpec((k, tn), lambda i, j: (0, j)),
                  pl.BlockSpec((k, LANES), lambda i, j: (0, 0))],
        out_specs=(pl.BlockSpec((tm, tn), lambda i, j: (i, j)),
                   pl.BlockSpec((tm, LANES), lambda i, j: (i, 0))),
        compiler_params=_params("parallel", "arbitrary"),
        name="gla_proj",
    )(x, w_main, w_gate)


def _gla_core_kernel(q_ref, k_ref, v_ref, r_ref, gl_ref, wgu_ref, gb_ref, ng_ref, o_ref, st_ref, *, nchunks):
    C, c = GLA_CHUNK, GLA_SUBCHUNK
    nsub = C // c
    dk = GLA_HEAD_K
    st_ref[...] = jnp.zeros_like(st_ref)

    rows = lax.broadcasted_iota(jnp.int32, (C, C), 0)
    cols = lax.broadcasted_iota(jnp.int32, (C, C), 1)
    tri = (rows >= cols).astype(F32)
    sub_rows = lax.broadcasted_iota(jnp.int32, (c, C), 0)
    sub_cols = lax.broadcasted_iota(jnp.int32, (c, C), 1)
    wgu = wgu_ref[...]
    gb = gb_ref[...]
    ng = ng_ref[...]
    nt = (((1,), (1,)), ((), ()))
    tn = (((0,), (0,)), ((), ()))

    def chunk(ci, carry):
        r0 = pl.multiple_of(ci * C, C)
        q = q_ref[pl.ds(r0, C), :].astype(F32) * (dk ** -0.5)
        k = k_ref[pl.ds(r0, C), :].astype(F32)
        v = v_ref[pl.ds(r0, C), :]
        z = jnp.dot(gl_ref[pl.ds(r0, C), :], wgu, preferred_element_type=F32,
                    precision=lax.Precision.HIGHEST) + gb
        g = (jnp.minimum(z, 0.0) - jnp.log(1.0 + jnp.exp(-jnp.abs(z)))) * (1.0 / GLA_GATE_NORMALIZER)
        b = jnp.dot(tri, g, preferred_element_type=F32, precision=lax.Precision.HIGHEST)
        b_last = b[C - 1:C, :]

        st = st_ref[...]
        o = lax.dot_general((q * jnp.exp(b)).astype(BF16), st.astype(BF16), nt, preferred_element_type=F32)

        a_parts = []
        for s in range(nsub):
            lo = s * c
            bs = b[lo:lo + c, :]
            qs = q[lo:lo + c, :]
            ks = k[lo:lo + c, :]
            if s == 0:
                a_s = jnp.zeros((c, C), F32)
            else:
                b_ref = b[lo - 1:lo, :]
                q_ref_s = (qs * jnp.exp(bs - b_ref)).astype(BF16)
                k_ref_s = (k[:lo, :] * jnp.exp(b_ref - b[:lo, :])).astype(BF16)
                k_ref_s = jnp.concatenate([k_ref_s, jnp.zeros((C - lo, dk), BF16)], axis=0)
                a_s = lax.dot_general(q_ref_s, k_ref_s, nt, preferred_element_type=F32)
            for j in range(c):
                e = jnp.exp(jnp.minimum(bs - bs[j:j + 1, :], 0.0))
                col = jnp.sum(qs * ks[j:j + 1, :] * e, axis=-1, keepdims=True)
                a_s = jnp.where((sub_cols == lo + j) & (sub_rows >= j), col, a_s)
            a_parts.append(a_s)
        a = jnp.concatenate(a_parts, axis=0).astype(BF16)
        o = o + jnp.dot(a, v, preferred_element_type=F32)

        kd = (k * jnp.exp(b_last - b)).astype(BF16)
        st_ref[...] = st * jnp.exp(b_last) + lax.dot_general(v, kd, tn, preferred_element_type=F32)

        o = o * lax.rsqrt(jnp.mean(o * o, axis=-1, keepdims=True) + RMS_EPS) * ng
        r = r_ref[pl.ds(r0, C), :].astype(F32)
        o_ref[pl.ds(r0, C), :] = (o * (r / (1.0 + jnp.exp(-r)))).astype(o_ref.dtype)
        return carry

    lax.fori_loop(0, nchunks, chunk, 0)


def _gla_core(proj, g_low, w_gate_up, gate_bias, norm_g, batch, seq):
    dk, dv, H = GLA_HEAD_K, GLA_HEAD_V, GLA_HEADS
    q_blk = lambda off: pl.BlockSpec((None, seq, dk), lambda b, h: (b, 0, off // dk + h))
    v_blk = lambda off: pl.BlockSpec((None, seq, dv), lambda b, h: (b, 0, off // dv + h))
    return pl.pallas_call(
        functools.partial(_gla_core_kernel, nchunks=seq // GLA_CHUNK),
        out_shape=jax.ShapeDtypeStruct((batch, seq, GLA_DV), BF16),
        grid=(batch, H),
        in_specs=[q_blk(0), q_blk(GLA_DK), v_blk(2 * GLA_DK), v_blk(2 * GLA_DK + GLA_DV),
                  pl.BlockSpec((None, seq, LANES), lambda b, h: (b, 0, 0)),
                  pl.BlockSpec((LANES, dk), lambda b, h: (0, h)),
                  pl.BlockSpec((1, dk), lambda b, h: (0, h)),
                  pl.BlockSpec((1, dv), lambda b, h: (0, 0))],
        out_specs=pl.BlockSpec((None, seq, dv), lambda b, h: (b, 0, h)),
        scratch_shapes=[pltpu.VMEM((dv, dk), F32)],
        compiler_params=_params("parallel", "arbitrary"),
        name="gla_core",
    )(proj, proj, proj, proj, g_low, w_gate_up, gate_bias, norm_g)


LN_ROWS = 64


def _mm_res_ln_kernel(a_ref, w_ref, x_ref, g_ref, b_ref, of_ref, ob_ref, acc_ref, *, nk):
    kk = pl.program_id(1)
    part = jnp.dot(a_ref[...], w_ref[...], preferred_element_type=F32)
    if nk == 1:
        acc_ref[...] = part
    else:
        @pl.when(kk == 0)
        def _():
            acc_ref[...] = part

        @pl.when(kk > 0)
        def _():
            acc_ref[...] += part

    @pl.when(kk == nk - 1)
    def _():
        gamma = g_ref[...]
        beta = b_ref[...]

        def rows(i, carry):
            r0 = pl.multiple_of(i * LN_ROWS, LN_ROWS)
            y = DEEPNORM_ALPHA * x_ref[pl.ds(r0, LN_ROWS), :] + acc_ref[pl.ds(r0, LN_ROWS), :]
            mu = jnp.mean(y, axis=-1, keepdims=True)
            yc = y - mu
            var = jnp.mean(yc * yc, axis=-1, keepdims=True)
            out = yc * lax.rsqrt(var + LN_EPS) * gamma + beta
            of_ref[pl.ds(r0, LN_ROWS), :] = out
            ob_ref[pl.ds(r0, LN_ROWS), :] = out.astype(BF16)
            return carry

        lax.fori_loop(0, acc_ref.shape[0] // LN_ROWS, rows, 0)


def _mm_res_ln(a, w, x, gamma, beta, *, tm=512, tk=512):
    m, k = a.shape
    n = w.shape[1]
    nk = k // tk
    return pl.pallas_call(
        functools.partial(_mm_res_ln_kernel, nk=nk),
        out_shape=(jax.ShapeDtypeStruct((m, n), F32), jax.ShapeDtypeStruct((m, n), BF16)),
        grid=(m // tm, nk),
        in_specs=[pl.BlockSpec((tm, tk), lambda i, kk: (i, kk)),
                  pl.BlockSpec((tk, n), lambda i, kk: (kk, 0)),
                  pl.BlockSpec((tm, n), lambda i, kk: (i, 0)),
                  pl.BlockSpec((1, n), lambda i, kk: (0, 0)),
                  pl.BlockSpec((1, n), lambda i, kk: (0, 0))],
        out_specs=(pl.BlockSpec((tm, n), lambda i, kk: (i, 0)),
                   pl.BlockSpec((tm, n), lambda i, kk: (i, 0))),
        scratch_shapes=[pltpu.VMEM((tm, n), F32)],
        compiler_params=_params("parallel", "arbitrary"),
        name="mm_res_ln",
    )(a, w, x, gamma, beta)


FFN_ROWS = 64
HALO = 8


def _ffn_up_kernel(x_ref, wg_ref, wu_ref, cwg_ref, cwu_ref, cbg_ref, cbu_ref, o_ref, hg_ref, hu_ref, *, seq):
    x = x_ref[...]
    tf = o_ref.shape[-1]
    hg_ref[pl.ds(0, HALO), :] = jnp.zeros((HALO, tf), F32)
    hu_ref[pl.ds(0, HALO), :] = jnp.zeros((HALO, tf), F32)
    hg_ref[pl.ds(HALO, seq), :] = jnp.dot(x, wg_ref[...], preferred_element_type=F32)
    hu_ref[pl.ds(HALO, seq), :] = jnp.dot(x, wu_ref[...], preferred_element_type=F32)
    cwg = cwg_ref[...]
    cwu = cwu_ref[...]
    cbg = cbg_ref[...]
    cbu = cbu_ref[...]

    def conv(h_ref, r0, cw, cb):
        win = h_ref[pl.ds(r0, FFN_ROWS + HALO), :]
        h0 = win[HALO:, :]
        h1 = pltpu.roll(win, 1, axis=0)[HALO:, :]
        h2 = pltpu.roll(win, 2, axis=0)[HALO:, :]
        return cw[2:3, :] * h0 + cw[1:2, :] * h1 + cw[0:1, :] * h2 + cb

    def rows(i, carry):
        r0 = pl.multiple_of(i * FFN_ROWS, FFN_ROWS)
        g = conv(hg_ref, r0, cwg, cbg)
        u = conv(hu_ref, r0, cwu, cbu)
        o_ref[pl.ds(r0, FFN_ROWS), :] = (g / (1.0 + jnp.exp(-g)) * u).astype(o_ref.dtype)
        return carry

    lax.fori_loop(0, seq // FFN_ROWS, rows, 0)


def _ffn_up(x, wg, wu, cwg, cwu, cbg, cbu, batch, seq):
    k = x.shape[1]
    tf = FF_TILE
    w_blk = pl.BlockSpec((k, tf), lambda b, j: (0, j))
    c_blk = pl.BlockSpec((CONV_WIDTH, tf), lambda b, j: (0, j))
    b_blk = pl.BlockSpec((1, tf), lambda b, j: (0, j))
    return pl.pallas_call(
        functools.partial(_ffn_up_kernel, seq=seq),
        out_shape=jax.ShapeDtypeStruct((batch * seq, D_FF_PAD), BF16),
        grid=(batch, D_FF_PAD // tf),
        in_specs=[pl.BlockSpec((seq, k), lambda b, j: (b, 0)), w_blk, w_blk, c_blk, c_blk, b_blk, b_blk],
        out_specs=pl.BlockSpec((seq, tf), lambda b, j: (b, j)),
        scratch_shapes=[pltpu.VMEM((seq + HALO, tf), F32), pltpu.VMEM((seq + HALO, tf), F32)],
        compiler_params=_params("parallel", "arbitrary"),
        name="ffn_up",
    )(x, wg, wu, cwg, cwu, cbg, cbu)


NEG = -1e30


def _band_block(q, k_prev, v_prev, k_cur, v_cur, steps):
    P = q.shape[0]
    nt = (((1,), (1,)), ((), ()))
    scale = DIL_HEAD_DIM ** -0.5
    qi = lax.broadcasted_iota(jnp.int32, (P, P), 0)
    kj = lax.broadcasted_iota(jnp.int32, (P, P), 1)
    s_cur = lax.dot_general(q, k_cur, nt, preferred_element_type=F32) * scale
    s_cur = jnp.where((qi >= kj) & (qi - kj <= steps), s_cur, NEG)
    m = jnp.max(s_cur, axis=-1, keepdims=True)
    if k_prev is not None:
        s_prev = lax.dot_general(q, k_prev, nt, preferred_element_type=F32) * scale
        s_prev = jnp.where(qi + P - kj <= steps, s_prev, NEG)
        m = jnp.maximum(m, jnp.max(s_prev, axis=-1, keepdims=True))
    p_cur = jnp.exp(s_cur - m)
    l = jnp.sum(p_cur, axis=-1, keepdims=True)
    acc = jnp.dot(p_cur.astype(BF16), v_cur, preferred_element_type=F32)
    if k_prev is not None:
        p_prev = jnp.exp(s_prev - m)
        l = l + jnp.sum(p_prev, axis=-1, keepdims=True)
        acc = acc + jnp.dot(p_prev.astype(BF16), v_prev, preferred_element_type=F32)
    return m, l, acc


def _dil_attn_kernel(*refs, seq):
    qkv = refs[:3 * DIL_GROUPS]
    o_ref = refs[3 * DIL_GROUPS]
    m_ref, l_ref, acc_ref = refs[3 * DIL_GROUPS + 1:]
    P = DIL_BLOCK
    first = True
    for gi, (window, dil) in enumerate(DIL_PATTERNS):
        q_ref, k_ref, v_ref = qkv[3 * gi:3 * gi + 3]
        steps = window // dil
        L = seq // dil
        for r in range(dil):
            for lb in range(L // P):
                lo = lb * P
                q = q_ref[r, lo:lo + P, :]
                k_cur = k_ref[r, lo:lo + P, :]
                v_cur = v_ref[r, lo:lo + P, :]
                if lb > 0:
                    k_prev = k_ref[r, lo - P:lo, :]
                    v_prev = v_ref[r, lo - P:lo, :]
                else:
                    k_prev = v_prev = None
                m, l, acc = _band_block(q, k_prev, v_prev, k_cur, v_cur, steps)
                m = jnp.broadcast_to(m, (P, LANES))
                l = jnp.broadcast_to(l, (P, LANES))
                tok = pl.ds(lo * dil + r, P, stride=dil) if dil > 1 else pl.ds(lo, P)
                if first:
                    m_ref[tok, :] = m
                    l_ref[tok, :] = l
                    acc_ref[tok, :] = acc
                else:
                    m_old = m_ref[tok, :]
                    m_new = jnp.maximum(m_old, m)
                    w_old = jnp.exp(m_old - m_new)
                    w_new = jnp.exp(m - m_new)
                    m_ref[tok, :] = m_new
                    l_ref[tok, :] = l_ref[tok, :] * w_old + l * w_new
                    acc_ref[tok, :] = acc_ref[tok, :] * w_old + acc * w_new
        first = False
    o_ref[...] = (acc_ref[...] / l_ref[...]).astype(o_ref.dtype)


def _dil_attn(groups, batch, seq):
    dh = DIL_HEAD_DIM
    in_specs, args = [], []
    for (window, dil), qkv in zip(DIL_PATTERNS, groups):
        for t in qkv:
            in_specs.append(pl.BlockSpec((None, None, dil, seq // dil, dh), lambda b, h: (b, h, 0, 0, 0)))
            args.append(t)
    return pl.pallas_call(
        functools.partial(_dil_attn_kernel, seq=seq),
        out_shape=jax.ShapeDtypeStruct((batch, seq, DIL_WIDTH), BF16),
        grid=(batch, DIL_HEADS),
        in_specs=in_specs,
        out_specs=pl.BlockSpec((None, seq, dh), lambda b, h: (b, 0, h)),
        scratch_shapes=[pltpu.VMEM((seq, LANES), F32), pltpu.VMEM((seq, LANES), F32),
                        pltpu.VMEM((seq, dh), F32)],
        compiler_params=_params("parallel", "parallel"),
        name="dil_attn",
    )(*args)


def _pad_cols(w, n):
    return jnp.pad(w, ((0, 0), (0, n - w.shape[1])))


def _gla_layer(xf, xb, w_in, w_gate_up, gate_bias, norm_g, w_out, gamma, beta, batch, seq):
    w_main = w_in[:, :GLA_MAIN].astype(BF16)
    w_gate = _pad_cols(w_in[:, GLA_MAIN:], LANES).astype(BF16)
    proj, g_low = _gla_proj(xb, w_main, w_gate)
    wgu = jnp.pad(w_gate_up, ((0, LANES - GLA_GATE_RANK), (0, 0)))
    o = _gla_core(proj.reshape(batch, seq, GLA_MAIN), g_low.reshape(batch, seq, LANES),
                  wgu, gate_bias[None, :], norm_g[None, :], batch, seq)
    return _mm_res_ln(o.reshape(batch * seq, GLA_DV), w_out.astype(BF16), xf, gamma[None, :], beta[None, :],
                      tk=GLA_DV)


def _dil_layer(xf, xb, w_in, w_out, gamma, beta, batch, seq):
    proj = _matmul(xb, w_in.astype(BF16))
    proj = proj.reshape(batch, seq, DIL_GROUPS, 3, DIL_HEADS, DIL_HEAD_DIM)
    groups = []
    for gi, (window, dil) in enumerate(DIL_PATTERNS):
        qkv = []
        for c in range(3):
            t = proj[:, :, gi, c].reshape(batch, seq // dil, dil, DIL_HEADS, DIL_HEAD_DIM)
            qkv.append(t.transpose(0, 3, 2, 1, 4))
        groups.append(qkv)
    o = _dil_attn(groups, batch, seq)
    return _mm_res_ln(o.reshape(batch * seq, DIL_WIDTH), w_out.astype(BF16), xf, gamma[None, :], beta[None, :],
                      tk=DIL_WIDTH)


def _ffn_layer(xf, xb, w_up, conv_w, conv_b, w_down, gamma, beta, batch, seq):
    wg = _pad_cols(w_up[:, :D_FF], D_FF_PAD).astype(BF16)
    wu = _pad_cols(w_up[:, D_FF:], D_FF_PAD).astype(BF16)
    cwg = _pad_cols(conv_w[:, :D_FF], D_FF_PAD)
    cwu = _pad_cols(conv_w[:, D_FF:], D_FF_PAD)
    cbg = _pad_cols(conv_b[None, :D_FF], D_FF_PAD)
    cbu = _pad_cols(conv_b[None, D_FF:], D_FF_PAD)
    act = _ffn_up(xb, wg, wu, cwg, cwu, cbg, cbu, batch, seq)
    wd = jnp.pad(w_down, ((0, D_FF_PAD - D_FF), (0, 0))).astype(BF16)
    return _mm_res_ln(act, wd, xf, gamma[None, :], beta[None, :], tk=FF_TILE)


def kernel(x, gla_w_in, gla_w_gate_up, gla_gate_bias, gla_norm_g, gla_w_out, dil_w_in, dil_w_out, ffn_w_up,
           ffn_conv_w, ffn_conv_b, ffn_w_down, ln_g, ln_b):
    batch, seq, d = x.shape
    xf = x.reshape(batch * seq, d)
    xb = xf.astype(BF16)
    for i in range(DEPTH):
        j = i // 2
        if i % 2 == 0:
            xf, xb = _gla_layer(xf, xb, gla_w_in[j], gla_w_gate_up[j], gla_gate_bias[j], gla_norm_g[j],
                                gla_w_out[j], ln_g[i, 0], ln_b[i, 0], batch, seq)
        else:
            xf, xb = _dil_layer(xf, xb, dil_w_in[j], dil_w_out[j], ln_g[i, 0], ln_b[i, 0], batch, seq)
        xf, xb = _ffn_layer(xf, xb, ffn_w_up[i], ffn_conv_w[i], ffn_conv_b[i], ffn_w_down[i],
                            ln_g[i, 1], ln_b[i, 1], batch, seq)
    return xf.reshape(batch, seq, d)
```

```python
import functools

import jax
import jax.numpy as jnp
from jax import lax
from jax.experimental import pallas as pl
from jax.experimental.pallas import tpu as pltpu

F32 = jnp.float32
BF16 = jnp.bfloat16

D_MODEL = 2048
DEPTH = 4

GLA_HEADS = 4
GLA_DK = D_MODEL // 2
GLA_DV = D_MODEL
GLA_HEAD_K = GLA_DK // GLA_HEADS
GLA_HEAD_V = GLA_DV // GLA_HEADS
GLA_GATE_RANK = 16
GLA_GATE_NORMALIZER = 16.0
GLA_CHUNK = 64
GLA_SUBCHUNK = 16
GLA_MAIN = 2 * GLA_DK + 2 * GLA_DV

DIL_PATTERNS = ((128, 1), (512, 4), (2048, 16))
DIL_GROUPS = len(DIL_PATTERNS)
DIL_HEADS = 8
DIL_HEAD_DIM = 128
DIL_WIDTH = DIL_HEADS * DIL_HEAD_DIM
DIL_BLOCK = 128
DIL_IN = 3 * DIL_GROUPS * DIL_WIDTH

D_FF = 5504
CONV_WIDTH = 3

DEEPNORM_ALPHA = (2 * DEPTH) ** 0.25
LN_EPS = 1e-5
RMS_EPS = 1e-6

LANES = 128
FF_TILE = 512
D_FF_PAD = -(-D_FF // FF_TILE) * FF_TILE
VMEM_LIMIT = 56 * 1024 * 1024


def _params(*sem):
    return pltpu.CompilerParams(dimension_semantics=sem, vmem_limit_bytes=VMEM_LIMIT)


def _matmul_kernel(x_ref, w_ref, o_ref):
    o_ref[...] = jnp.dot(x_ref[...], w_ref[...], preferred_element_type=F32).astype(o_ref.dtype)


def _matmul(x, w, *, tm=1024, tn=1024):
    m, k = x.shape
    n = w.shape[1]
    tm = min(tm, m)
    return pl.pallas_call(
        _matmul_kernel,
        out_shape=jax.ShapeDtypeStruct((m, n), BF16),
        grid=(m // tm, n // tn),
        in_specs=[pl.BlockSpec((tm, k), lambda i, j: (i, 0)),
                  pl.BlockSpec((k, tn), lambda i, j: (0, j))],
        out_specs=pl.BlockSpec((tm, tn), lambda i, j: (i, j)),
        compiler_params=_params("parallel", "arbitrary"),
        name="proj_matmul",
    )(x, w)


def _gla_proj_kernel(x_ref, w_ref, wg_ref, o_ref, g_ref):
    x = x_ref[...]
    o_ref[...] = jnp.dot(x, w_ref[...], preferred_element_type=F32).astype(o_ref.dtype)

    @pl.when(pl.program_id(1) == 0)
    def _():
        g_ref[...] = jnp.dot(x, wg_ref[...], preferred_element_type=F32)


def _gla_proj(x, w_main, w_gate, *, tm=1024, tn=1024):
    m, k = x.shape
    n = w_main.shape[1]
    tm = min(tm, m)
    return pl.pallas_call(
        _gla_proj_kernel,
        out_shape=(jax.ShapeDtypeStruct((m, n), BF16), jax.ShapeDtypeStruct((m, LANES), F32)),
        grid=(m // tm, n // tn),
        in_specs=[pl.BlockSpec((tm, k), lambda i, j: (i, 0)),
                  pl.BlockSpec((k, tn), lambda i, j: (0, j)),
                  pl.BlockSpec((k, LANES), lambda i, j: (0, 0))],
        out_specs=(pl.BlockSpec((tm, tn), lambda i, j: (i, j)),
                   pl.BlockSpec((tm, LANES), lambda i, j: (i, 0))),
        compiler_params=_params("parallel", "arbitrary"),
        name="gla_proj",
    )(x, w_main, w_gate)


def _gla_core_kernel(q_ref, k_ref, v_ref, r_ref, gl_ref, wgu_ref, gb_ref, ng_ref, o_ref, st_ref, *, nchunks):
    C, c = GLA_CHUNK, GLA_SUBCHUNK
    nsub = C // c
    dk = GLA_HEAD_K
    st_ref[...] = jnp.zeros_like(st_ref)

    rows = lax.broadcasted_iota(jnp.int32, (C, C), 0)
    cols = lax.broadcasted_iota(jnp.int32, (C, C), 1)
    tri = (rows >= cols).astype(F32)
    sub_rows = lax.broadcasted_iota(jnp.int32, (c, C), 0)
    sub_cols = lax.broadcasted_iota(jnp.int32, (c, C), 1)
    wgu = wgu_ref[...]
    gb = gb_ref[...]
    ng = ng_ref[...]
    nt = (((1,), (1,)), ((), ()))
    tn = (((0,), (0,)), ((), ()))

    def chunk(ci, carry):
        r0 = pl.multiple_of(ci * C, C)
        q = q_ref[pl.ds(r0, C), :].astype(F32) * (dk ** -0.5)
        k = k_ref[pl.ds(r0, C), :].astype(F32)
        v = v_ref[pl.ds(r0, C), :]
        z = jnp.dot(gl_ref[pl.ds(r0, C), :], wgu, preferred_element_type=F32,
                    precision=lax.Precision.HIGHEST) + gb
        g = (jnp.minimum(z, 0.0) - jnp.log(1.0 + jnp.exp(-jnp.abs(z)))) * (1.0 / GLA_GATE_NORMALIZER)
        b = jnp.dot(tri, g, preferred_element_type=F32, precision=lax.Precision.HIGHEST)
        b_last = b[C - 1:C, :]

        st = st_ref[...]
        o = lax.dot_general((q * jnp.exp(b)).astype(BF16), st.astype(BF16), nt, preferred_element_type=F32)

        a_parts = []
        for s in range(nsub):
            lo = s * c
            bs = b[lo:lo + c, :]
            qs = q[lo:lo + c, :]
            ks = k[lo:lo + c, :]
            if s == 0:
                a_s = jnp.zeros((c, C), F32)
            else:
                b_ref = b[lo - 1:lo, :]
                q_ref_s = (qs * jnp.exp(bs - b_ref)).astype(BF16)
                k_ref_s = (k[:lo, :] * jnp.exp(b_ref - b[:lo, :])).astype(BF16)
                k_ref_s = jnp.concatenate([k_ref_s, jnp.zeros((C - lo, dk), BF16)], axis=0)
                a_s = lax.dot_general(q_ref_s, k_ref_s, nt, preferred_element_type=F32)
            for j in range(c):
                e = jnp.exp(jnp.minimum(bs - bs[j:j + 1, :], 0.0))
                col = jnp.sum(qs * ks[j:j + 1, :] * e, axis=-1, keepdims=True)
                a_s = jnp.where((sub_cols == lo + j) & (sub_rows >= j), col, a_s)
            a_parts.append(a_s)
        a = jnp.concatenate(a_parts, axis=0).astype(BF16)
        o = o + jnp.dot(a, v, preferred_element_type=F32)

        kd = (k * jnp.exp(b_last - b)).astype(BF16)
        st_ref[...] = st * jnp.exp(b_last) + lax.dot_general(v, kd, tn, preferred_element_type=F32)

        o = o * lax.rsqrt(jnp.mean(o * o, axis=-1, keepdims=True) + RMS_EPS) * ng
        r = r_ref[pl.ds(r0, C), :].astype(F32)
        o_ref[pl.ds(r0, C), :] = (o * (r / (1.0 + jnp.exp(-r)))).astype(o_ref.dtype)
        return carry

    lax.fori_loop(0, nchunks, chunk, 0)


def _gla_core(proj, g_low, w_gate_up, gate_bias, norm_g, batch, seq):
    dk, dv, H = GLA_HEAD_K, GLA_HEAD_V, GLA_HEADS
    q_blk = lambda off: pl.BlockSpec((None, seq, dk), lambda b, h: (b, 0, off // dk + h))
    v_blk = lambda off: pl.BlockSpec((None, seq, dv), lambda b, h: (b, 0, off // dv + h))
    return pl.pallas_call(
        functools.partial(_gla_core_kernel, nchunks=seq // GLA_CHUNK),
        out_shape=jax.ShapeDtypeStruct((batch, seq, GLA_DV), BF16),
        grid=(batch, H),
        in_specs=[q_blk(0), q_blk(GLA_DK), v_blk(2 * GLA_DK), v_blk(2 * GLA_DK + GLA_DV),
                  pl.BlockSpec((None, seq, LANES), lambda b, h: (b, 0, 0)),
                  pl.BlockSpec((LANES, dk), lambda b, h: (0, h)),
                  pl.BlockSpec((1, dk), lambda b, h: (0, h)),
                  pl.BlockSpec((1, dv), lambda b, h: (0, 0))],
        out_specs=pl.BlockSpec((None, seq, dv), lambda b, h: (b, 0, h)),
        scratch_shapes=[pltpu.VMEM((dv, dk), F32)],
        compiler_params=_params("parallel", "arbitrary"),
        name="gla_core",
    )(proj, proj, proj, proj, g_low, w_gate_up, gate_bias, norm_g)


LN_ROWS = 64


def _mm_res_ln_kernel(a_ref, w_ref, x_ref, g_ref, b_ref, of_ref, ob_ref, acc_ref, *, nk):
    kk = pl.program_id(1)
    part = jnp.dot(a_ref[...], w_ref[...], preferred_element_type=F32)
    if nk == 1:
        acc_ref[...] = part
    else:
        @pl.when(kk == 0)
        def _():
            acc_ref[...] = part

        @pl.when(kk > 0)
        def _():
            acc_ref[...] += part

    @pl.when(kk == nk - 1)
    def _():
        gamma = g_ref[...]
        beta = b_ref[...]

        def rows(i, carry):
            r0 = pl.multiple_of(i * LN_ROWS, LN_ROWS)
            y = DEEPNORM_ALPHA * x_ref[pl.ds(r0, LN_ROWS), :] + acc_ref[pl.ds(r0, LN_ROWS), :]
            mu = jnp.mean(y, axis=-1, keepdims=True)
            yc = y - mu
            var = jnp.mean(yc * yc, axis=-1, keepdims=True)
            out = yc * lax.rsqrt(var + LN_EPS) * gamma + beta
            of_ref[pl.ds(r0, LN_ROWS), :] = out
            ob_ref[pl.ds(r0, LN_ROWS), :] = out.astype(BF16)
            return carry

        lax.fori_loop(0, acc_ref.shape[0] // LN_ROWS, rows, 0)


def _mm_res_ln(a, w, x, gamma, beta, *, tm=512, tk=512):
    m, k = a.shape
    n = w.shape[1]
    nk = k // tk
    return pl.pallas_call(
        functools.partial(_mm_res_ln_kernel, nk=nk),
        out_shape=(jax.ShapeDtypeStruct((m, n), F32), jax.ShapeDtypeStruct((m, n), BF16)),
        grid=(m // tm, nk),
        in_specs=[pl.BlockSpec((tm, tk), lambda i, kk: (i, kk)),
                  pl.BlockSpec((tk, n), lambda i, kk: (kk, 0)),
                  pl.BlockSpec((tm, n), lambda i, kk: (i, 0)),
                  pl.BlockSpec((1, n), lambda i, kk: (0, 0)),
                  pl.BlockSpec((1, n), lambda i, kk: (0, 0))],
        out_specs=(pl.BlockSpec((tm, n), lambda i, kk: (i, 0)),
                   pl.BlockSpec((tm, n), lambda i, kk: (i, 0))),
        scratch_shapes=[pltpu.VMEM((tm, n), F32)],
        compiler_params=_params("parallel", "arbitrary"),
        name="mm_res_ln",
    )(a, w, x, gamma, beta)


FFN_TM = 1024
FFN_CHUNK = 256
FFN_PIECE = 32
HALO = 8


def _ffn_kernel(x_ref, wup_ref, cw_ref, cb_ref, wd_ref, xres_ref, g_ref, b_ref, of_ref, ob_ref,
                h0_ref, h1_ref, halo_ref, act_ref, *, nj, tiles_per_seq):
    i = pl.program_id(0)
    j = pl.program_id(1)
    tm = x_ref.shape[0]
    tf = wd_ref.shape[0]
    nchunk = tm // FFN_CHUNK

    def up_chunk(c, h_ref):
        r0 = c * FFN_CHUNK
        h_ref[pl.ds(HALO + r0, FFN_CHUNK), :] = jnp.dot(x_ref[pl.ds(r0, FFN_CHUNK), :], wup_ref[...],
                                                        preferred_element_type=F32)

    def act_chunk(c, h_ref):
        cw = cw_ref[...]
        cb = cb_ref[...]
        for p in range(FFN_CHUNK // FFN_PIECE):
            r0 = c * FFN_CHUNK + p * FFN_PIECE
            win = h_ref[pl.ds(r0, FFN_PIECE + HALO), :]
            y = (cw[2:3, :] * win[HALO:, :] + cw[1:2, :] * pltpu.roll(win, 1, axis=0)[HALO:, :]
                 + cw[0:1, :] * pltpu.roll(win, 2, axis=0)[HALO:, :] + cb)
            gate = y[:, :tf]
            act_ref[pl.ds(r0, FFN_PIECE), :] = (gate / (1.0 + jnp.exp(-gate)) * y[:, tf:]).astype(BF16)
        return act_ref[pl.ds(c * FFN_CHUNK, FFN_CHUNK), :]

    def set_halo(h_ref):
        @pl.when(i % tiles_per_seq == 0)
        def _():
            h_ref[pl.ds(0, HALO), :] = jnp.zeros((HALO, 2 * tf), F32)

        @pl.when(i % tiles_per_seq != 0)
        def _():
            h_ref[pl.ds(0, HALO), :] = halo_ref[j]

    def save_halo(h_ref):
        halo_ref[j] = h_ref[pl.ds(tm, HALO), :]

    def first_step(h_ref):
        set_halo(h_ref)
        for c in range(nchunk):
            up_chunk(c, h_ref)
            of_ref[pl.ds(c * FFN_CHUNK, FFN_CHUNK), :] = jnp.zeros((FFN_CHUNK, of_ref.shape[1]), F32)
        save_halo(h_ref)

    def mid_step(h_up, h_prev):
        set_halo(h_up)
        for c in range(nchunk):
            up_chunk(c, h_up)
            a = act_chunk(c, h_prev)
            of_ref[pl.ds(c * FFN_CHUNK, FFN_CHUNK), :] += jnp.dot(a, wd_ref[...], preferred_element_type=F32)
        save_halo(h_up)

    def last_step(h_prev):
        gamma = g_ref[...]
        beta = b_ref[...]
        for c in range(nchunk):
            a = act_chunk(c, h_prev)
            of_ref[pl.ds(c * FFN_CHUNK, FFN_CHUNK), :] += jnp.dot(a, wd_ref[...], preferred_element_type=F32)
            for p in range(FFN_CHUNK // FFN_PIECE):
                rows = pl.ds(c * FFN_CHUNK + p * FFN_PIECE, FFN_PIECE)
                y = DEEPNORM_ALPHA * xres_ref[rows, :] + of_ref[rows, :]
                mu = jnp.mean(y, axis=-1, keepdims=True)
                yc = y - mu
                var = jnp.mean(yc * yc, axis=-1, keepdims=True)
                out = yc * lax.rsqrt(var + LN_EPS) * gamma + beta
                of_ref[rows, :] = out
                ob_ref[rows, :] = out.astype(BF16)

    slots = (h0_ref, h1_ref)

    @pl.when(j == 0)
    def _():
        first_step(slots[0])

    for parity in (0, 1):
        @pl.when((j > 0) & (j < nj) & (j % 2 == parity))
        def _():
            mid_step(slots[parity], slots[1 - parity])

    @pl.when(j == nj)
    def _():
        last_step(slots[(nj - 1) % 2])


def _ffn(xb, xf, w_up, conv_w, conv_b, w_down, gamma, beta, seq):
    m, d = xb.shape
    tf = FF_TILE
    nj = w_down.shape[0] // tf
    tm = min(FFN_TM, seq)
    one = pl.Buffered(1)
    return pl.pallas_call(
        functools.partial(_ffn_kernel, nj=nj, tiles_per_seq=seq // tm),
        out_shape=(jax.ShapeDtypeStruct((m, d), F32), jax.ShapeDtypeStruct((m, d), BF16)),
        grid=(m // tm, nj + 1),
        in_specs=[pl.BlockSpec((tm, d), lambda i, j: (i, 0)),
                  pl.BlockSpec((d, 2 * tf), lambda i, j: (0, jnp.minimum(j, nj - 1))),
                  pl.BlockSpec((CONV_WIDTH, 2 * tf), lambda i, j: (0, jnp.maximum(j - 1, 0))),
                  pl.BlockSpec((1, 2 * tf), lambda i, j: (0, jnp.maximum(j - 1, 0))),
                  pl.BlockSpec((tf, d), lambda i, j: (jnp.maximum(j - 1, 0), 0)),
                  pl.BlockSpec((tm, d), lambda i, j: (i, 0), pipeline_mode=one),
                  pl.BlockSpec((1, d), lambda i, j: (0, 0)),
                  pl.BlockSpec((1, d), lambda i, j: (0, 0))],
        out_specs=(pl.BlockSpec((tm, d), lambda i, j: (i, 0), pipeline_mode=one),
                   pl.BlockSpec((tm, d), lambda i, j: (i, 0), pipeline_mode=one)),
        scratch_shapes=[pltpu.VMEM((tm + HALO, 2 * tf), F32), pltpu.VMEM((tm + HALO, 2 * tf), F32),
                        pltpu.VMEM((nj, HALO, 2 * tf), F32), pltpu.VMEM((tm, tf), BF16)],
        compiler_params=_params("arbitrary", "arbitrary"),
        name="ffn",
    )(xb, w_up, conv_w, conv_b, w_down, xf, gamma, beta)


NEG = -1e30


def _band_block(q, k_prev, v_prev, k_cur, v_cur, steps):
    P = q.shape[0]
    nt = (((1,), (1,)), ((), ()))
    scale = DIL_HEAD_DIM ** -0.5
    qi = lax.broadcasted_iota(jnp.int32, (P, P), 0)
    kj = lax.broadcasted_iota(jnp.int32, (P, P), 1)
    s_cur = lax.dot_general(q, k_cur, nt, preferred_element_type=F32) * scale
    s_cur = jnp.where((qi >= kj) & (qi - kj <= steps), s_cur, NEG)
    m = jnp.max(s_cur, axis=-1, keepdims=True)
    if k_prev is not None:
        s_prev = lax.dot_general(q, k_prev, nt, preferred_element_type=F32) * scale
        s_prev = jnp.where(qi + P - kj <= steps, s_prev, NEG)
        m = jnp.maximum(m, jnp.max(s_prev, axis=-1, keepdims=True))
    p_cur = jnp.exp(s_cur - m)
    l = jnp.sum(p_cur, axis=-1, keepdims=True)
    acc = jnp.dot(p_cur.astype(BF16), v_cur, preferred_element_type=F32)
    if k_prev is not None:
        p_prev = jnp.exp(s_prev - m)
        l = l + jnp.sum(p_prev, axis=-1, keepdims=True)
        acc = acc + jnp.dot(p_prev.astype(BF16), v_prev, preferred_element_type=F32)
    return m, l, acc


def _dil_attn_kernel(*refs, seq):
    qkv = refs[:3 * DIL_GROUPS]
    o_ref = refs[3 * DIL_GROUPS]
    m_ref, l_ref, acc_ref = refs[3 * DIL_GROUPS + 1:]
    P = DIL_BLOCK
    first = True
    for gi, (window, dil) in enumerate(DIL_PATTERNS):
        q_ref, k_ref, v_ref = qkv[3 * gi:3 * gi + 3]
        steps = window // dil
        L = seq // dil
        for r in range(dil):
            for lb in range(L // P):
                lo = lb * P
                q = q_ref[r, lo:lo + P, :]
                k_cur = k_ref[r, lo:lo + P, :]
                v_cur = v_ref[r, lo:lo + P, :]
                if lb > 0:
                    k_prev = k_ref[r, lo - P:lo, :]
                    v_prev = v_ref[r, lo - P:lo, :]
                else:
                    k_prev = v_prev = None
                m, l, acc = _band_block(q, k_prev, v_prev, k_cur, v_cur, steps)
                m = jnp.broadcast_to(m, (P, LANES))
                l = jnp.broadcast_to(l, (P, LANES))
                tok = pl.ds(lo * dil + r, P, stride=dil) if dil > 1 else pl.ds(lo, P)
                if first:
                    m_ref[tok, :] = m
                    l_ref[tok, :] = l
                    acc_ref[tok, :] = acc
                else:
                    m_old = m_ref[tok, :]
                    m_new = jnp.maximum(m_old, m)
                    w_old = jnp.exp(m_old - m_new)
                    w_new = jnp.exp(m - m_new)
                    m_ref[tok, :] = m_new
                    l_ref[tok, :] = l_ref[tok, :] * w_old + l * w_new
                    acc_ref[tok, :] = acc_ref[tok, :] * w_old + acc * w_new
        first = False
    o_ref[...] = (acc_ref[...] / l_ref[...]).astype(o_ref.dtype)


def _dil_attn(groups, batch, seq):
    dh = DIL_HEAD_DIM
    in_specs, args = [], []
    for (window, dil), qkv in zip(DIL_PATTERNS, groups):
        for t in qkv:
            in_specs.append(pl.BlockSpec((None, None, dil, seq // dil, dh), lambda b, h: (b, h, 0, 0, 0)))
            args.append(t)
    return pl.pallas_call(
        functools.partial(_dil_attn_kernel, seq=seq),
        out_shape=jax.ShapeDtypeStruct((batch, seq, DIL_WIDTH), BF16),
        grid=(batch, DIL_HEADS),
        in_specs=in_specs,
        out_specs=pl.BlockSpec((None, seq, dh), lambda b, h: (b, 0, h)),
        scratch_shapes=[pltpu.VMEM((seq, LANES), F32), pltpu.VMEM((seq, LANES), F32),
                        pltpu.VMEM((seq, dh), F32)],
        compiler_params=_params("parallel", "parallel"),
        name="dil_attn",
    )(*args)


def _pad_cols(w, n):
    return jnp.pad(w, ((0, 0), (0, n - w.shape[1])))


def _gla_layer(xf, xb, w_in, w_gate_up, gate_bias, norm_g, w_out, gamma, beta, batch, seq):
    w_main = w_in[:, :GLA_MAIN].astype(BF16)
    w_gate = _pad_cols(w_in[:, GLA_MAIN:], LANES).astype(BF16)
    proj, g_low = _gla_proj(xb, w_main, w_gate)
    wgu = jnp.pad(w_gate_up, ((0, LANES - GLA_GATE_RANK), (0, 0)))
    o = _gla_core(proj.reshape(batch, seq, GLA_MAIN), g_low.reshape(batch, seq, LANES),
                  wgu, gate_bias[None, :], norm_g[None, :], batch, seq)
    return _mm_res_ln(o.reshape(batch * seq, GLA_DV), w_out.astype(BF16), xf, gamma[None, :], beta[None, :],
                      tk=GLA_DV)


def _dil_layer(xf, xb, w_in, w_out, gamma, beta, batch, seq):
    proj = _matmul(xb, w_in.astype(BF16))
    proj = proj.reshape(batch, seq, DIL_GROUPS, 3, DIL_HEADS, DIL_HEAD_DIM)
    groups = []
    for gi, (window, dil) in enumerate(DIL_PATTERNS):
        qkv = []
        for c in range(3):
            t = proj[:, :, gi, c].reshape(batch, seq // dil, dil, DIL_HEADS, DIL_HEAD_DIM)
            qkv.append(t.transpose(0, 3, 2, 1, 4))
        groups.append(qkv)
    o = _dil_attn(groups, batch, seq)
    return _mm_res_ln(o.reshape(batch * seq, DIL_WIDTH), w_out.astype(BF16), xf, gamma[None, :], beta[None, :],
                      tk=DIL_WIDTH)


def _ffn_layer(xf, xb, w_up, conv_w, conv_b, w_down, gamma, beta, batch, seq):
    def tiled(t):
        r = t.shape[0]
        halves = [_pad_cols(h, D_FF_PAD).reshape(r, D_FF_PAD // FF_TILE, FF_TILE) for h in (t[:, :D_FF], t[:, D_FF:])]
        return jnp.concatenate(halves, axis=2).reshape(r, 2 * D_FF_PAD)

    wd = jnp.pad(w_down, ((0, D_FF_PAD - D_FF), (0, 0))).astype(BF16)
    return _ffn(xb, xf, tiled(w_up.astype(BF16)), tiled(conv_w), tiled(conv_b[None, :]), wd,
                gamma[None, :], beta[None, :], seq)


def kernel(x, gla_w_in, gla_w_gate_up, gla_gate_bias, gla_norm_g, gla_w_out, dil_w_in, dil_w_out, ffn_w_up,
           ffn_conv_w, ffn_conv_b, ffn_w_down, ln_g, ln_b):
    batch, seq, d = x.shape
    xf = x.reshape(batch * seq, d)
    xb = xf.astype(BF16)
    for i in range(DEPTH):
        j = i // 2
        if i % 2 == 0:
            xf, xb = _gla_layer(xf, xb, gla_w_in[j], gla_w_gate_up[j], gla_gate_bias[j], gla_norm_g[j],
                                gla_w_out[j], ln_g[i, 0], ln_b[i, 0], batch, seq)
        else:
            xf, xb = _dil_layer(xf, xb, dil_w_in[j], dil_w_out[j], ln_g[i, 0], ln_b[i, 0], batch, seq)
        xf, xb = _ffn_layer(xf, xb, ffn_w_up[i], ffn_conv_w[i], ffn_conv_b[i], ffn_w_down[i],
                            ln_g[i, 1], ln_b[i, 1], batch, seq)
    return xf.reshape(batch, seq, d)
```

```python
import functools

import jax
import jax.numpy as jnp
from jax import lax
from jax.experimental import pallas as pl
from jax.experimental.pallas import tpu as pltpu

F32 = jnp.float32
BF16 = jnp.bfloat16

D_MODEL = 2048
DEPTH = 4

GLA_HEADS = 4
GLA_DK = D_MODEL // 2
GLA_DV = D_MODEL
GLA_HEAD_K = GLA_DK // GLA_HEADS
GLA_HEAD_V = GLA_DV // GLA_HEADS
GLA_GATE_RANK = 16
GLA_GATE_NORMALIZER = 16.0
GLA_CHUNK = 64
GLA_MAIN = 2 * GLA_DK + 2 * GLA_DV

DIL_PATTERNS = ((128, 1), (512, 4), (2048, 16))
DIL_GROUPS = len(DIL_PATTERNS)
DIL_HEADS = 8
DIL_HEAD_DIM = 128
DIL_WIDTH = DIL_HEADS * DIL_HEAD_DIM
DIL_BLOCK = 128

D_FF = 5504
CONV_WIDTH = 3

DEEPNORM_ALPHA = (2 * DEPTH) ** 0.25
LN_EPS = 1e-5
RMS_EPS = 1e-6

LANES = 128
LOG2E = 1.4426950408889634
FF_TILE = 512
D_FF_PAD = -(-D_FF // FF_TILE) * FF_TILE
VMEM_LIMIT = 56 * 1024 * 1024


def _params(*sem):
    return pltpu.CompilerParams(dimension_semantics=sem, vmem_limit_bytes=VMEM_LIMIT)


def _gla_proj_kernel(x_ref, w_ref, wg_ref, o_ref, g_ref):
    x = x_ref[...]
    o_ref[...] = jnp.dot(x, w_ref[...], preferred_element_type=F32).astype(o_ref.dtype)

    @pl.when(pl.program_id(1) == 0)
    def _():
        g_ref[...] = jnp.dot(x, wg_ref[...], preferred_element_type=F32)


def _gla_proj(x, w_main, w_gate, *, tm=1024, tn=1024):
    m, k = x.shape
    n = w_main.shape[1]
    tm = min(tm, m)
    return pl.pallas_call(
        _gla_proj_kernel,
        out_shape=(jax.ShapeDtypeStruct((m, n), BF16), jax.ShapeDtypeStruct((m, LANES), F32)),
        grid=(m // tm, n // tn),
        in_specs=[pl.BlockSpec((tm, k), lambda i, j: (i, 0)),
                  pl.BlockSpec((k, tn), lambda i, j: (0, j)),
                  pl.BlockSpec((k, LANES), lambda i, j: (0, 0))],
        out_specs=(pl.BlockSpec((tm, tn), lambda i, j: (i, j)),
                   pl.BlockSpec((tm, LANES), lambda i, j: (i, 0))),
        compiler_params=_params("parallel", "arbitrary"),
        name="gla_proj",
    )(x, w_main, w_gate)


GLA_HEADS_PER_STEP = 2
GLA_PAIR_ROWS = 8


def _gla_core_kernel(q_ref, k_ref, v_ref, r_ref, gl_ref, wgu_ref, gb_ref, ng_ref, o_ref, st_ref, *, nchunks):
    C, c = GLA_CHUNK, GLA_PAIR_ROWS
    nsub = C // c
    dk, dv = GLA_HEAD_K, GLA_HEAD_V
    heads = range(GLA_HEADS_PER_STEP)
    st_ref[...] = jnp.zeros_like(st_ref)

    rows = lax.broadcasted_iota(jnp.int32, (C, C), 0)
    cols = lax.broadcasted_iota(jnp.int32, (C, C), 1)
    tri = (rows >= cols).astype(F32)
    sub_rows = lax.broadcasted_iota(jnp.int32, (c, C), 0)
    sub_cols = lax.broadcasted_iota(jnp.int32, (c, C), 1)
    ng = ng_ref[...]
    nt = (((1,), (1,)), ((), ()))
    tn = (((0,), (0,)), ((), ()))
    hi = lax.Precision.HIGHEST

    def chunk(ci, carry):
        r0 = pl.multiple_of(ci * C, C)
        rs = pl.ds(r0, C)
        gl = gl_ref[rs, :]
        q = [q_ref[rs, h * dk:(h + 1) * dk].astype(F32) * (dk ** -0.5) for h in heads]
        k = [k_ref[rs, h * dk:(h + 1) * dk].astype(F32) for h in heads]
        v = [v_ref[rs, h * dv:(h + 1) * dv] for h in heads]
        z = [(jnp.dot(gl, wgu_ref[:, h * dk:(h + 1) * dk], preferred_element_type=F32, precision=hi)
              + gb_ref[:, h * dk:(h + 1) * dk]) * LOG2E for h in heads]
        g = [(jnp.minimum(zh, 0.0) - jnp.log2(1.0 + jnp.exp2(-jnp.abs(zh)))) * (1.0 / GLA_GATE_NORMALIZER) for zh in z]
        b = [jnp.dot(tri, gh, preferred_element_type=F32, precision=hi) for gh in g]
        b_last = [bh[C - 1:C, :] for bh in b]

        st = [st_ref[h] for h in heads]
        qd = [(qh * jnp.exp2(bh)).astype(BF16) for qh, bh in zip(q, b)]
        o = [lax.dot_general(qdh, sth.astype(BF16), nt, preferred_element_type=F32) for qdh, sth in zip(qd, st)]

        subs = [(h, s) for s in range(nsub) for h in heads]
        a_s = {}
        for h, s in subs:
            lo = s * c
            if s == 0:
                a_s[h, s] = jnp.zeros((c, C), F32)
            else:
                b_ref = b[h][lo - 1:lo, :]
                q_ref_s = (q[h][lo:lo + c, :] * jnp.exp2(b[h][lo:lo + c, :] - b_ref)).astype(BF16)
                k_ref_s = (k[h][:lo, :] * jnp.exp2(b_ref - b[h][:lo, :])).astype(BF16)
                k_ref_s = jnp.concatenate([k_ref_s, jnp.zeros((C - lo, dk), BF16)], axis=0)
                a_s[h, s] = lax.dot_general(q_ref_s, k_ref_s, nt, preferred_element_type=F32)
        for j in range(c):
            for h, s in subs:
                lo = s * c
                bs = b[h][lo:lo + c, :]
                e = jnp.exp2(bs - bs[j:j + 1, :])
                col = jnp.sum(q[h][lo:lo + c, :] * k[h][lo + j:lo + j + 1, :] * e, axis=-1, keepdims=True)
                a_s[h, s] = jnp.where((sub_cols == lo + j) & (sub_rows >= j), col, a_s[h, s])
        a = [jnp.concatenate([a_s[h, s] for s in range(nsub)], axis=0).astype(BF16) for h in heads]
        o = [oh + jnp.dot(ah, vh, preferred_element_type=F32) for oh, ah, vh in zip(o, a, v)]

        kd = [(kh * jnp.exp2(bl - bh)).astype(BF16) for kh, bl, bh in zip(k, b_last, b)]
        upd = [lax.dot_general(vh, kdh, tn, preferred_element_type=F32) for vh, kdh in zip(v, kd)]
        for h in heads:
            st_ref[h] = st[h] * jnp.exp2(b_last[h]) + upd[h]

        o = [oh * lax.rsqrt(jnp.mean(oh * oh, axis=-1, keepdims=True) + RMS_EPS) * ng for oh in o]
        r = [r_ref[rs, h * dv:(h + 1) * dv].astype(F32) for h in heads]
        for h in heads:
            o_ref[rs, h * dv:(h + 1) * dv] = (o[h] * (r[h] / (1.0 + jnp.exp(-r[h])))).astype(o_ref.dtype)
        return carry

    lax.fori_loop(0, nchunks, chunk, 0)


def _gla_core(proj, g_low, w_gate_up, gate_bias, norm_g, batch, seq):
    dk, dv, hp = GLA_HEAD_K, GLA_HEAD_V, GLA_HEADS_PER_STEP
    q_blk = lambda off: pl.BlockSpec((None, seq, hp * dk), lambda b, h: (b, 0, off // (hp * dk) + h))
    v_blk = lambda off: pl.BlockSpec((None, seq, hp * dv), lambda b, h: (b, 0, off // (hp * dv) + h))
    return pl.pallas_call(
        functools.partial(_gla_core_kernel, nchunks=seq // GLA_CHUNK),
        out_shape=jax.ShapeDtypeStruct((batch, seq, GLA_DV), BF16),
        grid=(batch, GLA_HEADS // hp),
        in_specs=[q_blk(0), q_blk(GLA_DK), v_blk(2 * GLA_DK), v_blk(2 * GLA_DK + GLA_DV),
                  pl.BlockSpec((None, seq, LANES), lambda b, h: (b, 0, 0)),
                  pl.BlockSpec((LANES, hp * dk), lambda b, h: (0, h)),
                  pl.BlockSpec((1, hp * dk), lambda b, h: (0, h)),
                  pl.BlockSpec((1, dv), lambda b, h: (0, 0))],
        out_specs=pl.BlockSpec((None, seq, hp * dv), lambda b, h: (b, 0, h)),
        scratch_shapes=[pltpu.VMEM((hp, dv, dk), F32)],
        compiler_params=_params("parallel", "arbitrary"),
        name="gla_core",
    )(proj, proj, proj, proj, g_low, w_gate_up, gate_bias, norm_g)


LN_ROWS = 64


def _mm_res_ln_kernel(a_ref, w_ref, x_ref, g_ref, b_ref, of_ref, ob_ref, acc_ref):
    if len(a_ref.shape) == 3:
        a = jnp.concatenate([a_ref[h] for h in range(a_ref.shape[0])], axis=1)
    else:
        a = a_ref[...]
    acc_ref[...] = jnp.dot(a, w_ref[...], preferred_element_type=F32)
    gamma = g_ref[...]
    beta = b_ref[...]

    def rows(i, carry):
        r0 = pl.multiple_of(i * LN_ROWS, LN_ROWS)
        y = DEEPNORM_ALPHA * x_ref[pl.ds(r0, LN_ROWS), :] + acc_ref[pl.ds(r0, LN_ROWS), :]
        mu = jnp.mean(y, axis=-1, keepdims=True)
        yc = y - mu
        var = jnp.mean(yc * yc, axis=-1, keepdims=True)
        out = yc * lax.rsqrt(var + LN_EPS) * gamma + beta
        of_ref[pl.ds(r0, LN_ROWS), :] = out
        ob_ref[pl.ds(r0, LN_ROWS), :] = out.astype(BF16)
        return carry

    lax.fori_loop(0, acc_ref.shape[0] // LN_ROWS, rows, 0)


def _mm_res_ln(a, w, x, gamma, beta, *, tm=512):
    k, n = w.shape
    m = x.shape[0]
    if a.ndim == 4:
        _, heads, seq, dh = a.shape
        tps = seq // tm
        a_spec = pl.BlockSpec((None, heads, tm, dh), lambda i: (i // tps, 0, i % tps, 0))
    else:
        a_spec = pl.BlockSpec((tm, k), lambda i: (i, 0))
    return pl.pallas_call(
        _mm_res_ln_kernel,
        out_shape=(jax.ShapeDtypeStruct((m, n), F32), jax.ShapeDtypeStruct((m, n), BF16)),
        grid=(m // tm,),
        in_specs=[a_spec,
                  pl.BlockSpec((k, n), lambda i: (0, 0)),
                  pl.BlockSpec((tm, n), lambda i: (i, 0)),
                  pl.BlockSpec((1, n), lambda i: (0, 0)),
                  pl.BlockSpec((1, n), lambda i: (0, 0))],
        out_specs=(pl.BlockSpec((tm, n), lambda i: (i, 0)),
                   pl.BlockSpec((tm, n), lambda i: (i, 0))),
        scratch_shapes=[pltpu.VMEM((tm, n), F32)],
        compiler_params=_params("parallel"),
        name="mm_res_ln",
    )(a, w, x, gamma, beta)


FFN_TM = 1024
FFN_CHUNK = 256
FFN_PIECE = 32
HALO = 8


def _ffn_kernel(x_ref, wup_ref, cw_ref, cb_ref, wd_ref, xres_ref, g_ref, b_ref, of_ref, ob_ref,
                h0_ref, h1_ref, halo_ref, act_ref, *, nj, tiles_per_seq):
    i = pl.program_id(0)
    j = pl.program_id(1)
    tm = x_ref.shape[0]
    tf = wd_ref.shape[0]
    nchunk = tm // FFN_CHUNK

    def up_chunk(c, h_ref):
        r0 = c * FFN_CHUNK
        h_ref[pl.ds(HALO + r0, FFN_CHUNK), :] = jnp.dot(x_ref[pl.ds(r0, FFN_CHUNK), :], wup_ref[...],
                                                        preferred_element_type=F32)

    def act_chunk(c, h_ref):
        cw = cw_ref[...]
        cb = cb_ref[...]
        for p in range(FFN_CHUNK // FFN_PIECE):
            r0 = c * FFN_CHUNK + p * FFN_PIECE
            win = h_ref[pl.ds(r0, FFN_PIECE + HALO), :]
            y = (cw[2:3, :] * win[HALO:, :] + cw[1:2, :] * pltpu.roll(win, 1, axis=0)[HALO:, :]
                 + cw[0:1, :] * pltpu.roll(win, 2, axis=0)[HALO:, :] + cb)
            gate = y[:, :tf]
            act_ref[pl.ds(r0, FFN_PIECE), :] = (gate / (1.0 + jnp.exp(-gate)) * y[:, tf:]).astype(BF16)
        return act_ref[pl.ds(c * FFN_CHUNK, FFN_CHUNK), :]

    def set_halo(h_ref):
        @pl.when(i % tiles_per_seq == 0)
        def _():
            h_ref[pl.ds(0, HALO), :] = jnp.zeros((HALO, 2 * tf), F32)

        @pl.when(i % tiles_per_seq != 0)
        def _():
            h_ref[pl.ds(0, HALO), :] = halo_ref[j]

    def save_halo(h_ref):
        halo_ref[j] = h_ref[pl.ds(tm, HALO), :]

    def first_step(h_ref):
        set_halo(h_ref)
        for c in range(nchunk):
            up_chunk(c, h_ref)
            of_ref[pl.ds(c * FFN_CHUNK, FFN_CHUNK), :] = jnp.zeros((FFN_CHUNK, of_ref.shape[1]), F32)
        save_halo(h_ref)

    def mid_step(h_up, h_prev):
        set_halo(h_up)
        for c in range(nchunk):
            up_chunk(c, h_up)
            a = act_chunk(c, h_prev)
            of_ref[pl.ds(c * FFN_CHUNK, FFN_CHUNK), :] += jnp.dot(a, wd_ref[...], preferred_element_type=F32)
        save_halo(h_up)

    def last_step(h_prev):
        gamma = g_ref[...]
        beta = b_ref[...]
        for c in range(nchunk):
            a = act_chunk(c, h_prev)
            of_ref[pl.ds(c * FFN_CHUNK, FFN_CHUNK), :] += jnp.dot(a, wd_ref[...], preferred_element_type=F32)
            for p in range(FFN_CHUNK // FFN_PIECE):
                rows = pl.ds(c * FFN_CHUNK + p * FFN_PIECE, FFN_PIECE)
                y = DEEPNORM_ALPHA * xres_ref[rows, :] + of_ref[rows, :]
                mu = jnp.mean(y, axis=-1, keepdims=True)
                yc = y - mu
                var = jnp.mean(yc * yc, axis=-1, keepdims=True)
                out = yc * lax.rsqrt(var + LN_EPS) * gamma + beta
                of_ref[rows, :] = out
                ob_ref[rows, :] = out.astype(BF16)

    slots = (h0_ref, h1_ref)

    @pl.when(j == 0)
    def _():
        first_step(slots[0])

    for parity in (0, 1):
        @pl.when((j > 0) & (j < nj) & (j % 2 == parity))
        def _():
            mid_step(slots[parity], slots[1 - parity])

    @pl.when(j == nj)
    def _():
        last_step(slots[(nj - 1) % 2])


def _ffn(xb, xf, w_up, conv_w, conv_b, w_down, gamma, beta, seq):
    m, d = xb.shape
    tf = FF_TILE
    nj = w_down.shape[0] // tf
    tm = min(FFN_TM, seq)
    one = pl.Buffered(1)
    return pl.pallas_call(
        functools.partial(_ffn_kernel, nj=nj, tiles_per_seq=seq // tm),
        out_shape=(jax.ShapeDtypeStruct((m, d), F32), jax.ShapeDtypeStruct((m, d), BF16)),
        grid=(m // tm, nj + 1),
        in_specs=[pl.BlockSpec((tm, d), lambda i, j: (i, 0)),
                  pl.BlockSpec((d, 2 * tf), lambda i, j: (0, jnp.minimum(j, nj - 1))),
                  pl.BlockSpec((CONV_WIDTH, 2 * tf), lambda i, j: (0, jnp.maximum(j - 1, 0))),
                  pl.BlockSpec((1, 2 * tf), lambda i, j: (0, jnp.maximum(j - 1, 0))),
                  pl.BlockSpec((tf, d), lambda i, j: (jnp.maximum(j - 1, 0), 0)),
                  pl.BlockSpec((tm, d), lambda i, j: (i, 0), pipeline_mode=one),
                  pl.BlockSpec((1, d), lambda i, j: (0, 0)),
                  pl.BlockSpec((1, d), lambda i, j: (0, 0))],
        out_specs=(pl.BlockSpec((tm, d), lambda i, j: (i, 0), pipeline_mode=one),
                   pl.BlockSpec((tm, d), lambda i, j: (i, 0), pipeline_mode=one)),
        scratch_shapes=[pltpu.VMEM((tm + HALO, 2 * tf), F32), pltpu.VMEM((tm + HALO, 2 * tf), F32),
                        pltpu.VMEM((nj, HALO, 2 * tf), F32), pltpu.VMEM((tm, tf), BF16)],
        compiler_params=_params("arbitrary", "arbitrary"),
        name="ffn",
    )(xb, w_up, conv_w, conv_b, w_down, xf, gamma, beta)


def _dil_proj_kernel(x_ref, w_ref, o_ref, *scratch, dil):
    acc = jnp.dot(x_ref[...], w_ref[...], preferred_element_type=F32)
    dh = DIL_HEAD_DIM
    if dil == 1:
        for h in range(DIL_HEADS):
            o_ref[h, 0] = acc[:, h * dh:(h + 1) * dh].astype(o_ref.dtype)
        return
    (rows_ref,) = scratch
    n = x_ref.shape[0] // dil
    for h in range(DIL_HEADS):
        rows_ref[h] = acc[:, h * dh:(h + 1) * dh]
    for h in range(DIL_HEADS):
        for r in range(dil):
            o_ref[h, r] = rows_ref[h, pl.ds(r, n, stride=dil), :].astype(o_ref.dtype)


def _dil_proj(xb, w, dil, batch, seq, *, tm=1024):
    m, k = xb.shape
    H, dh = DIL_HEADS, DIL_HEAD_DIM
    tm = min(tm, seq)
    tps = seq // tm
    scratch = [] if dil == 1 else [pltpu.VMEM((H, tm, dh), F32)]
    return pl.pallas_call(
        functools.partial(_dil_proj_kernel, dil=dil),
        out_shape=jax.ShapeDtypeStruct((batch, 3, H, dil, seq // dil, dh), BF16),
        grid=(m // tm, 3),
        in_specs=[pl.BlockSpec((tm, k), lambda i, c: (i, 0)),
                  pl.BlockSpec((k, H * dh), lambda i, c: (0, c))],
        out_specs=pl.BlockSpec((None, None, H, dil, tm // dil, dh), lambda i, c: (i // tps, c, 0, 0, i % tps, 0)),
        scratch_shapes=scratch,
        compiler_params=_params("parallel", "arbitrary"),
        name=f"dil_proj_d{dil}",
    )(xb, w)


NEG = -1e30
EXP2_SCALE = (DIL_HEAD_DIM ** -0.5) * LOG2E
ATTN_ROWS = 256


def _dil_attn_kernel(*refs, seq):
    qkv = refs[:3 * DIL_GROUPS]
    o_ref = refs[3 * DIL_GROUPS]
    m_ref, l_ref, acc_ref = refs[3 * DIL_GROUPS + 1:]
    P = DIL_BLOCK
    dh = DIL_HEAD_DIM
    nt = (((1,), (1,)), ((), ()))
    qi = lax.broadcasted_iota(jnp.int32, (P, P), 0)
    kj = lax.broadcasted_iota(jnp.int32, (P, P), 1)
    ones = jnp.ones((P, LANES), BF16)
    for gi, (window, dil) in enumerate(DIL_PATTERNS):
        q_ref, k_ref, v_ref = qkv[3 * gi:3 * gi + 3]
        steps = window // dil
        bias_cur = jnp.where((qi >= kj) & (qi - kj <= steps), 0.0, NEG)
        bias_prev = jnp.where(qi + P - kj <= steps, 0.0, NEG)
        blocks = [(r, lb * P) for r in range(dil) for lb in range(seq // dil // P)]
        s_cur = [lax.dot_general(q_ref[r, lo:lo + P, :], k_ref[r, lo:lo + P, :], nt,
                                 preferred_element_type=F32) + bias_cur for r, lo in blocks]
        s_prev = [lax.dot_general(q_ref[r, lo:lo + P, :], k_ref[r, lo - P:lo, :], nt,
                                  preferred_element_type=F32) + bias_prev if lo > 0 else None for r, lo in blocks]
        m = [jnp.max(sc if sp is None else jnp.maximum(sc, sp), axis=-1, keepdims=True)
             for sc, sp in zip(s_cur, s_prev)]
        p_cur = [jnp.exp2((sc - mb) * EXP2_SCALE).astype(BF16) for sc, mb in zip(s_cur, m)]
        p_prev = [None if sp is None else jnp.exp2((sp - mb) * EXP2_SCALE).astype(BF16) for sp, mb in zip(s_prev, m)]
        acc = [jnp.dot(pc, jnp.concatenate([v_ref[r, lo:lo + P, :], ones], axis=1), preferred_element_type=F32)
               for pc, (r, lo) in zip(p_cur, blocks)]
        acc = [ac if pp is None else
               ac + jnp.dot(pp, jnp.concatenate([v_ref[r, lo - P:lo, :], ones], axis=1), preferred_element_type=F32)
               for ac, pp, (r, lo) in zip(acc, p_prev, blocks)]
        for (r, lo), mb, ab in zip(blocks, m, acc):
            tok = pl.ds(lo * dil + r, P, stride=dil) if dil > 1 else pl.ds(lo, P)
            m_ref[gi, tok, :] = jnp.broadcast_to(mb, (P, LANES))
            l_ref[gi, tok, :] = ab[:, dh:]
            acc_ref[gi, tok, :] = ab[:, :dh]
    for c in range(seq // ATTN_ROWS):
        rows = pl.ds(c * ATTN_ROWS, ATTN_ROWS)
        ms = [m_ref[gi, rows, :] for gi in range(DIL_GROUPS)]
        m_all = functools.reduce(jnp.maximum, ms)
        ws = [jnp.exp2((mg - m_all) * EXP2_SCALE) for mg in ms]
        l_all = sum(w * l_ref[gi, rows, :] for gi, w in enumerate(ws))
        acc_all = sum(w * acc_ref[gi, rows, :] for gi, w in enumerate(ws))
        o_ref[rows, :] = (acc_all / l_all).astype(o_ref.dtype)


def _dil_attn(groups, batch, seq):
    dh = DIL_HEAD_DIM
    in_specs, args = [], []
    for (window, dil), t in zip(DIL_PATTERNS, groups):
        for c in range(3):
            in_specs.append(pl.BlockSpec((None, None, None, dil, seq // dil, dh),
                                         lambda b, h, c=c: (b, c, h, 0, 0, 0)))
            args.append(t)
    return pl.pallas_call(
        functools.partial(_dil_attn_kernel, seq=seq),
        out_shape=jax.ShapeDtypeStruct((batch, DIL_HEADS, seq, dh), BF16),
        grid=(batch, DIL_HEADS),
        in_specs=in_specs,
        out_specs=pl.BlockSpec((None, None, seq, dh), lambda b, h: (b, h, 0, 0)),
        scratch_shapes=[pltpu.VMEM((DIL_GROUPS, seq, LANES), F32), pltpu.VMEM((DIL_GROUPS, seq, LANES), F32),
                        pltpu.VMEM((DIL_GROUPS, seq, dh), F32)],
        compiler_params=_params("parallel", "parallel"),
        name="dil_attn",
    )(*args)


def _pad_cols(w, n):
    return jnp.pad(w, ((0, 0), (0, n - w.shape[1])))


def _gla_layer(xf, xb, w_in, w_gate_up, gate_bias, norm_g, w_out, gamma, beta, batch, seq):
    w_main = w_in[:, :GLA_MAIN].astype(BF16)
    w_gate = _pad_cols(w_in[:, GLA_MAIN:], LANES).astype(BF16)
    proj, g_low = _gla_proj(xb, w_main, w_gate)
    wgu = jnp.pad(w_gate_up, ((0, LANES - GLA_GATE_RANK), (0, 0)))
    o = _gla_core(proj.reshape(batch, seq, GLA_MAIN), g_low.reshape(batch, seq, LANES),
                  wgu, gate_bias[None, :], norm_g[None, :], batch, seq)
    return _mm_res_ln(o.reshape(batch * seq, GLA_DV), w_out.astype(BF16), xf, gamma[None, :], beta[None, :])


def _dil_layer(xf, xb, w_in, w_out, gamma, beta, batch, seq):
    gw = 3 * DIL_WIDTH
    w_in = w_in.astype(BF16)
    groups = [_dil_proj(xb, w_in[:, gi * gw:(gi + 1) * gw], dil, batch, seq)
              for gi, (window, dil) in enumerate(DIL_PATTERNS)]
    o = _dil_attn(groups, batch, seq)
    return _mm_res_ln(o, w_out.astype(BF16), xf, gamma[None, :], beta[None, :])


def _ffn_layer(xf, xb, w_up, conv_w, conv_b, w_down, gamma, beta, batch, seq):
    def tiled(t):
        r = t.shape[0]
        halves = [_pad_cols(h, D_FF_PAD).reshape(r, D_FF_PAD // FF_TILE, FF_TILE) for h in (t[:, :D_FF], t[:, D_FF:])]
        return jnp.concatenate(halves, axis=2).reshape(r, 2 * D_FF_PAD)

    wd = jnp.pad(w_down, ((0, D_FF_PAD - D_FF), (0, 0))).astype(BF16)
    return _ffn(xb, xf, tiled(w_up.astype(BF16)), tiled(conv_w), tiled(conv_b[None, :]), wd,
                gamma[None, :], beta[None, :], seq)


def kernel(x, gla_w_in, gla_w_gate_up, gla_gate_bias, gla_norm_g, gla_w_out, dil_w_in, dil_w_out, ffn_w_up,
           ffn_conv_w, ffn_conv_b, ffn_w_down, ln_g, ln_b):
    batch, seq, d = x.shape
    xf = x.reshape(batch * seq, d)
    xb = xf.astype(BF16)
    for i in range(DEPTH):
        j = i // 2
        if i % 2 == 0:
            xf, xb = _gla_layer(xf, xb, gla_w_in[j], gla_w_gate_up[j], gla_gate_bias[j], gla_norm_g[j],
                                gla_w_out[j], ln_g[i, 0], ln_b[i, 0], batch, seq)
        else:
            xf, xb = _dil_layer(xf, xb, dil_w_in[j], dil_w_out[j], ln_g[i, 0], ln_b[i, 0], batch, seq)
        xf, xb = _ffn_layer(xf, xb, ffn_w_up[i], ffn_conv_w[i], ffn_conv_b[i], ffn_w_down[i],
                            ln_g[i, 1], ln_b[i, 1], batch, seq)
    return xf.reshape(batch, seq, d)
```

```python
import functools

import jax
import jax.numpy as jnp
from jax import lax
from jax.experimental import pallas as pl
from jax.experimental.pallas import tpu as pltpu

F32 = jnp.float32
BF16 = jnp.bfloat16

D_MODEL = 2048
DEPTH = 4

GLA_HEADS = 4
GLA_DK = D_MODEL // 2
GLA_DV = D_MODEL
GLA_HEAD_K = GLA_DK // GLA_HEADS
GLA_HEAD_V = GLA_DV // GLA_HEADS
GLA_GATE_RANK = 16
GLA_GATE_NORMALIZER = 16.0
GLA_CHUNK = 64
GLA_MAIN = 2 * GLA_DK + 2 * GLA_DV

DIL_PATTERNS = ((128, 1), (512, 4), (2048, 16))
DIL_GROUPS = len(DIL_PATTERNS)
DIL_HEADS = 8
DIL_HEAD_DIM = 128
DIL_WIDTH = DIL_HEADS * DIL_HEAD_DIM
DIL_BLOCK = 128

D_FF = 5504
CONV_WIDTH = 3

DEEPNORM_ALPHA = (2 * DEPTH) ** 0.25
LN_EPS = 1e-5
RMS_EPS = 1e-6

LANES = 128
LOG2E = 1.4426950408889634
FF_TILE = 512
D_FF_PAD = -(-D_FF // FF_TILE) * FF_TILE
VMEM_LIMIT = 56 * 1024 * 1024


def _params(*sem):
    return pltpu.CompilerParams(dimension_semantics=sem, vmem_limit_bytes=VMEM_LIMIT)


def _gla_proj_kernel(x_ref, w_ref, wg_ref, o_ref, g_ref):
    x = x_ref[...]
    o_ref[...] = jnp.dot(x, w_ref[...], preferred_element_type=F32).astype(o_ref.dtype)

    @pl.when(pl.program_id(1) == 0)
    def _():
        g_ref[...] = jnp.dot(x, wg_ref[...], preferred_element_type=F32)


def _gla_proj(x, w_main, w_gate, *, tm=1024, tn=1024):
    m, k = x.shape
    n = w_main.shape[1]
    tm = min(tm, m)
    return pl.pallas_call(
        _gla_proj_kernel,
        out_shape=(jax.ShapeDtypeStruct((m, n), BF16), jax.ShapeDtypeStruct((m, LANES), F32)),
        grid=(m // tm, n // tn),
        in_specs=[pl.BlockSpec((tm, k), lambda i, j: (i, 0)),
                  pl.BlockSpec((k, tn), lambda i, j: (0, j)),
                  pl.BlockSpec((k, LANES), lambda i, j: (0, 0))],
        out_specs=(pl.BlockSpec((tm, tn), lambda i, j: (i, j)),
                   pl.BlockSpec((tm, LANES), lambda i, j: (i, 0))),
        compiler_params=_params("parallel", "arbitrary"),
        name="gla_proj",
    )(x, w_main, w_gate)


GLA_HEADS_PER_STEP = 2
GLA_PAIR_ROWS = 8


def _gla_core_kernel(q_ref, k_ref, v_ref, r_ref, gl_ref, wgu_ref, gb_ref, ng_ref, o_ref, st_ref, *, nchunks):
    C, c = GLA_CHUNK, GLA_PAIR_ROWS
    nsub = C // c
    dk, dv = GLA_HEAD_K, GLA_HEAD_V
    heads = range(GLA_HEADS_PER_STEP)
    st_ref[...] = jnp.zeros_like(st_ref)

    rows = lax.broadcasted_iota(jnp.int32, (C, C), 0)
    cols = lax.broadcasted_iota(jnp.int32, (C, C), 1)
    tri = (rows >= cols).astype(F32)
    sub_rows = lax.broadcasted_iota(jnp.int32, (c, C), 0)
    sub_cols = lax.broadcasted_iota(jnp.int32, (c, C), 1)
    ng = ng_ref[...]
    nt = (((1,), (1,)), ((), ()))
    tn = (((0,), (0,)), ((), ()))
    hi = lax.Precision.HIGHEST

    def chunk(ci, carry):
        r0 = pl.multiple_of(ci * C, C)
        rs = pl.ds(r0, C)
        gl = gl_ref[rs, :]
        q = [q_ref[rs, h * dk:(h + 1) * dk].astype(F32) * (dk ** -0.5) for h in heads]
        k = [k_ref[rs, h * dk:(h + 1) * dk].astype(F32) for h in heads]
        v = [v_ref[rs, h * dv:(h + 1) * dv] for h in heads]
        z = [(jnp.dot(gl, wgu_ref[:, h * dk:(h + 1) * dk], preferred_element_type=F32, precision=hi)
              + gb_ref[:, h * dk:(h + 1) * dk]) * LOG2E for h in heads]
        g = [(jnp.minimum(zh, 0.0) - jnp.log2(1.0 + jnp.exp2(-jnp.abs(zh)))) * (1.0 / GLA_GATE_NORMALIZER) for zh in z]
        b = [jnp.dot(tri, gh, preferred_element_type=F32, precision=hi) for gh in g]
        b_last = [bh[C - 1:C, :] for bh in b]

        st = [st_ref[h] for h in heads]
        qd = [(qh * jnp.exp2(bh)).astype(BF16) for qh, bh in zip(q, b)]
        o = [lax.dot_general(qdh, sth.astype(BF16), nt, preferred_element_type=F32) for qdh, sth in zip(qd, st)]

        subs = [(h, s) for s in range(nsub) for h in heads]
        a_s = {}
        for h, s in subs:
            lo = s * c
            if s == 0:
                a_s[h, s] = jnp.zeros((c, C), F32)
            else:
                b_ref = b[h][lo - 1:lo, :]
                q_ref_s = (q[h][lo:lo + c, :] * jnp.exp2(b[h][lo:lo + c, :] - b_ref)).astype(BF16)
                k_ref_s = (k[h][:lo, :] * jnp.exp2(b_ref - b[h][:lo, :])).astype(BF16)
                k_ref_s = jnp.concatenate([k_ref_s, jnp.zeros((C - lo, dk), BF16)], axis=0)
                a_s[h, s] = lax.dot_general(q_ref_s, k_ref_s, nt, preferred_element_type=F32)
        for j in range(c):
            for h, s in subs:
                lo = s * c
                bs = b[h][lo:lo + c, :]
                e = jnp.exp2(bs - bs[j:j + 1, :])
                col = jnp.sum(q[h][lo:lo + c, :] * k[h][lo + j:lo + j + 1, :] * e, axis=-1, keepdims=True)
                a_s[h, s] = jnp.where((sub_cols == lo + j) & (sub_rows >= j), col, a_s[h, s])
        a = [jnp.concatenate([a_s[h, s] for s in range(nsub)], axis=0).astype(BF16) for h in heads]
        o = [oh + jnp.dot(ah, vh, preferred_element_type=F32) for oh, ah, vh in zip(o, a, v)]

        kd = [(kh * jnp.exp2(bl - bh)).astype(BF16) for kh, bl, bh in zip(k, b_last, b)]
        upd = [lax.dot_general(vh, kdh, tn, preferred_element_type=F32) for vh, kdh in zip(v, kd)]
        for h in heads:
            st_ref[h] = st[h] * jnp.exp2(b_last[h]) + upd[h]

        o = [oh * lax.rsqrt(jnp.mean(oh * oh, axis=-1, keepdims=True) + RMS_EPS) * ng for oh in o]
        r = [r_ref[rs, h * dv:(h + 1) * dv].astype(F32) for h in heads]
        for h in heads:
            o_ref[rs, h * dv:(h + 1) * dv] = (o[h] * (r[h] / (1.0 + jnp.exp(-r[h])))).astype(o_ref.dtype)
        return carry

    lax.fori_loop(0, nchunks, chunk, 0)


def _gla_core(proj, g_low, w_gate_up, gate_bias, norm_g, batch, seq):
    dk, dv, hp = GLA_HEAD_K, GLA_HEAD_V, GLA_HEADS_PER_STEP
    q_blk = lambda off: pl.BlockSpec((None, seq, hp * dk), lambda b, h: (b, 0, off // (hp * dk) + h))
    v_blk = lambda off: pl.BlockSpec((None, seq, hp * dv), lambda b, h: (b, 0, off // (hp * dv) + h))
    return pl.pallas_call(
        functools.partial(_gla_core_kernel, nchunks=seq // GLA_CHUNK),
        out_shape=jax.ShapeDtypeStruct((batch, seq, GLA_DV), BF16),
        grid=(batch, GLA_HEADS // hp),
        in_specs=[q_blk(0), q_blk(GLA_DK), v_blk(2 * GLA_DK), v_blk(2 * GLA_DK + GLA_DV),
                  pl.BlockSpec((None, seq, LANES), lambda b, h: (b, 0, 0)),
                  pl.BlockSpec((LANES, hp * dk), lambda b, h: (0, h)),
                  pl.BlockSpec((1, hp * dk), lambda b, h: (0, h)),
                  pl.BlockSpec((1, dv), lambda b, h: (0, 0))],
        out_specs=pl.BlockSpec((None, seq, hp * dv), lambda b, h: (b, 0, h)),
        scratch_shapes=[pltpu.VMEM((hp, dv, dk), F32)],
        compiler_params=_params("parallel", "arbitrary"),
        name="gla_core",
    )(proj, proj, proj, proj, g_low, w_gate_up, gate_bias, norm_g)


LN_ROWS = 64


def _mm_res_ln_kernel(a_ref, w_ref, x_ref, g_ref, b_ref, of_ref, ob_ref, acc_ref):
    if len(a_ref.shape) == 3:
        a = jnp.concatenate([a_ref[h] for h in range(a_ref.shape[0])], axis=1)
    else:
        a = a_ref[...]
    acc_ref[...] = jnp.dot(a, w_ref[...], preferred_element_type=F32)
    gamma = g_ref[...]
    beta = b_ref[...]

    def rows(i, carry):
        r0 = pl.multiple_of(i * LN_ROWS, LN_ROWS)
        y = DEEPNORM_ALPHA * x_ref[pl.ds(r0, LN_ROWS), :] + acc_ref[pl.ds(r0, LN_ROWS), :]
        mu = jnp.mean(y, axis=-1, keepdims=True)
        yc = y - mu
        var = jnp.mean(yc * yc, axis=-1, keepdims=True)
        out = yc * lax.rsqrt(var + LN_EPS) * gamma + beta
        of_ref[pl.ds(r0, LN_ROWS), :] = out
        ob_ref[pl.ds(r0, LN_ROWS), :] = out.astype(BF16)
        return carry

    lax.fori_loop(0, acc_ref.shape[0] // LN_ROWS, rows, 0)


def _mm_res_ln(a, w, x, gamma, beta, *, tm=512):
    k, n = w.shape
    m = x.shape[0]
    if a.ndim == 4:
        _, heads, seq, dh = a.shape
        tps = seq // tm
        a_spec = pl.BlockSpec((None, heads, tm, dh), lambda i: (i // tps, 0, i % tps, 0))
    else:
        a_spec = pl.BlockSpec((tm, k), lambda i: (i, 0))
    return pl.pallas_call(
        _mm_res_ln_kernel,
        out_shape=(jax.ShapeDtypeStruct((m, n), F32), jax.ShapeDtypeStruct((m, n), BF16)),
        grid=(m // tm,),
        in_specs=[a_spec,
                  pl.BlockSpec((k, n), lambda i: (0, 0)),
                  pl.BlockSpec((tm, n), lambda i: (i, 0)),
                  pl.BlockSpec((1, n), lambda i: (0, 0)),
                  pl.BlockSpec((1, n), lambda i: (0, 0))],
        out_specs=(pl.BlockSpec((tm, n), lambda i: (i, 0)),
                   pl.BlockSpec((tm, n), lambda i: (i, 0))),
        scratch_shapes=[pltpu.VMEM((tm, n), F32)],
        compiler_params=_params("parallel"),
        name="mm_res_ln",
    )(a, w, x, gamma, beta)


FFN_TM = 1024
FFN_CHUNK = 256
FFN_PIECE = 32
HALO = 8


def _ffn_kernel(x_ref, wg_ref, wu_ref, cwg_ref, cwu_ref, cbg_ref, cbu_ref, wd_ref, xres_ref, g_ref, b_ref,
                of_ref, ob_ref, y0_ref, y1_ref, halo_ref, act_ref, *, nj, tiles_per_seq):
    i = pl.program_id(0)
    j = pl.program_id(1)
    tm = x_ref.shape[0]
    tf = wd_ref.shape[0]
    nchunk = tm // FFN_CHUNK

    def conv(h, prev, cw_ref, cb_ref):
        cw = cw_ref[...]
        win = jnp.concatenate([prev, h], axis=0)
        return (cw[2:3, :] * h + cw[1:2, :] * pltpu.roll(win, 1, axis=0)[HALO:, :]
                + cw[0:1, :] * pltpu.roll(win, 2, axis=0)[HALO:, :] + cb_ref[...])

    def up_steps(y_ref):
        @pl.when(i % tiles_per_seq == 0)
        def _():
            halo_ref[j] = jnp.zeros(halo_ref.shape[1:], F32)

        prev = [halo_ref[j, :, half * tf:(half + 1) * tf] for half in range(2)]
        for c in range(nchunk):
            rows = pl.ds(c * FFN_CHUNK, FFN_CHUNK)
            xs = x_ref[rows, :]
            for half, (w_ref, cw_ref, cb_ref) in enumerate(((wg_ref, cwg_ref, cbg_ref), (wu_ref, cwu_ref, cbu_ref))):
                h = jnp.dot(xs, w_ref[...], preferred_element_type=F32)
                y_ref[rows, half * tf:(half + 1) * tf] = conv(h, prev[half], cw_ref, cb_ref)
                prev[half] = h[FFN_CHUNK - HALO:, :]
            yield c
        for half in range(2):
            halo_ref[j, :, half * tf:(half + 1) * tf] = prev[half]

    def down_chunk(c, y_ref):
        for p in range(FFN_CHUNK // FFN_PIECE):
            rows = pl.ds(c * FFN_CHUNK + p * FFN_PIECE, FFN_PIECE)
            gate = y_ref[rows, :tf]
            act_ref[rows, :] = (gate / (1.0 + jnp.exp2(gate * (-LOG2E))) * y_ref[rows, tf:]).astype(BF16)
        rows = pl.ds(c * FFN_CHUNK, FFN_CHUNK)
        of_ref[rows, :] += jnp.dot(act_ref[rows, :], wd_ref[...], preferred_element_type=F32)

    def first_step(y_ref):
        for c in up_steps(y_ref):
            rows = pl.ds(c * FFN_CHUNK, FFN_CHUNK)
            of_ref[rows, :] = DEEPNORM_ALPHA * xres_ref[...] if c == 0 else jnp.zeros((FFN_CHUNK, of_ref.shape[1]), F32)

    def mid_step(y_up, y_prev):
        @pl.when(j < nchunk)
        def _():
            rows = pl.ds(pl.multiple_of(j * FFN_CHUNK, FFN_CHUNK), FFN_CHUNK)
            of_ref[rows, :] += DEEPNORM_ALPHA * xres_ref[...]

        for c in up_steps(y_up):
            down_chunk(c, y_prev)

    def last_step(y_prev):
        gamma = g_ref[...]
        beta = b_ref[...]
        for c in range(nchunk):
            down_chunk(c, y_prev)
            for p in range(FFN_CHUNK // FFN_PIECE):
                rows = pl.ds(c * FFN_CHUNK + p * FFN_PIECE, FFN_PIECE)
                y = of_ref[rows, :]
                mu = jnp.mean(y, axis=-1, keepdims=True)
                yc = y - mu
                var = jnp.mean(yc * yc, axis=-1, keepdims=True)
                out = yc * lax.rsqrt(var + LN_EPS) * gamma + beta
                of_ref[rows, :] = out
                ob_ref[rows, :] = out.astype(BF16)

    slots = (y0_ref, y1_ref)

    @pl.when(j == 0)
    def _():
        first_step(slots[0])

    for parity in (0, 1):
        @pl.when((j > 0) & (j < nj) & (j % 2 == parity))
        def _():
            mid_step(slots[parity], slots[1 - parity])

    @pl.when(j == nj)
    def _():
        last_step(slots[(nj - 1) % 2])


def _ffn(xb, xf, layer, wg, wu, cwg, cwu, cbg, cbu, wd, gamma, beta, seq):
    m, d = xb.shape
    tf = FF_TILE
    nj = wd.shape[1] // tf
    tm = min(FFN_TM, seq)
    nchunk = tm // FFN_CHUNK
    assert nj > nchunk
    one = pl.Buffered(1)
    up_j = lambda j: jnp.minimum(j, nj - 1)
    down_j = lambda j: jnp.maximum(j - 1, 0)
    up_blk = lambda rows: pl.BlockSpec((None, rows, tf), lambda i, j: (layer, 0, up_j(j)))
    return pl.pallas_call(
        functools.partial(_ffn_kernel, nj=nj, tiles_per_seq=seq // tm),
        out_shape=(jax.ShapeDtypeStruct((m, d), F32), jax.ShapeDtypeStruct((m, d), BF16)),
        grid=(m // tm, nj + 1),
        in_specs=[pl.BlockSpec((tm, d), lambda i, j: (i, 0)),
                  up_blk(d), up_blk(d), up_blk(CONV_WIDTH), up_blk(CONV_WIDTH), up_blk(1), up_blk(1),
                  pl.BlockSpec((None, tf, d), lambda i, j: (layer, down_j(j), 0)),
                  pl.BlockSpec((FFN_CHUNK, d), lambda i, j: (i * nchunk + jnp.minimum(j, nchunk - 1), 0)),
                  pl.BlockSpec((1, d), lambda i, j: (0, 0)),
                  pl.BlockSpec((1, d), lambda i, j: (0, 0))],
        out_specs=(pl.BlockSpec((tm, d), lambda i, j: (i, 0), pipeline_mode=one),
                   pl.BlockSpec((tm, d), lambda i, j: (i, 0), pipeline_mode=one)),
        scratch_shapes=[pltpu.VMEM((tm, 2 * tf), F32), pltpu.VMEM((tm, 2 * tf), F32),
                        pltpu.VMEM((nj, HALO, 2 * tf), F32), pltpu.VMEM((tm, tf), BF16)],
        compiler_params=_params("arbitrary", "arbitrary"),
        name="ffn",
    )(xb, wg, wu, cwg, cwu, cbg, cbu, wd, xf, gamma, beta)


def _dil_proj_kernel(x_ref, w_ref, o_ref, *scratch, dil):
    acc = jnp.dot(x_ref[...], w_ref[...], preferred_element_type=F32)
    dh = DIL_HEAD_DIM
    if dil == 1:
        for h in range(DIL_HEADS):
            o_ref[h, 0] = acc[:, h * dh:(h + 1) * dh].astype(o_ref.dtype)
        return
    (rows_ref,) = scratch
    n = x_ref.shape[0] // dil
    for h in range(DIL_HEADS):
        rows_ref[h] = acc[:, h * dh:(h + 1) * dh]
    for h in range(DIL_HEADS):
        for r in range(dil):
            o_ref[h, r] = rows_ref[h, pl.ds(r, n, stride=dil), :].astype(o_ref.dtype)


def _dil_proj(xb, w, dil, batch, seq, *, tm=1024):
    m, k = xb.shape
    H, dh = DIL_HEADS, DIL_HEAD_DIM
    tm = min(tm, seq)
    tps = seq // tm
    scratch = [] if dil == 1 else [pltpu.VMEM((H, tm, dh), F32)]
    return pl.pallas_call(
        functools.partial(_dil_proj_kernel, dil=dil),
        out_shape=jax.ShapeDtypeStruct((batch, 3, H, dil, seq // dil, dh), BF16),
        grid=(m // tm, 3),
        in_specs=[pl.BlockSpec((tm, k), lambda i, c: (i, 0)),
                  pl.BlockSpec((k, H * dh), lambda i, c: (0, c))],
        out_specs=pl.BlockSpec((None, None, H, dil, tm // dil, dh), lambda i, c: (i // tps, c, 0, 0, i % tps, 0)),
        scratch_shapes=scratch,
        compiler_params=_params("parallel", "arbitrary"),
        name=f"dil_proj_d{dil}",
    )(xb, w)


NEG = -1e30
EXP2_SCALE = (DIL_HEAD_DIM ** -0.5) * LOG2E
ATTN_ROWS = 256


def _dil_attn_kernel(*refs, seq):
    qkv = refs[:3 * DIL_GROUPS]
    o_ref = refs[3 * DIL_GROUPS]
    m_ref, l_ref, acc_ref = refs[3 * DIL_GROUPS + 1:]
    P = DIL_BLOCK
    dh = DIL_HEAD_DIM
    nt = (((1,), (1,)), ((), ()))
    qi = lax.broadcasted_iota(jnp.int32, (P, P), 0)
    kj = lax.broadcasted_iota(jnp.int32, (P, P), 1)
    ones = jnp.ones((P, LANES), BF16)
    for gi, (window, dil) in enumerate(DIL_PATTERNS):
        q_ref, k_ref, v_ref = qkv[3 * gi:3 * gi + 3]
        steps = window // dil
        bias_cur = jnp.where((qi >= kj) & (qi - kj <= steps), 0.0, NEG)
        bias_prev = jnp.where(qi + P - kj <= steps, 0.0, NEG)
        blocks = [(r, lb * P) for r in range(dil) for lb in range(seq // dil // P)]
        s_cur = [lax.dot_general(q_ref[r, lo:lo + P, :], k_ref[r, lo:lo + P, :], nt,
                                 preferred_element_type=F32) + bias_cur for r, lo in blocks]
        s_prev = [lax.dot_general(q_ref[r, lo:lo + P, :], k_ref[r, lo - P:lo, :], nt,
                                  preferred_element_type=F32) + bias_prev if lo > 0 else None for r, lo in blocks]
        m = [jnp.max(sc if sp is None else jnp.maximum(sc, sp), axis=-1, keepdims=True)
             for sc, sp in zip(s_cur, s_prev)]
        p_cur = [jnp.exp2((sc - mb) * EXP2_SCALE).astype(BF16) for sc, mb in zip(s_cur, m)]
        p_prev = [None if sp is None else jnp.exp2((sp - mb) * EXP2_SCALE).astype(BF16) for sp, mb in zip(s_prev, m)]
        acc = [jnp.dot(pc, jnp.concatenate([v_ref[r, lo:lo + P, :], ones], axis=1), preferred_element_type=F32)
               for pc, (r, lo) in zip(p_cur, blocks)]
        acc = [ac if pp is None else
               ac + jnp.dot(pp, jnp.concatenate([v_ref[r, lo - P:lo, :], ones], axis=1), preferred_element_type=F32)
               for ac, pp, (r, lo) in zip(acc, p_prev, blocks)]
        for (r, lo), mb, ab in zip(blocks, m, acc):
            tok = pl.ds(lo * dil + r, P, stride=dil) if dil > 1 else pl.ds(lo, P)
            m_ref[gi, tok, :] = jnp.broadcast_to(mb, (P, LANES))
            l_ref[gi, tok, :] = ab[:, dh:]
            acc_ref[gi, tok, :] = ab[:, :dh]
    for c in range(seq // ATTN_ROWS):
        rows = pl.ds(c * ATTN_ROWS, ATTN_ROWS)
        ms = [m_ref[gi, rows, :] for gi in range(DIL_GROUPS)]
        m_all = functools.reduce(jnp.maximum, ms)
        ws = [jnp.exp2((mg - m_all) * EXP2_SCALE) for mg in ms]
        l_all = sum(w * l_ref[gi, rows, :] for gi, w in enumerate(ws))
        acc_all = sum(w * acc_ref[gi, rows, :] for gi, w in enumerate(ws))
        o_ref[rows, :] = (acc_all / l_all).astype(o_ref.dtype)


def _dil_attn(groups, batch, seq):
    dh = DIL_HEAD_DIM
    in_specs, args = [], []
    for (window, dil), t in zip(DIL_PATTERNS, groups):
        for c in range(3):
            in_specs.append(pl.BlockSpec((None, None, None, dil, seq // dil, dh),
                                         lambda b, h, c=c: (b, c, h, 0, 0, 0)))
            args.append(t)
    return pl.pallas_call(
        functools.partial(_dil_attn_kernel, seq=seq),
        out_shape=jax.ShapeDtypeStruct((batch, DIL_HEADS, seq, dh), BF16),
        grid=(batch, DIL_HEADS),
        in_specs=in_specs,
        out_specs=pl.BlockSpec((None, None, seq, dh), lambda b, h: (b, h, 0, 0)),
        scratch_shapes=[pltpu.VMEM((DIL_GROUPS, seq, LANES), F32), pltpu.VMEM((DIL_GROUPS, seq, LANES), F32),
                        pltpu.VMEM((DIL_GROUPS, seq, dh), F32)],
        compiler_params=_params("parallel", "parallel"),
        name="dil_attn",
    )(*args)


def _pad_cols(w, n):
    return jnp.pad(w, ((0, 0), (0, n - w.shape[1])))


def _gla_layer(xf, xb, w_in, w_gate_up, gate_bias, norm_g, w_out, gamma, beta, batch, seq):
    w_main = w_in[:, :GLA_MAIN].astype(BF16)
    w_gate = _pad_cols(w_in[:, GLA_MAIN:], LANES).astype(BF16)
    proj, g_low = _gla_proj(xb, w_main, w_gate)
    wgu = jnp.pad(w_gate_up, ((0, LANES - GLA_GATE_RANK), (0, 0)))
    o = _gla_core(proj.reshape(batch, seq, GLA_MAIN), g_low.reshape(batch, seq, LANES),
                  wgu, gate_bias[None, :], norm_g[None, :], batch, seq)
    return _mm_res_ln(o.reshape(batch * seq, GLA_DV), w_out.astype(BF16), xf, gamma[None, :], beta[None, :])


def _dil_layer(xf, xb, w_in, w_out, gamma, beta, batch, seq):
    gw = 3 * DIL_WIDTH
    w_in = w_in.astype(BF16)
    groups = [_dil_proj(xb, w_in[:, gi * gw:(gi + 1) * gw], dil, batch, seq)
              for gi, (window, dil) in enumerate(DIL_PATTERNS)]
    o = _dil_attn(groups, batch, seq)
    return _mm_res_ln(o, w_out.astype(BF16), xf, gamma[None, :], beta[None, :])


def _ffn_weights(w_up, conv_w, conv_b, w_down):
    pad = D_FF_PAD - D_FF
    halves = lambda t: [jnp.pad(h, ((0, 0), (0, 0), (0, pad))) for h in (t[..., :D_FF], t[..., D_FF:])]
    wg, wu = (h.astype(BF16) for h in halves(w_up))
    cwg, cwu = halves(conv_w)
    cbg, cbu = halves(conv_b[:, None, :])
    wd = jnp.pad(w_down, ((0, 0), (0, pad), (0, 0))).astype(BF16)
    return wg, wu, cwg, cwu, cbg, cbu, wd


def kernel(x, gla_w_in, gla_w_gate_up, gla_gate_bias, gla_norm_g, gla_w_out, dil_w_in, dil_w_out, ffn_w_up,
           ffn_conv_w, ffn_conv_b, ffn_w_down, ln_g, ln_b):
    batch, seq, d = x.shape
    xf = x.reshape(batch * seq, d)
    xb = xf.astype(BF16)
    ffn_weights = _ffn_weights(ffn_w_up, ffn_conv_w, ffn_conv_b, ffn_w_down)
    for i in range(DEPTH):
        j = i // 2
        if i % 2 == 0:
            xf, xb = _gla_layer(xf, xb, gla_w_in[j], gla_w_gate_up[j], gla_gate_bias[j], gla_norm_g[j],
                                gla_w_out[j], ln_g[i, 0], ln_b[i, 0], batch, seq)
        else:
            xf, xb = _dil_layer(xf, xb, dil_w_in[j], dil_w_out[j], ln_g[i, 0], ln_b[i, 0], batch, seq)
        xf, xb = _ffn(xb, xf, i, *ffn_weights, ln_g[i, 1][None, :], ln_b[i, 1][None, :], seq)
    return xf.reshape(batch, seq, d)
```

```python
import functools

import jax
import jax.numpy as jnp
from jax import lax
from jax.experimental import pallas as pl
from jax.experimental.pallas import tpu as pltpu

F32 = jnp.float32
BF16 = jnp.bfloat16

D_MODEL = 2048
DEPTH = 4

GLA_HEADS = 4
GLA_DK = D_MODEL // 2
GLA_DV = D_MODEL
GLA_HEAD_K = GLA_DK // GLA_HEADS
GLA_HEAD_V = GLA_DV // GLA_HEADS
GLA_GATE_RANK = 16
GLA_GATE_NORMALIZER = 16.0
GLA_CHUNK = 64
GLA_MAIN = 2 * GLA_DK + 2 * GLA_DV

DIL_PATTERNS = ((128, 1), (512, 4), (2048, 16))
DIL_GROUPS = len(DIL_PATTERNS)
DIL_HEADS = 8
DIL_HEAD_DIM = 128
DIL_WIDTH = DIL_HEADS * DIL_HEAD_DIM
DIL_BLOCK = 128

D_FF = 5504
CONV_WIDTH = 3

DEEPNORM_ALPHA = (2 * DEPTH) ** 0.25
LN_EPS = 1e-5
RMS_EPS = 1e-6

LANES = 128
LOG2E = 1.4426950408889634
FF_TILE = 512
D_FF_PAD = -(-D_FF // FF_TILE) * FF_TILE
VMEM_LIMIT = 56 * 1024 * 1024


def _params(*sem):
    return pltpu.CompilerParams(dimension_semantics=sem, vmem_limit_bytes=VMEM_LIMIT)


def _gla_proj_kernel(x_ref, w_ref, wg_ref, o_ref, g_ref):
    x = x_ref[...]
    o_ref[...] = jnp.dot(x, w_ref[...], preferred_element_type=F32).astype(o_ref.dtype)

    @pl.when(pl.program_id(1) == 0)
    def _():
        g_ref[...] = jnp.dot(x, wg_ref[...], preferred_element_type=F32)


def _gla_proj(x, w_main, w_gate, *, tm=1024, tn=1024):
    m, k = x.shape
    n = w_main.shape[1]
    tm = min(tm, m)
    return pl.pallas_call(
        _gla_proj_kernel,
        out_shape=(jax.ShapeDtypeStruct((m, n), BF16), jax.ShapeDtypeStruct((m, LANES), F32)),
        grid=(m // tm, n // tn),
        in_specs=[pl.BlockSpec((tm, k), lambda i, j: (i, 0)),
                  pl.BlockSpec((k, tn), lambda i, j: (0, j)),
                  pl.BlockSpec((k, LANES), lambda i, j: (0, 0))],
        out_specs=(pl.BlockSpec((tm, tn), lambda i, j: (i, j)),
                   pl.BlockSpec((tm, LANES), lambda i, j: (i, 0))),
        compiler_params=_params("parallel", "arbitrary"),
        name="gla_proj",
    )(x, w_main, w_gate)


GLA_HEADS_PER_STEP = 2
GLA_PAIR_ROWS = 8


def _gla_core_kernel(q_ref, k_ref, v_ref, r_ref, gl_ref, wgu_ref, gb_ref, ng_ref, o_ref,
                     st_ref, a_ref, qd_ref, kd_ref, dec_ref, vt_ref, whi_ref, wlo_ref, *, nchunks):
    C, c = GLA_CHUNK, GLA_PAIR_ROWS
    nsub = C // c
    dk, dv = GLA_HEAD_K, GLA_HEAD_V
    heads = range(GLA_HEADS_PER_STEP)

    rows_i = lax.broadcasted_iota(jnp.int32, (C, C), 0)
    cols_i = lax.broadcasted_iota(jnp.int32, (C, C), 1)
    tri = (rows_i >= cols_i).astype(F32)
    sub_rows = lax.broadcasted_iota(jnp.int32, (c, C), 0)
    sub_cols = lax.broadcasted_iota(jnp.int32, (c, C), 1)
    nt = (((1,), (1,)), ((), ()))
    hi = lax.Precision.HIGHEST

    w = wgu_ref[...]
    whi_ref[...] = w.astype(BF16)
    wlo_ref[...] = (w - whi_ref[...].astype(F32)).astype(BF16)

    def prepare(pi, carry):
        items = [(h, u) for u in range(2) for h in heads]
        rows = [pl.ds(pl.multiple_of((2 * pi + u) * C, C), C) for h, u in items]
        pair_rows = pl.ds(pl.multiple_of(pi * 2 * C, 2 * C), 2 * C)
        q = [q_ref[rs, h * dk:(h + 1) * dk].astype(F32) * (dk ** -0.5) for (h, u), rs in zip(items, rows)]
        k = [k_ref[rs, h * dk:(h + 1) * dk].astype(F32) for (h, u), rs in zip(items, rows)]
        gl = [gl_ref[rs, :] for rs in rows]
        gl_hi = [t.astype(BF16) for t in gl]
        gl_lo = [(t - th.astype(F32)).astype(BF16) for t, th in zip(gl, gl_hi)]
        z = [(jnp.dot(th, whi_ref[:, h * dk:(h + 1) * dk], preferred_element_type=F32)
              + jnp.dot(tl, whi_ref[:, h * dk:(h + 1) * dk], preferred_element_type=F32)
              + jnp.dot(th, wlo_ref[:, h * dk:(h + 1) * dk], preferred_element_type=F32)
              + gb_ref[:, h * dk:(h + 1) * dk]) * LOG2E for (h, u), th, tl in zip(items, gl_hi, gl_lo)]
        g = [(jnp.minimum(zh, 0.0) - jnp.log2(1.0 + jnp.exp2(-jnp.abs(zh)))) * (1.0 / GLA_GATE_NORMALIZER) for zh in z]
        b = [jnp.dot(tri, gh, preferred_element_type=F32, precision=hi) for gh in g]
        b_last = [bh[C - 1:C, :] for bh in b]
        qd = [qh * jnp.exp2(bh) for qh, bh in zip(q, b)]
        kd = [kh * jnp.exp2(bl - bh) for kh, bh, bl in zip(k, b, b_last)]
        dec = [jnp.exp2(bl) for bl in b_last]

        n = len(items)
        subs = [(t, s) for s in range(nsub) for t in range(n)]
        a_s = {}
        for t, s in subs:
            lo = s * c
            if s == 0:
                a_s[t, s] = jnp.zeros((c, C), F32)
            else:
                b_ref = b[t][lo - 1:lo, :]
                q_ref_s = (q[t][lo:lo + c, :] * jnp.exp2(b[t][lo:lo + c, :] - b_ref)).astype(BF16)
                k_ref_s = (k[t][:lo, :] * jnp.exp2(b_ref - b[t][:lo, :])).astype(BF16)
                k_ref_s = jnp.concatenate([k_ref_s, jnp.zeros((C - lo, dk), BF16)], axis=0)
                a_s[t, s] = lax.dot_general(q_ref_s, k_ref_s, nt, preferred_element_type=F32)
        for j in range(c):
            for t, s in subs:
                lo = s * c
                bs = b[t][lo:lo + c, :]
                e = jnp.exp2(bs - bs[j:j + 1, :])
                col = jnp.sum(q[t][lo:lo + c, :] * k[t][lo + j:lo + j + 1, :] * e, axis=-1, keepdims=True)
                a_s[t, s] = jnp.where((sub_cols == lo + j) & (sub_rows >= j), col, a_s[t, s])
        a = [jnp.concatenate([a_s[t, s] for s in range(nsub)], axis=0) for t in range(n)]

        for h in heads:
            t0, t1 = items.index((h, 0)), items.index((h, 1))
            qd0, qd1 = qd[t0].astype(BF16), qd[t1].astype(BF16)
            kd0, kd1 = kd[t0].astype(BF16), kd[t1].astype(BF16)
            cross = lax.dot_general(qd1, kd0, nt, preferred_element_type=F32)
            qd_ref[h, rows[t0], :] = qd0
            qd_ref[h, rows[t1], :] = (qd[t1] * dec[t0]).astype(BF16)
            kd_ref[h, rows[t0], :] = (kd[t0] * dec[t1]).astype(BF16)
            kd_ref[h, rows[t1], :] = kd1
            dec_ref[h, pl.ds(pi, 1), :] = dec[t0] * dec[t1]
            a_ref[h, rows[t0], :] = jnp.concatenate([a[t0], jnp.zeros((C, C), F32)], axis=1).astype(BF16)
            a_ref[h, rows[t1], :] = jnp.concatenate([cross, a[t1]], axis=1).astype(BF16)
            vt_ref[h, pi] = v_ref[pair_rows, h * dv:(h + 1) * dv].astype(F32).T.astype(BF16)
        return carry

    lax.fori_loop(0, nchunks // 2, prepare, 0)

    st_ref[...] = jnp.zeros_like(st_ref)
    ng = ng_ref[...]

    def recur(pi, carry):
        rs = pl.ds(pl.multiple_of(pi * 2 * C, 2 * C), 2 * C)
        st = [st_ref[h] for h in heads]
        o = [lax.dot_general(qd_ref[h, rs, :], st[h].astype(BF16), nt, preferred_element_type=F32) for h in heads]
        o = [o[h] + jnp.dot(a_ref[h, rs, :], v_ref[rs, h * dv:(h + 1) * dv], preferred_element_type=F32) for h in heads]
        upd = [jnp.dot(vt_ref[h, pi], kd_ref[h, rs, :], preferred_element_type=F32) for h in heads]
        for h in heads:
            st_ref[h] = st[h] * dec_ref[h, pl.ds(pi, 1), :] + upd[h]
        o = [oh * lax.rsqrt(jnp.mean(oh * oh, axis=-1, keepdims=True) + RMS_EPS) * ng for oh in o]
        r = [r_ref[rs, h * dv:(h + 1) * dv].astype(F32) for h in heads]
        for h in heads:
            o_ref[rs, h * dv:(h + 1) * dv] = (o[h] * (r[h] / (1.0 + jnp.exp2(r[h] * (-LOG2E))))).astype(o_ref.dtype)
        return carry

    lax.fori_loop(0, nchunks // 2, recur, 0)


def _gla_core(proj, g_low, w_gate_up, gate_bias, norm_g, batch, seq):
    dk, dv, hp = GLA_HEAD_K, GLA_HEAD_V, GLA_HEADS_PER_STEP
    nchunks = seq // GLA_CHUNK
    npairs = nchunks // 2
    q_blk = lambda off: pl.BlockSpec((None, seq, hp * dk), lambda b, h: (b, 0, off // (hp * dk) + h))
    v_blk = lambda off: pl.BlockSpec((None, seq, hp * dv), lambda b, h: (b, 0, off // (hp * dv) + h))
    return pl.pallas_call(
        functools.partial(_gla_core_kernel, nchunks=nchunks),
        out_shape=jax.ShapeDtypeStruct((batch, seq, GLA_DV), BF16),
        grid=(batch, GLA_HEADS // hp),
        in_specs=[q_blk(0), q_blk(GLA_DK), v_blk(2 * GLA_DK), v_blk(2 * GLA_DK + GLA_DV),
                  pl.BlockSpec((None, seq, LANES), lambda b, h: (b, 0, 0)),
                  pl.BlockSpec((LANES, hp * dk), lambda b, h: (0, h)),
                  pl.BlockSpec((1, hp * dk), lambda b, h: (0, h)),
                  pl.BlockSpec((1, dv), lambda b, h: (0, 0))],
        out_specs=pl.BlockSpec((None, seq, hp * dv), lambda b, h: (b, 0, h)),
        scratch_shapes=[pltpu.VMEM((hp, dv, dk), F32),
                        pltpu.VMEM((hp, seq, 2 * GLA_CHUNK), BF16),
                        pltpu.VMEM((hp, seq, dk), BF16),
                        pltpu.VMEM((hp, seq, dk), BF16),
                        pltpu.VMEM((hp, npairs, dk), F32),
                        pltpu.VMEM((hp, npairs, dv, 2 * GLA_CHUNK), BF16),
                        pltpu.VMEM((LANES, hp * dk), BF16),
                        pltpu.VMEM((LANES, hp * dk), BF16)],
        compiler_params=_params("parallel", "arbitrary"),
        name="gla_core",
    )(proj, proj, proj, proj, g_low, w_gate_up, gate_bias, norm_g)


LN_ROWS = 64


def _mm_res_ln_kernel(a_ref, w_ref, x_ref, g_ref, b_ref, of_ref, ob_ref, acc_ref):
    if len(a_ref.shape) == 3:
        a = jnp.concatenate([a_ref[h] for h in range(a_ref.shape[0])], axis=1)
    else:
        a = a_ref[...]
    acc_ref[...] = jnp.dot(a, w_ref[...], preferred_element_type=F32)
    gamma = g_ref[...]
    beta = b_ref[...]

    def rows(i, carry):
        r0 = pl.multiple_of(i * LN_ROWS, LN_ROWS)
        y = DEEPNORM_ALPHA * x_ref[pl.ds(r0, LN_ROWS), :] + acc_ref[pl.ds(r0, LN_ROWS), :]
        mu = jnp.mean(y, axis=-1, keepdims=True)
        yc = y - mu
        var = jnp.mean(yc * yc, axis=-1, keepdims=True)
        out = yc * lax.rsqrt(var + LN_EPS) * gamma + beta
        of_ref[pl.ds(r0, LN_ROWS), :] = out
        ob_ref[pl.ds(r0, LN_ROWS), :] = out.astype(BF16)
        return carry

    lax.fori_loop(0, acc_ref.shape[0] // LN_ROWS, rows, 0)


def _mm_res_ln(a, w, x, gamma, beta, *, tm=512):
    k, n = w.shape
    m = x.shape[0]
    if a.ndim == 4:
        _, heads, seq, dh = a.shape
        tps = seq // tm
        a_spec = pl.BlockSpec((None, heads, tm, dh), lambda i: (i // tps, 0, i % tps, 0))
    else:
        a_spec = pl.BlockSpec((tm, k), lambda i: (i, 0))
    return pl.pallas_call(
        _mm_res_ln_kernel,
        out_shape=(jax.ShapeDtypeStruct((m, n), F32), jax.ShapeDtypeStruct((m, n), BF16)),
        grid=(m // tm,),
        in_specs=[a_spec,
                  pl.BlockSpec((k, n), lambda i: (0, 0)),
                  pl.BlockSpec((tm, n), lambda i: (i, 0)),
                  pl.BlockSpec((1, n), lambda i: (0, 0)),
                  pl.BlockSpec((1, n), lambda i: (0, 0))],
        out_specs=(pl.BlockSpec((tm, n), lambda i: (i, 0)),
                   pl.BlockSpec((tm, n), lambda i: (i, 0))),
        scratch_shapes=[pltpu.VMEM((tm, n), F32)],
        compiler_params=_params("parallel"),
        name="mm_res_ln",
    )(a, w, x, gamma, beta)


FFN_TM = 1024
FFN_CHUNK = 256
FFN_PIECE = 32
HALO = 8


def _ffn_kernel(x_ref, wg_ref, wu_ref, cwg_ref, cwu_ref, cbg_ref, cbu_ref, wd_ref, xres_ref, g_ref, b_ref,
                of_ref, ob_ref, y0_ref, y1_ref, halo_ref, act_ref, *, nj, tiles_per_seq):
    i = pl.program_id(0)
    j = pl.program_id(1)
    tm = x_ref.shape[0]
    tf = wd_ref.shape[0]
    nchunk = tm // FFN_CHUNK

    def conv(h, prev, cw_ref, cb_ref):
        cw = cw_ref[...]
        win = jnp.concatenate([prev, h], axis=0)
        return (cw[2:3, :] * h + cw[1:2, :] * pltpu.roll(win, 1, axis=0)[HALO:, :]
                + cw[0:1, :] * pltpu.roll(win, 2, axis=0)[HALO:, :] + cb_ref[...])

    def up_steps(y_ref):
        @pl.when(i % tiles_per_seq == 0)
        def _():
            halo_ref[j] = jnp.zeros(halo_ref.shape[1:], F32)

        prev = [halo_ref[j, :, half * tf:(half + 1) * tf] for half in range(2)]
        for c in range(nchunk):
            rows = pl.ds(c * FFN_CHUNK, FFN_CHUNK)
            xs = x_ref[rows, :]
            for half, (w_ref, cw_ref, cb_ref) in enumerate(((wg_ref, cwg_ref, cbg_ref), (wu_ref, cwu_ref, cbu_ref))):
                h = jnp.dot(xs, w_ref[...], preferred_element_type=F32)
                y_ref[rows, half * tf:(half + 1) * tf] = conv(h, prev[half], cw_ref, cb_ref)
                prev[half] = h[FFN_CHUNK - HALO:, :]
            yield c
        for half in range(2):
            halo_ref[j, :, half * tf:(half + 1) * tf] = prev[half]

    def down_chunk(c, y_ref):
        for p in range(FFN_CHUNK // FFN_PIECE):
            rows = pl.ds(c * FFN_CHUNK + p * FFN_PIECE, FFN_PIECE)
            gate = y_ref[rows, :tf]
            act_ref[rows, :] = (gate / (1.0 + jnp.exp2(gate * (-LOG2E))) * y_ref[rows, tf:]).astype(BF16)
        rows = pl.ds(c * FFN_CHUNK, FFN_CHUNK)
        of_ref[rows, :] += jnp.dot(act_ref[rows, :], wd_ref[...], preferred_element_type=F32)

    def first_step(y_ref):
        for c in up_steps(y_ref):
            rows = pl.ds(c * FFN_CHUNK, FFN_CHUNK)
            of_ref[rows, :] = DEEPNORM_ALPHA * xres_ref[...] if c == 0 else jnp.zeros((FFN_CHUNK, of_ref.shape[1]), F32)

    def mid_step(y_up, y_prev):
        @pl.when(j < nchunk)
        def _():
            rows = pl.ds(pl.multiple_of(j * FFN_CHUNK, FFN_CHUNK), FFN_CHUNK)
            of_ref[rows, :] += DEEPNORM_ALPHA * xres_ref[...]

        for c in up_steps(y_up):
            down_chunk(c, y_prev)

    def last_step(y_prev):
        gamma = g_ref[...]
        beta = b_ref[...]
        for c in range(nchunk):
            down_chunk(c, y_prev)
            for p in range(FFN_CHUNK // FFN_PIECE):
                rows = pl.ds(c * FFN_CHUNK + p * FFN_PIECE, FFN_PIECE)
                y = of_ref[rows, :]
                mu = jnp.mean(y, axis=-1, keepdims=True)
                yc = y - mu
                var = jnp.mean(yc * yc, axis=-1, keepdims=True)
                out = yc * lax.rsqrt(var + LN_EPS) * gamma + beta
                of_ref[rows, :] = out
                ob_ref[rows, :] = out.astype(BF16)

    slots = (y0_ref, y1_ref)

    @pl.when(j == 0)
    def _():
        first_step(slots[0])

    for parity in (0, 1):
        @pl.when((j > 0) & (j < nj) & (j % 2 == parity))
        def _():
            mid_step(slots[parity], slots[1 - parity])

    @pl.when(j == nj)
    def _():
        last_step(slots[(nj - 1) % 2])


def _ffn(xb, xf, layer, wg, wu, cwg, cwu, cbg, cbu, wd, gamma, beta, seq):
    m, d = xb.shape
    tf = FF_TILE
    nj = wd.shape[1] // tf
    tm = min(FFN_TM, seq)
    nchunk = tm // FFN_CHUNK
    assert nj > nchunk
    one = pl.Buffered(1)
    up_j = lambda j: jnp.minimum(j, nj - 1)
    down_j = lambda j: jnp.maximum(j - 1, 0)
    up_blk = lambda rows: pl.BlockSpec((None, rows, tf), lambda i, j: (layer, 0, up_j(j)))
    return pl.pallas_call(
        functools.partial(_ffn_kernel, nj=nj, tiles_per_seq=seq // tm),
        out_shape=(jax.ShapeDtypeStruct((m, d), F32), jax.ShapeDtypeStruct((m, d), BF16)),
        grid=(m // tm, nj + 1),
        in_specs=[pl.BlockSpec((tm, d), lambda i, j: (i, 0)),
                  up_blk(d), up_blk(d), up_blk(CONV_WIDTH), up_blk(CONV_WIDTH), up_blk(1), up_blk(1),
                  pl.BlockSpec((None, tf, d), lambda i, j: (layer, down_j(j), 0)),
                  pl.BlockSpec((FFN_CHUNK, d), lambda i, j: (i * nchunk + jnp.minimum(j, nchunk - 1), 0)),
                  pl.BlockSpec((1, d), lambda i, j: (0, 0)),
                  pl.BlockSpec((1, d), lambda i, j: (0, 0))],
        out_specs=(pl.BlockSpec((tm, d), lambda i, j: (i, 0), pipeline_mode=one),
                   pl.BlockSpec((tm, d), lambda i, j: (i, 0), pipeline_mode=one)),
        scratch_shapes=[pltpu.VMEM((tm, 2 * tf), F32), pltpu.VMEM((tm, 2 * tf), F32),
                        pltpu.VMEM((nj, HALO, 2 * tf), F32), pltpu.VMEM((tm, tf), BF16)],
        compiler_params=_params("arbitrary", "arbitrary"),
        name="ffn",
    )(xb, wg, wu, cwg, cwu, cbg, cbu, wd, xf, gamma, beta)


def _dil_proj_kernel(x_ref, w_ref, o_ref, *scratch, dil):
    acc = jnp.dot(x_ref[...], w_ref[...], preferred_element_type=F32)
    dh = DIL_HEAD_DIM
    if dil == 1:
        for h in range(DIL_HEADS):
            o_ref[h, 0] = acc[:, h * dh:(h + 1) * dh].astype(o_ref.dtype)
        return
    (rows_ref,) = scratch
    n = x_ref.shape[0] // dil
    for h in range(DIL_HEADS):
        rows_ref[h] = acc[:, h * dh:(h + 1) * dh]
    for h in range(DIL_HEADS):
        for r in range(dil):
            o_ref[h, r] = rows_ref[h, pl.ds(r, n, stride=dil), :].astype(o_ref.dtype)


def _dil_proj(xb, w, dil, batch, seq, *, tm=1024):
    m, k = xb.shape
    H, dh = DIL_HEADS, DIL_HEAD_DIM
    tm = min(tm, seq)
    tps = seq // tm
    scratch = [] if dil == 1 else [pltpu.VMEM((H, tm, dh), F32)]
    return pl.pallas_call(
        functools.partial(_dil_proj_kernel, dil=dil),
        out_shape=jax.ShapeDtypeStruct((batch, 3, H, dil, seq // dil, dh), BF16),
        grid=(m // tm, 3),
        in_specs=[pl.BlockSpec((tm, k), lambda i, c: (i, 0)),
                  pl.BlockSpec((k, H * dh), lambda i, c: (0, c))],
        out_specs=pl.BlockSpec((None, None, H, dil, tm // dil, dh), lambda i, c: (i // tps, c, 0, 0, i % tps, 0)),
        scratch_shapes=scratch,
        compiler_params=_params("parallel", "arbitrary"),
        name=f"dil_proj_d{dil}",
    )(xb, w)


NEG = -1e30
EXP2_SCALE = (DIL_HEAD_DIM ** -0.5) * LOG2E
ATTN_ROWS = 256


def _dil_attn_kernel(*refs, seq):
    qkv = refs[:3 * DIL_GROUPS]
    o_ref = refs[3 * DIL_GROUPS]
    m_ref, l_ref, acc_ref = refs[3 * DIL_GROUPS + 1:]
    P = DIL_BLOCK
    dh = DIL_HEAD_DIM
    nt = (((1,), (1,)), ((), ()))
    qi = lax.broadcasted_iota(jnp.int32, (P, P), 0)
    kj = lax.broadcasted_iota(jnp.int32, (P, P), 1)
    ones = jnp.ones((P, LANES), BF16)
    for gi, (window, dil) in enumerate(DIL_PATTERNS):
        q_ref, k_ref, v_ref = qkv[3 * gi:3 * gi + 3]
        steps = window // dil
        bias_cur = jnp.where((qi >= kj) & (qi - kj <= steps), 0.0, NEG)
        bias_prev = jnp.where(qi + P - kj <= steps, 0.0, NEG)
        blocks = [(r, lb * P) for r in range(dil) for lb in range(seq // dil // P)]
        s_cur = [lax.dot_general(q_ref[r, lo:lo + P, :], k_ref[r, lo:lo + P, :], nt,
                                 preferred_element_type=F32) + bias_cur for r, lo in blocks]
        s_prev = [lax.dot_general(q_ref[r, lo:lo + P, :], k_ref[r, lo - P:lo, :], nt,
                                  preferred_element_type=F32) + bias_prev if lo > 0 else None for r, lo in blocks]
        m = [jnp.max(sc if sp is None else jnp.maximum(sc, sp), axis=-1, keepdims=True)
             for sc, sp in zip(s_cur, s_prev)]
        p_cur = [jnp.exp2((sc - mb) * EXP2_SCALE).astype(BF16) for sc, mb in zip(s_cur, m)]
        p_prev = [None if sp is None else jnp.exp2((sp - mb) * EXP2_SCALE).astype(BF16) for sp, mb in zip(s_prev, m)]
        acc = [jnp.dot(pc, jnp.concatenate([v_ref[r, lo:lo + P, :], ones], axis=1), preferred_element_type=F32)
               for pc, (r, lo) in zip(p_cur, blocks)]
        acc = [ac if pp is None else
               ac + jnp.dot(pp, jnp.concatenate([v_ref[r, lo - P:lo, :], ones], axis=1), preferred_element_type=F32)
               for ac, pp, (r, lo) in zip(acc, p_prev, blocks)]
        for (r, lo), mb, ab in zip(blocks, m, acc):
            tok = pl.ds(lo * dil + r, P, stride=dil) if dil > 1 else pl.ds(lo, P)
            m_ref[gi, tok, :] = jnp.broadcast_to(mb, (P, LANES))
            l_ref[gi, tok, :] = ab[:, dh:]
            acc_ref[gi, tok, :] = ab[:, :dh]
    for c in range(seq // ATTN_ROWS):
        rows = pl.ds(c * ATTN_ROWS, ATTN_ROWS)
        ms = [m_ref[gi, rows, :] for gi in range(DIL_GROUPS)]
        m_all = functools.reduce(jnp.maximum, ms)
        ws = [jnp.exp2((mg - m_all) * EXP2_SCALE) for mg in ms]
        l_all = sum(w * l_ref[gi, rows, :] for gi, w in enumerate(ws))
        acc_all = sum(w * acc_ref[gi, rows, :] for gi, w in enumerate(ws))
        o_ref[rows, :] = (acc_all / l_all).astype(o_ref.dtype)


def _dil_attn(groups, batch, seq):
    dh = DIL_HEAD_DIM
    in_specs, args = [], []
    for (window, dil), t in zip(DIL_PATTERNS, groups):
        for c in range(3):
            in_specs.append(pl.BlockSpec((None, None, None, dil, seq // dil, dh),
                                         lambda b, h, c=c: (b, c, h, 0, 0, 0)))
            args.append(t)
    return pl.pallas_call(
        functools.partial(_dil_attn_kernel, seq=seq),
        out_shape=jax.ShapeDtypeStruct((batch, DIL_HEADS, seq, dh), BF16),
        grid=(batch, DIL_HEADS),
        in_specs=in_specs,
        out_specs=pl.BlockSpec((None, None, seq, dh), lambda b, h: (b, h, 0, 0)),
        scratch_shapes=[pltpu.VMEM((DIL_GROUPS, seq, LANES), F32), pltpu.VMEM((DIL_GROUPS, seq, LANES), F32),
                        pltpu.VMEM((DIL_GROUPS, seq, dh), F32)],
        compiler_params=_params("parallel", "parallel"),
        name="dil_attn",
    )(*args)


def _pad_cols(w, n):
    return jnp.pad(w, ((0, 0), (0, n - w.shape[1])))


def _gla_layer(xf, xb, w_in, w_gate_up, gate_bias, norm_g, w_out, gamma, beta, batch, seq):
    w_main = w_in[:, :GLA_MAIN].astype(BF16)
    w_gate = _pad_cols(w_in[:, GLA_MAIN:], LANES).astype(BF16)
    proj, g_low = _gla_proj(xb, w_main, w_gate)
    wgu = jnp.pad(w_gate_up, ((0, LANES - GLA_GATE_RANK), (0, 0)))
    o = _gla_core(proj.reshape(batch, seq, GLA_MAIN), g_low.reshape(batch, seq, LANES),
                  wgu, gate_bias[None, :], norm_g[None, :], batch, seq)
    return _mm_res_ln(o.reshape(batch * seq, GLA_DV), w_out.astype(BF16), xf, gamma[None, :], beta[None, :])


def _dil_layer(xf, xb, w_in, w_out, gamma, beta, batch, seq):
    gw = 3 * DIL_WIDTH
    w_in = w_in.astype(BF16)
    groups = [_dil_proj(xb, w_in[:, gi * gw:(gi + 1) * gw], dil, batch, seq)
              for gi, (window, dil) in enumerate(DIL_PATTERNS)]
    o = _dil_attn(groups, batch, seq)
    return _mm_res_ln(o, w_out.astype(BF16), xf, gamma[None, :], beta[None, :])


def _ffn_weights(w_up, conv_w, conv_b, w_down):
    pad = D_FF_PAD - D_FF
    halves = lambda t: [jnp.pad(h, ((0, 0), (0, 0), (0, pad))) for h in (t[..., :D_FF], t[..., D_FF:])]
    wg, wu = (h.astype(BF16) for h in halves(w_up))
    cwg, cwu = halves(conv_w)
    cbg, cbu = halves(conv_b[:, None, :])
    wd = jnp.pad(w_down, ((0, 0), (0, pad), (0, 0))).astype(BF16)
    return wg, wu, cwg, cwu, cbg, cbu, wd


def kernel(x, gla_w_in, gla_w_gate_up, gla_gate_bias, gla_norm_g, gla_w_out, dil_w_in, dil_w_out, ffn_w_up,
           ffn_conv_w, ffn_conv_b, ffn_w_down, ln_g, ln_b):
    batch, seq, d = x.shape
    xf = x.reshape(batch * seq, d)
    xb = xf.astype(BF16)
    ffn_weights = _ffn_weights(ffn_w_up, ffn_conv_w, ffn_conv_b, ffn_w_down)
    for i in range(DEPTH):
        j = i // 2
        if i % 2 == 0:
            xf, xb = _gla_layer(xf, xb, gla_w_in[j], gla_w_gate_up[j], gla_gate_bias[j], gla_norm_g[j],
                                gla_w_out[j], ln_g[i, 0], ln_b[i, 0], batch, seq)
        else:
            xf, xb = _dil_layer(xf, xb, dil_w_in[j], dil_w_out[j], ln_g[i, 0], ln_b[i, 0], batch, seq)
        xf, xb = _ffn(xb, xf, i, *ffn_weights, ln_g[i, 1][None, :], ln_b[i, 1][None, :], seq)
    return xf.reshape(batch, seq, d)
```

```python
import functools

import jax
import jax.numpy as jnp
from jax import lax
from jax.experimental import pallas as pl
from jax.experimental.pallas import tpu as pltpu

F32 = jnp.float32
BF16 = jnp.bfloat16

D_MODEL = 2048
DEPTH = 4

GLA_HEADS = 4
GLA_DK = D_MODEL // 2
GLA_DV = D_MODEL
GLA_HEAD_K = GLA_DK // GLA_HEADS
GLA_HEAD_V = GLA_DV // GLA_HEADS
GLA_GATE_RANK = 16
GLA_GATE_NORMALIZER = 16.0
GLA_CHUNK = 64
GLA_MAIN = 2 * GLA_DK + 2 * GLA_DV

DIL_PATTERNS = ((128, 1), (512, 4), (2048, 16))
DIL_GROUPS = len(DIL_PATTERNS)
DIL_HEADS = 8
DIL_HEAD_DIM = 128
DIL_WIDTH = DIL_HEADS * DIL_HEAD_DIM
DIL_BLOCK = 128

D_FF = 5504
CONV_WIDTH = 3

DEEPNORM_ALPHA = (2 * DEPTH) ** 0.25
LN_EPS = 1e-5
RMS_EPS = 1e-6

LANES = 128
LOG2E = 1.4426950408889634
FF_TILE = 512
D_FF_PAD = -(-D_FF // FF_TILE) * FF_TILE
VMEM_LIMIT = 56 * 1024 * 1024


def _params(*sem):
    return pltpu.CompilerParams(dimension_semantics=sem, vmem_limit_bytes=VMEM_LIMIT)


def _gla_proj_kernel(x_ref, w_ref, wg_ref, o_ref, g_ref, *scratch):
    if scratch:
        (xb_ref,) = scratch

        @pl.when(pl.program_id(1) == 0)
        def _():
            xb_ref[...] = x_ref[...].astype(BF16)
    else:
        xb_ref = x_ref
    x = xb_ref[...]
    o_ref[...] = jnp.dot(x, w_ref[...], preferred_element_type=F32).astype(o_ref.dtype)

    @pl.when(pl.program_id(1) == 0)
    def _():
        g_ref[...] = jnp.dot(x, wg_ref[...], preferred_element_type=F32)


def _gla_proj(x, w_main, w_gate, *, tm=1024, tn=1024):
    m, k = x.shape
    n = w_main.shape[1]
    tm = min(tm, m)
    scratch = [] if x.dtype == BF16 else [pltpu.VMEM((tm, k), BF16)]
    return pl.pallas_call(
        _gla_proj_kernel,
        out_shape=(jax.ShapeDtypeStruct((m, n), BF16), jax.ShapeDtypeStruct((m, LANES), F32)),
        grid=(m // tm, n // tn),
        in_specs=[pl.BlockSpec((tm, k), lambda i, j: (i, 0)),
                  pl.BlockSpec((k, tn), lambda i, j: (0, j)),
                  pl.BlockSpec((k, LANES), lambda i, j: (0, 0))],
        out_specs=(pl.BlockSpec((tm, tn), lambda i, j: (i, j)),
                   pl.BlockSpec((tm, LANES), lambda i, j: (i, 0))),
        scratch_shapes=scratch,
        compiler_params=_params("parallel", "arbitrary"),
        name="gla_proj",
    )(x, w_main, w_gate)


GLA_HEADS_PER_STEP = 2
GLA_PAIR_ROWS = 8


def _gla_core_kernel(q_ref, k_ref, v_ref, r_ref, gl_ref, wgu_ref, gb_ref, ng_ref, o_ref,
                     st_ref, a_ref, qd_ref, kd_ref, dec_ref, vt_ref, whi_ref, wlo_ref, *, nchunks):
    C, c = GLA_CHUNK, GLA_PAIR_ROWS
    nsub = C // c
    dk, dv = GLA_HEAD_K, GLA_HEAD_V
    heads = range(GLA_HEADS_PER_STEP)

    rows_i = lax.broadcasted_iota(jnp.int32, (C, C), 0)
    cols_i = lax.broadcasted_iota(jnp.int32, (C, C), 1)
    tri = (rows_i >= cols_i).astype(F32)
    sub_rows = lax.broadcasted_iota(jnp.int32, (c, C), 0)
    sub_cols = lax.broadcasted_iota(jnp.int32, (c, C), 1)
    nt = (((1,), (1,)), ((), ()))
    hi = lax.Precision.HIGHEST

    w = wgu_ref[...]
    whi_ref[...] = w.astype(BF16)
    wlo_ref[...] = (w - whi_ref[...].astype(F32)).astype(BF16)

    def prepare(pi, carry):
        items = [(h, u) for u in range(2) for h in heads]
        rows = [pl.ds(pl.multiple_of((2 * pi + u) * C, C), C) for h, u in items]
        pair_rows = pl.ds(pl.multiple_of(pi * 2 * C, 2 * C), 2 * C)
        q = [q_ref[rs, h * dk:(h + 1) * dk].astype(F32) * (dk ** -0.5) for (h, u), rs in zip(items, rows)]
        k = [k_ref[rs, h * dk:(h + 1) * dk].astype(F32) for (h, u), rs in zip(items, rows)]
        gl = [gl_ref[rs, :] for rs in rows]
        gl_hi = [t.astype(BF16) for t in gl]
        gl_lo = [(t - th.astype(F32)).astype(BF16) for t, th in zip(gl, gl_hi)]
        z = [(jnp.dot(th, whi_ref[:, h * dk:(h + 1) * dk], preferred_element_type=F32)
              + jnp.dot(tl, whi_ref[:, h * dk:(h + 1) * dk], preferred_element_type=F32)
              + jnp.dot(th, wlo_ref[:, h * dk:(h + 1) * dk], preferred_element_type=F32)
              + gb_ref[:, h * dk:(h + 1) * dk]) * LOG2E for (h, u), th, tl in zip(items, gl_hi, gl_lo)]
        g = [(jnp.minimum(zh, 0.0) - jnp.log2(1.0 + jnp.exp2(-jnp.abs(zh)))) * (1.0 / GLA_GATE_NORMALIZER) for zh in z]
        b = [jnp.dot(tri, gh, preferred_element_type=F32, precision=hi) for gh in g]
        b_last = [bh[C - 1:C, :] for bh in b]
        qd = [qh * jnp.exp2(bh) for qh, bh in zip(q, b)]
        kd = [kh * jnp.exp2(bl - bh) for kh, bh, bl in zip(k, b, b_last)]
        dec = [jnp.exp2(bl) for bl in b_last]

        n = len(items)
        subs = [(t, s) for s in range(nsub) for t in range(n)]
        a_s = {}
        for t, s in subs:
            lo = s * c
            if s == 0:
                a_s[t, s] = jnp.zeros((c, C), F32)
            else:
                b_ref = b[t][lo - 1:lo, :]
                q_ref_s = (q[t][lo:lo + c, :] * jnp.exp2(b[t][lo:lo + c, :] - b_ref)).astype(BF16)
                k_ref_s = (k[t][:lo, :] * jnp.exp2(b_ref - b[t][:lo, :])).astype(BF16)
                k_ref_s = jnp.concatenate([k_ref_s, jnp.zeros((C - lo, dk), BF16)], axis=0)
                a_s[t, s] = lax.dot_general(q_ref_s, k_ref_s, nt, preferred_element_type=F32)
        for j in range(c):
            for t, s in subs:
                lo = s * c
                bs = b[t][lo:lo + c, :]
                e = jnp.exp2(bs - bs[j:j + 1, :])
                col = jnp.sum(q[t][lo:lo + c, :] * k[t][lo + j:lo + j + 1, :] * e, axis=-1, keepdims=True)
                a_s[t, s] = jnp.where((sub_cols == lo + j) & (sub_rows >= j), col, a_s[t, s])
        a = [jnp.concatenate([a_s[t, s] for s in range(nsub)], axis=0) for t in range(n)]

        for h in heads:
            t0, t1 = items.index((h, 0)), items.index((h, 1))
            qd0, qd1 = qd[t0].astype(BF16), qd[t1].astype(BF16)
            kd0, kd1 = kd[t0].astype(BF16), kd[t1].astype(BF16)
            cross = lax.dot_general(qd1, kd0, nt, preferred_element_type=F32)
            qd_ref[h, rows[t0], :] = qd0
            qd_ref[h, rows[t1], :] = (qd[t1] * dec[t0]).astype(BF16)
            kd_ref[h, rows[t0], :] = (kd[t0] * dec[t1]).astype(BF16)
            kd_ref[h, rows[t1], :] = kd1
            dec_ref[h, pl.ds(pi, 1), :] = dec[t0] * dec[t1]
            a_ref[h, rows[t0], :] = jnp.concatenate([a[t0], jnp.zeros((C, C), F32)], axis=1).astype(BF16)
            a_ref[h, rows[t1], :] = jnp.concatenate([cross, a[t1]], axis=1).astype(BF16)
            vt_ref[h, pi] = v_ref[pair_rows, h * dv:(h + 1) * dv].astype(F32).T.astype(BF16)
        return carry

    lax.fori_loop(0, nchunks // 2, prepare, 0)

    st_ref[...] = jnp.zeros_like(st_ref)
    ng = ng_ref[...]

    def recur(pi, carry):
        rs = pl.ds(pl.multiple_of(pi * 2 * C, 2 * C), 2 * C)
        st = [st_ref[h] for h in heads]
        o = [lax.dot_general(qd_ref[h, rs, :], st[h].astype(BF16), nt, preferred_element_type=F32) for h in heads]
        o = [o[h] + jnp.dot(a_ref[h, rs, :], v_ref[rs, h * dv:(h + 1) * dv], preferred_element_type=F32) for h in heads]
        upd = [jnp.dot(vt_ref[h, pi], kd_ref[h, rs, :], preferred_element_type=F32) for h in heads]
        for h in heads:
            st_ref[h] = st[h] * dec_ref[h, pl.ds(pi, 1), :] + upd[h]
        o = [oh * lax.rsqrt(jnp.mean(oh * oh, axis=-1, keepdims=True) + RMS_EPS) * ng for oh in o]
        r = [r_ref[rs, h * dv:(h + 1) * dv].astype(F32) for h in heads]
        for h in heads:
            o_ref[rs, h * dv:(h + 1) * dv] = (o[h] * (r[h] / (1.0 + jnp.exp2(r[h] * (-LOG2E))))).astype(o_ref.dtype)
        return carry

    lax.fori_loop(0, nchunks // 2, recur, 0)


def _gla_core(proj, g_low, w_gate_up, gate_bias, norm_g, batch, seq):
    dk, dv, hp = GLA_HEAD_K, GLA_HEAD_V, GLA_HEADS_PER_STEP
    nchunks = seq // GLA_CHUNK
    npairs = nchunks // 2
    q_blk = lambda off: pl.BlockSpec((None, seq, hp * dk), lambda b, h: (b, 0, off // (hp * dk) + h))
    v_blk = lambda off: pl.BlockSpec((None, seq, hp * dv), lambda b, h: (b, 0, off // (hp * dv) + h))
    return pl.pallas_call(
        functools.partial(_gla_core_kernel, nchunks=nchunks),
        out_shape=jax.ShapeDtypeStruct((batch, seq, GLA_DV), BF16),
        grid=(batch, GLA_HEADS // hp),
        in_specs=[q_blk(0), q_blk(GLA_DK), v_blk(2 * GLA_DK), v_blk(2 * GLA_DK + GLA_DV),
                  pl.BlockSpec((None, seq, LANES), lambda b, h: (b, 0, 0)),
                  pl.BlockSpec((LANES, hp * dk), lambda b, h: (0, h)),
                  pl.BlockSpec((1, hp * dk), lambda b, h: (0, h)),
                  pl.BlockSpec((1, dv), lambda b, h: (0, 0))],
        out_specs=pl.BlockSpec((None, seq, hp * dv), lambda b, h: (b, 0, h)),
        scratch_shapes=[pltpu.VMEM((hp, dv, dk), F32),
                        pltpu.VMEM((hp, seq, 2 * GLA_CHUNK), BF16),
                        pltpu.VMEM((hp, seq, dk), BF16),
                        pltpu.VMEM((hp, seq, dk), BF16),
                        pltpu.VMEM((hp, npairs, dk), F32),
                        pltpu.VMEM((hp, npairs, dv, 2 * GLA_CHUNK), BF16),
                        pltpu.VMEM((LANES, hp * dk), BF16),
                        pltpu.VMEM((LANES, hp * dk), BF16)],
        compiler_params=_params("parallel", "arbitrary"),
        name="gla_core",
    )(proj, proj, proj, proj, g_low, w_gate_up, gate_bias, norm_g)


LN_ROWS = 64


def _mm_res_ln_kernel(a_ref, w_ref, x_ref, g_ref, b_ref, of_ref, ob_ref, acc_ref):
    if len(a_ref.shape) == 3:
        a = jnp.concatenate([a_ref[h] for h in range(a_ref.shape[0])], axis=1)
    else:
        a = a_ref[...]
    acc_ref[...] = jnp.dot(a, w_ref[...], preferred_element_type=F32)
    gamma = g_ref[...]
    beta = b_ref[...]

    def rows(i, carry):
        r0 = pl.multiple_of(i * LN_ROWS, LN_ROWS)
        y = DEEPNORM_ALPHA * x_ref[pl.ds(r0, LN_ROWS), :] + acc_ref[pl.ds(r0, LN_ROWS), :]
        mu = jnp.mean(y, axis=-1, keepdims=True)
        yc = y - mu
        var = jnp.mean(yc * yc, axis=-1, keepdims=True)
        out = yc * lax.rsqrt(var + LN_EPS) * gamma + beta
        of_ref[pl.ds(r0, LN_ROWS), :] = out
        ob_ref[pl.ds(r0, LN_ROWS), :] = out.astype(BF16)
        return carry

    lax.fori_loop(0, acc_ref.shape[0] // LN_ROWS, rows, 0)


def _mm_res_ln(a, w, x, gamma, beta, *, tm=512):
    k, n = w.shape
    m = x.shape[0]
    if a.ndim == 4:
        _, heads, seq, dh = a.shape
        tps = seq // tm
        a_spec = pl.BlockSpec((None, heads, tm, dh), lambda i: (i // tps, 0, i % tps, 0))
    else:
        a_spec = pl.BlockSpec((tm, k), lambda i: (i, 0))
    return pl.pallas_call(
        _mm_res_ln_kernel,
        out_shape=(jax.ShapeDtypeStruct((m, n), F32), jax.ShapeDtypeStruct((m, n), BF16)),
        grid=(m // tm,),
        in_specs=[a_spec,
                  pl.BlockSpec((k, n), lambda i: (0, 0)),
                  pl.BlockSpec((tm, n), lambda i: (i, 0)),
                  pl.BlockSpec((1, n), lambda i: (0, 0)),
                  pl.BlockSpec((1, n), lambda i: (0, 0))],
        out_specs=(pl.BlockSpec((tm, n), lambda i: (i, 0)),
                   pl.BlockSpec((tm, n), lambda i: (i, 0))),
        scratch_shapes=[pltpu.VMEM((tm, n), F32)],
        compiler_params=_params("parallel"),
        name="mm_res_ln",
    )(a, w, x, gamma, beta)


FFN_TM = 1024
FFN_CHUNK = 256
FFN_PIECE = 32
HALO = 8


def _ffn_kernel(x_ref, wg_ref, wu_ref, cwg_ref, cwu_ref, cbg_ref, cbu_ref, wd_ref, xres_ref, g_ref, b_ref,
                of_ref, ob_ref, y0_ref, y1_ref, halo_ref, act_ref, *, nj, tiles_per_seq):
    i = pl.program_id(0)
    j = pl.program_id(1)
    tm = x_ref.shape[0]
    tf = wd_ref.shape[0]
    nchunk = tm // FFN_CHUNK

    def conv(h, prev, cw_ref, cb_ref):
        cw = cw_ref[...]
        win = jnp.concatenate([prev, h], axis=0)
        return (cw[2:3, :] * h + cw[1:2, :] * pltpu.roll(win, 1, axis=0)[HALO:, :]
                + cw[0:1, :] * pltpu.roll(win, 2, axis=0)[HALO:, :] + cb_ref[...])

    def up_steps(y_ref):
        @pl.when(i % tiles_per_seq == 0)
        def _():
            halo_ref[j] = jnp.zeros(halo_ref.shape[1:], F32)

        prev = [halo_ref[j, :, half * tf:(half + 1) * tf] for half in range(2)]
        for c in range(nchunk):
            rows = pl.ds(c * FFN_CHUNK, FFN_CHUNK)
            xs = x_ref[rows, :]
            for half, (w_ref, cw_ref, cb_ref) in enumerate(((wg_ref, cwg_ref, cbg_ref), (wu_ref, cwu_ref, cbu_ref))):
                h = jnp.dot(xs, w_ref[...], preferred_element_type=F32)
                y_ref[rows, half * tf:(half + 1) * tf] = conv(h, prev[half], cw_ref, cb_ref)
                prev[half] = h[FFN_CHUNK - HALO:, :]
            yield c
        for half in range(2):
            halo_ref[j, :, half * tf:(half + 1) * tf] = prev[half]

    def down_chunk(c, y_ref):
        for p in range(FFN_CHUNK // FFN_PIECE):
            rows = pl.ds(c * FFN_CHUNK + p * FFN_PIECE, FFN_PIECE)
            gate = y_ref[rows, :tf]
            act_ref[rows, :] = (gate / (1.0 + jnp.exp2(gate * (-LOG2E))) * y_ref[rows, tf:]).astype(BF16)
        rows = pl.ds(c * FFN_CHUNK, FFN_CHUNK)
        of_ref[rows, :] += jnp.dot(act_ref[rows, :], wd_ref[...], preferred_element_type=F32)

    def first_step(y_ref):
        for c in up_steps(y_ref):
            rows = pl.ds(c * FFN_CHUNK, FFN_CHUNK)
            of_ref[rows, :] = DEEPNORM_ALPHA * xres_ref[...] if c == 0 else jnp.zeros((FFN_CHUNK, of_ref.shape[1]), F32)

    def mid_step(y_up, y_prev):
        @pl.when(j < nchunk)
        def _():
            rows = pl.ds(pl.multiple_of(j * FFN_CHUNK, FFN_CHUNK), FFN_CHUNK)
            of_ref[rows, :] += DEEPNORM_ALPHA * xres_ref[...]

        for c in up_steps(y_up):
            down_chunk(c, y_prev)

    def last_step(y_prev):
        gamma = g_ref[...]
        beta = b_ref[...]
        for c in range(nchunk):
            down_chunk(c, y_prev)
            for p in range(FFN_CHUNK // FFN_PIECE):
                rows = pl.ds(c * FFN_CHUNK + p * FFN_PIECE, FFN_PIECE)
                y = of_ref[rows, :]
                mu = jnp.mean(y, axis=-1, keepdims=True)
                yc = y - mu
                var = jnp.mean(yc * yc, axis=-1, keepdims=True)
                out = yc * lax.rsqrt(var + LN_EPS) * gamma + beta
                of_ref[rows, :] = out
                ob_ref[rows, :] = out.astype(BF16)

    slots = (y0_ref, y1_ref)

    @pl.when(j == 0)
    def _():
        first_step(slots[0])

    for parity in (0, 1):
        @pl.when((j > 0) & (j < nj) & (j % 2 == parity))
        def _():
            mid_step(slots[parity], slots[1 - parity])

    @pl.when(j == nj)
    def _():
        last_step(slots[(nj - 1) % 2])


def _ffn(xb, xf, layer, wg, wu, cwg, cwu, cbg, cbu, wd, gamma, beta, seq):
    m, d = xb.shape
    tf = FF_TILE
    nj = wd.shape[1] // tf
    tm = min(FFN_TM, seq)
    nchunk = tm // FFN_CHUNK
    assert nj > nchunk
    one = pl.Buffered(1)
    up_j = lambda j: jnp.minimum(j, nj - 1)
    down_j = lambda j: jnp.maximum(j - 1, 0)
    up_blk = lambda rows: pl.BlockSpec((None, rows, tf), lambda i, j: (layer, 0, up_j(j)))
    return pl.pallas_call(
        functools.partial(_ffn_kernel, nj=nj, tiles_per_seq=seq // tm),
        out_shape=(jax.ShapeDtypeStruct((m, d), F32), jax.ShapeDtypeStruct((m, d), BF16)),
        grid=(m // tm, nj + 1),
        in_specs=[pl.BlockSpec((tm, d), lambda i, j: (i, 0)),
                  up_blk(d), up_blk(d), up_blk(CONV_WIDTH), up_blk(CONV_WIDTH), up_blk(1), up_blk(1),
                  pl.BlockSpec((None, tf, d), lambda i, j: (layer, down_j(j), 0)),
                  pl.BlockSpec((FFN_CHUNK, d), lambda i, j: (i * nchunk + jnp.minimum(j, nchunk - 1), 0)),
                  pl.BlockSpec((1, d), lambda i, j: (0, 0)),
                  pl.BlockSpec((1, d), lambda i, j: (0, 0))],
        out_specs=(pl.BlockSpec((tm, d), lambda i, j: (i, 0), pipeline_mode=one),
                   pl.BlockSpec((tm, d), lambda i, j: (i, 0))),
        scratch_shapes=[pltpu.VMEM((tm, 2 * tf), F32), pltpu.VMEM((tm, 2 * tf), F32),
                        pltpu.VMEM((nj, HALO, 2 * tf), F32), pltpu.VMEM((tm, tf), BF16)],
        compiler_params=_params("arbitrary", "arbitrary"),
        name="ffn",
    )(xb, wg, wu, cwg, cwu, cbg, cbu, wd, xf, gamma, beta)


DIL_PERMUTE_FROM = 16
DIL_PERMUTE_ROWS = 256


def _dil_proj_kernel(x_ref, w_ref, o_ref, *scratch, dil):
    acc = jnp.dot(x_ref[...], w_ref[...], preferred_element_type=F32)
    dh = DIL_HEAD_DIM
    if dil == 1:
        for h in range(DIL_HEADS):
            o_ref[h, 0] = acc[:, h * dh:(h + 1) * dh].astype(o_ref.dtype)
        return
    if dil < DIL_PERMUTE_FROM:
        (rows_ref,) = scratch
        n = x_ref.shape[0] // dil
        for h in range(DIL_HEADS):
            rows_ref[h] = acc[:, h * dh:(h + 1) * dh]
        for h in range(DIL_HEADS):
            for r in range(dil):
                o_ref[h, r] = rows_ref[h, pl.ds(r, n, stride=dil), :].astype(o_ref.dtype)
        return
    blk = DIL_PERMUTE_ROWS
    n = blk // dil
    dst = lax.broadcasted_iota(jnp.int32, (blk, blk), 0)
    src = lax.broadcasted_iota(jnp.int32, (blk, blk), 1)
    perm = jnp.where((dst // n == src % dil) & (dst % n == src // dil), 1.0, 0.0).astype(BF16)
    y = acc.astype(BF16)
    for q in range(x_ref.shape[0] // blk):
        p = jnp.dot(perm, y[q * blk:(q + 1) * blk, :], preferred_element_type=F32).astype(o_ref.dtype)
        for h in range(DIL_HEADS):
            for r in range(dil):
                o_ref[h, r, q * n:(q + 1) * n, :] = p[r * n:(r + 1) * n, h * dh:(h + 1) * dh]


def _dil_proj(xb, w, dil, batch, seq, *, tm=1024):
    m, k = xb.shape
    H, dh = DIL_HEADS, DIL_HEAD_DIM
    tm = min(tm, seq)
    tps = seq // tm
    scratch = [pltpu.VMEM((H, tm, dh), F32)] if 1 < dil < DIL_PERMUTE_FROM else []
    return pl.pallas_call(
        functools.partial(_dil_proj_kernel, dil=dil),
        out_shape=jax.ShapeDtypeStruct((batch, 3, H, dil, seq // dil, dh), BF16),
        grid=(m // tm, 3),
        in_specs=[pl.BlockSpec((tm, k), lambda i, c: (i, 0)),
                  pl.BlockSpec((k, H * dh), lambda i, c: (0, c))],
        out_specs=pl.BlockSpec((None, None, H, dil, tm // dil, dh), lambda i, c: (i // tps, c, 0, 0, i % tps, 0)),
        scratch_shapes=scratch,
        compiler_params=_params("parallel", "arbitrary"),
        name=f"dil_proj_d{dil}",
    )(xb, w)


NEG = -1e30
EXP2_SCALE = (DIL_HEAD_DIM ** -0.5) * LOG2E
ATTN_ROWS = 256


def _dil_attn_kernel(*refs, seq):
    qkv = refs[:3 * DIL_GROUPS]
    o_ref = refs[3 * DIL_GROUPS]
    m_ref, l_ref, acc_ref = refs[3 * DIL_GROUPS + 1:]
    P = DIL_BLOCK
    dh = DIL_HEAD_DIM
    nt = (((1,), (1,)), ((), ()))
    qi = lax.broadcasted_iota(jnp.int32, (P, P), 0)
    kj = lax.broadcasted_iota(jnp.int32, (P, P), 1)
    ones = jnp.ones((P, LANES), BF16)
    for gi, (window, dil) in enumerate(DIL_PATTERNS):
        q_ref, k_ref, v_ref = qkv[3 * gi:3 * gi + 3]
        steps = window // dil
        bias_cur = jnp.where((qi >= kj) & (qi - kj <= steps), 0.0, NEG)
        bias_prev = jnp.where(qi + P - kj <= steps, 0.0, NEG)
        blocks = [(r, lb * P) for r in range(dil) for lb in range(seq // dil // P)]
        s_cur = [lax.dot_general(q_ref[r, lo:lo + P, :], k_ref[r, lo:lo + P, :], nt,
                                 preferred_element_type=F32) + bias_cur for r, lo in blocks]
        s_prev = [lax.dot_general(q_ref[r, lo:lo + P, :], k_ref[r, lo - P:lo, :], nt,
                                  preferred_element_type=F32) + bias_prev if lo > 0 else None for r, lo in blocks]
        m = [jnp.max(sc if sp is None else jnp.maximum(sc, sp), axis=-1, keepdims=True)
             for sc, sp in zip(s_cur, s_prev)]
        p_cur = [jnp.exp2((sc - mb) * EXP2_SCALE).astype(BF16) for sc, mb in zip(s_cur, m)]
        p_prev = [None if sp is None else jnp.exp2((sp - mb) * EXP2_SCALE).astype(BF16) for sp, mb in zip(s_prev, m)]
        acc = [jnp.dot(pc, jnp.concatenate([v_ref[r, lo:lo + P, :], ones], axis=1), preferred_element_type=F32)
               for pc, (r, lo) in zip(p_cur, blocks)]
        acc = [ac if pp is None else
               ac + jnp.dot(pp, jnp.concatenate([v_ref[r, lo - P:lo, :], ones], axis=1), preferred_element_type=F32)
               for ac, pp, (r, lo) in zip(acc, p_prev, blocks)]
        for (r, lo), mb, ab in zip(blocks, m, acc):
            tok = pl.ds(lo * dil + r, P, stride=dil) if dil > 1 else pl.ds(lo, P)
            m_ref[gi, tok, :] = jnp.broadcast_to(mb, (P, LANES))
            l_ref[gi, tok, :] = ab[:, dh:]
            acc_ref[gi, tok, :] = ab[:, :dh]
    for c in range(seq // ATTN_ROWS):
        rows = pl.ds(c * ATTN_ROWS, ATTN_ROWS)
        ms = [m_ref[gi, rows, :] for gi in range(DIL_GROUPS)]
        m_all = functools.reduce(jnp.maximum, ms)
        ws = [jnp.exp2((mg - m_all) * EXP2_SCALE) for mg in ms]
        l_all = sum(w * l_ref[gi, rows, :] for gi, w in enumerate(ws))
        acc_all = sum(w * acc_ref[gi, rows, :] for gi, w in enumerate(ws))
        o_ref[rows, :] = (acc_all / l_all).astype(o_ref.dtype)


def _dil_attn(groups, batch, seq):
    dh = DIL_HEAD_DIM
    in_specs, args = [], []
    for (window, dil), t in zip(DIL_PATTERNS, groups):
        for c in range(3):
            in_specs.append(pl.BlockSpec((None, None, None, dil, seq // dil, dh),
                                         lambda b, h, c=c: (b, c, h, 0, 0, 0)))
            args.append(t)
    return pl.pallas_call(
        functools.partial(_dil_attn_kernel, seq=seq),
        out_shape=jax.ShapeDtypeStruct((batch, DIL_HEADS, seq, dh), BF16),
        grid=(batch, DIL_HEADS),
        in_specs=in_specs,
        out_specs=pl.BlockSpec((None, None, seq, dh), lambda b, h: (b, h, 0, 0)),
        scratch_shapes=[pltpu.VMEM((DIL_GROUPS, seq, LANES), F32), pltpu.VMEM((DIL_GROUPS, seq, LANES), F32),
                        pltpu.VMEM((DIL_GROUPS, seq, dh), F32)],
        compiler_params=_params("parallel", "parallel"),
        name="dil_attn",
    )(*args)


def _pad_cols(w, n):
    return jnp.pad(w, ((0, 0), (0, n - w.shape[1])))


def _gla_layer(xf, xb, w_in, w_gate_up, gate_bias, norm_g, w_out, gamma, beta, batch, seq):
    w_main = w_in[:, :GLA_MAIN].astype(BF16)
    w_gate = _pad_cols(w_in[:, GLA_MAIN:], LANES).astype(BF16)
    proj, g_low = _gla_proj(xf if xb is None else xb, w_main, w_gate)
    wgu = jnp.pad(w_gate_up, ((0, LANES - GLA_GATE_RANK), (0, 0)))
    o = _gla_core(proj.reshape(batch, seq, GLA_MAIN), g_low.reshape(batch, seq, LANES),
                  wgu, gate_bias[None, :], norm_g[None, :], batch, seq)
    return _mm_res_ln(o.reshape(batch * seq, GLA_DV), w_out.astype(BF16), xf, gamma[None, :], beta[None, :])


def _dil_layer(xf, xb, w_in, w_out, gamma, beta, batch, seq):
    gw = 3 * DIL_WIDTH
    w_in = w_in.astype(BF16)
    groups = [_dil_proj(xb, w_in[:, gi * gw:(gi + 1) * gw], dil, batch, seq)
              for gi, (window, dil) in enumerate(DIL_PATTERNS)]
    o = _dil_attn(groups, batch, seq)
    return _mm_res_ln(o, w_out.astype(BF16), xf, gamma[None, :], beta[None, :])


def _ffn_weights(w_up, conv_w, conv_b, w_down):
    pad = D_FF_PAD - D_FF
    halves = lambda t: [jnp.pad(h, ((0, 0), (0, 0), (0, pad))) for h in (t[..., :D_FF], t[..., D_FF:])]
    wg, wu = (h.astype(BF16) for h in halves(w_up))
    cwg, cwu = halves(conv_w)
    cbg, cbu = halves(conv_b[:, None, :])
    wd = jnp.pad(w_down, ((0, 0), (0, pad), (0, 0))).astype(BF16)
    return wg, wu, cwg, cwu, cbg, cbu, wd


def kernel(x, gla_w_in, gla_w_gate_up, gla_gate_bias, gla_norm_g, gla_w_out, dil_w_in, dil_w_out, ffn_w_up,
           ffn_conv_w, ffn_conv_b, ffn_w_down, ln_g, ln_b):
    batch, seq, d = x.shape
    xf = x.reshape(batch * seq, d)
    xb = None
    ffn_weights = _ffn_weights(ffn_w_up, ffn_conv_w, ffn_conv_b, ffn_w_down)
    for i in range(DEPTH):
        j = i // 2
        if i % 2 == 0:
            xf, xb = _gla_layer(xf, xb, gla_w_in[j], gla_w_gate_up[j], gla_gate_bias[j], gla_norm_g[j],
                                gla_w_out[j], ln_g[i, 0], ln_b[i, 0], batch, seq)
        else:
            xf, xb = _dil_layer(xf, xb, dil_w_in[j], dil_w_out[j], ln_g[i, 0], ln_b[i, 0], batch, seq)
        xf, xb = _ffn(xb, xf, i, *ffn_weights, ln_g[i, 1][None, :], ln_b[i, 1][None, :], seq)
    return xf.reshape(batch, seq, d)
```

```python
import functools

import jax
import jax.numpy as jnp
from jax import lax
from jax.experimental import pallas as pl
from jax.experimental.pallas import tpu as pltpu

F32 = jnp.float32
BF16 = jnp.bfloat16

D_MODEL = 2048
DEPTH = 4

GLA_HEADS = 4
GLA_DK = D_MODEL // 2
GLA_DV = D_MODEL
GLA_HEAD_K = GLA_DK // GLA_HEADS
GLA_HEAD_V = GLA_DV // GLA_HEADS
GLA_GATE_RANK = 16
GLA_GATE_NORMALIZER = 16.0
GLA_CHUNK = 64
GLA_MAIN = 2 * GLA_DK + 2 * GLA_DV

DIL_PATTERNS = ((128, 1), (512, 4), (2048, 16))
DIL_GROUPS = len(DIL_PATTERNS)
DIL_HEADS = 8
DIL_HEAD_DIM = 128
DIL_WIDTH = DIL_HEADS * DIL_HEAD_DIM
DIL_BLOCK = 128

D_FF = 5504
CONV_WIDTH = 3

DEEPNORM_ALPHA = (2 * DEPTH) ** 0.25
LN_EPS = 1e-5
RMS_EPS = 1e-6

LANES = 128
LOG2E = 1.4426950408889634
FF_TILE = 512
D_FF_PAD = -(-D_FF // FF_TILE) * FF_TILE
VMEM_LIMIT = 56 * 1024 * 1024


def _params(*sem):
    return pltpu.CompilerParams(dimension_semantics=sem, vmem_limit_bytes=VMEM_LIMIT)


def _gla_proj_kernel(x_ref, w_ref, wg_ref, o_ref, g_ref, *scratch):
    if scratch:
        (xb_ref,) = scratch

        @pl.when(pl.program_id(1) == 0)
        def _():
            xb_ref[...] = x_ref[...].astype(BF16)
    else:
        xb_ref = x_ref
    x = xb_ref[...]
    o_ref[...] = jnp.dot(x, w_ref[...], preferred_element_type=F32).astype(o_ref.dtype)

    @pl.when(pl.program_id(1) == 0)
    def _():
        g_ref[...] = jnp.dot(x, wg_ref[...], preferred_element_type=F32)


def _gla_proj(x, w_main, w_gate, *, tm=1024, tn=1024):
    m, k = x.shape
    n = w_main.shape[1]
    tm = min(tm, m)
    scratch = [] if x.dtype == BF16 else [pltpu.VMEM((tm, k), BF16)]
    return pl.pallas_call(
        _gla_proj_kernel,
        out_shape=(jax.ShapeDtypeStruct((m, n), BF16), jax.ShapeDtypeStruct((m, LANES), F32)),
        grid=(m // tm, n // tn),
        in_specs=[pl.BlockSpec((tm, k), lambda i, j: (i, 0)),
                  pl.BlockSpec((k, tn), lambda i, j: (0, j)),
                  pl.BlockSpec((k, LANES), lambda i, j: (0, 0))],
        out_specs=(pl.BlockSpec((tm, tn), lambda i, j: (i, j)),
                   pl.BlockSpec((tm, LANES), lambda i, j: (i, 0))),
        scratch_shapes=scratch,
        compiler_params=_params("parallel", "arbitrary"),
        name="gla_proj",
    )(x, w_main, w_gate)


GLA_HEADS_PER_STEP = 2
GLA_PAIRS_PER_STEP = 2
GLA_PAIR_ROWS = 8


def _gla_core_kernel(q_ref, k_ref, v_ref, r_ref, gl_ref, wgu_ref, gb_ref, ng_ref, o_ref,
                     st_ref, a_ref, qd_ref, kd_ref, dec_ref, vt_ref, whi_ref, wlo_ref, *, nchunks):
    C, c = GLA_CHUNK, GLA_PAIR_ROWS
    nsub = C // c
    dk, dv = GLA_HEAD_K, GLA_HEAD_V
    heads = range(GLA_HEADS_PER_STEP)

    rows_i = lax.broadcasted_iota(jnp.int32, (C, C), 0)
    cols_i = lax.broadcasted_iota(jnp.int32, (C, C), 1)
    tri = (rows_i >= cols_i).astype(F32)
    sub_rows = lax.broadcasted_iota(jnp.int32, (c, C), 0)
    sub_cols = lax.broadcasted_iota(jnp.int32, (c, C), 1)
    nt = (((1,), (1,)), ((), ()))
    hi = lax.Precision.HIGHEST

    w = wgu_ref[...]
    whi_ref[...] = w.astype(BF16)
    wlo_ref[...] = (w - whi_ref[...].astype(F32)).astype(BF16)

    def prepare(pi, carry):
        NP = GLA_PAIRS_PER_STEP
        items = [(h, u) for u in range(2 * NP) for h in heads]
        rows = [pl.ds(pl.multiple_of((2 * NP * pi + u) * C, C), C) for h, u in items]
        q = [q_ref[rs, h * dk:(h + 1) * dk].astype(F32) * (dk ** -0.5) for (h, u), rs in zip(items, rows)]
        k = [k_ref[rs, h * dk:(h + 1) * dk].astype(F32) for (h, u), rs in zip(items, rows)]
        gl = [gl_ref[rs, :] for rs in rows]
        gl_hi = [t.astype(BF16) for t in gl]
        gl_lo = [(t - th.astype(F32)).astype(BF16) for t, th in zip(gl, gl_hi)]
        z = [(jnp.dot(th, whi_ref[:, h * dk:(h + 1) * dk], preferred_element_type=F32)
              + jnp.dot(tl, whi_ref[:, h * dk:(h + 1) * dk], preferred_element_type=F32)
              + jnp.dot(th, wlo_ref[:, h * dk:(h + 1) * dk], preferred_element_type=F32)
              + gb_ref[:, h * dk:(h + 1) * dk]) * LOG2E for (h, u), th, tl in zip(items, gl_hi, gl_lo)]
        g = [(jnp.minimum(zh, 0.0) - jnp.log2(1.0 + jnp.exp2(-jnp.abs(zh)))) * (1.0 / GLA_GATE_NORMALIZER) for zh in z]
        b = [jnp.dot(tri, gh, preferred_element_type=F32, precision=hi) for gh in g]
        b_last = [bh[C - 1:C, :] for bh in b]
        qd = [qh * jnp.exp2(bh) for qh, bh in zip(q, b)]
        kd = [kh * jnp.exp2(bl - bh) for kh, bh, bl in zip(k, b, b_last)]
        dec = [jnp.exp2(bl) for bl in b_last]

        n = len(items)
        subs = [(t, s) for s in range(nsub) for t in range(n)]
        a_s = {}
        for t, s in subs:
            lo = s * c
            if s == 0:
                a_s[t, s] = jnp.zeros((c, C), F32)
            else:
                b_ref = b[t][lo - 1:lo, :]
                q_ref_s = (q[t][lo:lo + c, :] * jnp.exp2(b[t][lo:lo + c, :] - b_ref)).astype(BF16)
                k_ref_s = (k[t][:lo, :] * jnp.exp2(b_ref - b[t][:lo, :])).astype(BF16)
                k_ref_s = jnp.concatenate([k_ref_s, jnp.zeros((C - lo, dk), BF16)], axis=0)
                a_s[t, s] = lax.dot_general(q_ref_s, k_ref_s, nt, preferred_element_type=F32)
        for j in range(c):
            for t, s in subs:
                lo = s * c
                bs = b[t][lo:lo + c, :]
                e = jnp.exp2(bs - bs[j:j + 1, :])
                col = jnp.sum(q[t][lo:lo + c, :] * k[t][lo + j:lo + j + 1, :] * e, axis=-1, keepdims=True)
                a_s[t, s] = jnp.where((sub_cols == lo + j) & (sub_rows >= j), col, a_s[t, s])
        a = [jnp.concatenate([a_s[t, s] for s in range(nsub)], axis=0) for t in range(n)]

        for h, pp in [(h, pp) for pp in range(NP) for h in heads]:
            t0, t1 = items.index((h, 2 * pp)), items.index((h, 2 * pp + 1))
            pr = NP * pi + pp
            pair_rows = pl.ds(pl.multiple_of(pr * 2 * C, 2 * C), 2 * C)
            qd0, qd1 = qd[t0].astype(BF16), qd[t1].astype(BF16)
            kd0, kd1 = kd[t0].astype(BF16), kd[t1].astype(BF16)
            cross = lax.dot_general(qd1, kd0, nt, preferred_element_type=F32)
            qd_ref[h, rows[t0], :] = qd0
            qd_ref[h, rows[t1], :] = (qd[t1] * dec[t0]).astype(BF16)
            kd_ref[h, rows[t0], :] = (kd[t0] * dec[t1]).astype(BF16)
            kd_ref[h, rows[t1], :] = kd1
            dec_ref[h, pl.ds(pr, 1), :] = dec[t0] * dec[t1]
            a_ref[h, rows[t0], :] = jnp.concatenate([a[t0], jnp.zeros((C, C), F32)], axis=1).astype(BF16)
            a_ref[h, rows[t1], :] = jnp.concatenate([cross, a[t1]], axis=1).astype(BF16)
            vt_ref[h, pr] = v_ref[pair_rows, h * dv:(h + 1) * dv].astype(F32).T.astype(BF16)
        return carry

    lax.fori_loop(0, nchunks // (2 * GLA_PAIRS_PER_STEP), prepare, 0)

    st_ref[...] = jnp.zeros_like(st_ref)
    ng = ng_ref[...]

    def recur(pi, carry):
        rs = pl.ds(pl.multiple_of(pi * 2 * C, 2 * C), 2 * C)
        st = [st_ref[h] for h in heads]
        o = [lax.dot_general(qd_ref[h, rs, :], st[h].astype(BF16), nt, preferred_element_type=F32) for h in heads]
        o = [o[h] + jnp.dot(a_ref[h, rs, :], v_ref[rs, h * dv:(h + 1) * dv], preferred_element_type=F32) for h in heads]
        upd = [jnp.dot(vt_ref[h, pi], kd_ref[h, rs, :], preferred_element_type=F32) for h in heads]
        for h in heads:
            st_ref[h] = st[h] * dec_ref[h, pl.ds(pi, 1), :] + upd[h]
        o = [oh * lax.rsqrt(jnp.mean(oh * oh, axis=-1, keepdims=True) + RMS_EPS) * ng for oh in o]
        r = [r_ref[rs, h * dv:(h + 1) * dv].astype(F32) for h in heads]
        for h in heads:
            o_ref[rs, h * dv:(h + 1) * dv] = (o[h] * (r[h] / (1.0 + jnp.exp2(r[h] * (-LOG2E))))).astype(o_ref.dtype)
        return carry

    lax.fori_loop(0, nchunks // 2, recur, 0)


def _gla_core(proj, g_low, w_gate_up, gate_bias, norm_g, batch, seq):
    dk, dv, hp = GLA_HEAD_K, GLA_HEAD_V, GLA_HEADS_PER_STEP
    nchunks = seq // GLA_CHUNK
    npairs = nchunks // 2
    q_blk = lambda off: pl.BlockSpec((None, seq, hp * dk), lambda b, h: (b, 0, off // (hp * dk) + h))
    v_blk = lambda off: pl.BlockSpec((None, seq, hp * dv), lambda b, h: (b, 0, off // (hp * dv) + h))
    return pl.pallas_call(
        functools.partial(_gla_core_kernel, nchunks=nchunks),
        out_shape=jax.ShapeDtypeStruct((batch, seq, GLA_DV), BF16),
        grid=(batch, GLA_HEADS // hp),
        in_specs=[q_blk(0), q_blk(GLA_DK), v_blk(2 * GLA_DK), v_blk(2 * GLA_DK + GLA_DV),
                  pl.BlockSpec((None, seq, LANES), lambda b, h: (b, 0, 0)),
                  pl.BlockSpec((LANES, hp * dk), lambda b, h: (0, h)),
                  pl.BlockSpec((1, hp * dk), lambda b, h: (0, h)),
                  pl.BlockSpec((1, dv), lambda b, h: (0, 0))],
        out_specs=pl.BlockSpec((None, seq, hp * dv), lambda b, h: (b, 0, h)),
        scratch_shapes=[pltpu.VMEM((hp, dv, dk), F32),
                        pltpu.VMEM((hp, seq, 2 * GLA_CHUNK), BF16),
                        pltpu.VMEM((hp, seq, dk), BF16),
                        pltpu.VMEM((hp, seq, dk), BF16),
                        pltpu.VMEM((hp, npairs, dk), F32),
                        pltpu.VMEM((hp, npairs, dv, 2 * GLA_CHUNK), BF16),
                        pltpu.VMEM((LANES, hp * dk), BF16),
                        pltpu.VMEM((LANES, hp * dk), BF16)],
        compiler_params=_params("parallel", "arbitrary"),
        name="gla_core",
    )(proj, proj, proj, proj, g_low, w_gate_up, gate_bias, norm_g)


LN_ROWS = 128


def _mm_res_ln_kernel(a_ref, w_ref, x_ref, g_ref, b_ref, of_ref, ob_ref):
    acc_ref = of_ref
    if len(a_ref.shape) == 3:
        a = jnp.concatenate([a_ref[h] for h in range(a_ref.shape[0])], axis=1)
    else:
        a = a_ref[...]
    acc_ref[...] = jnp.dot(a, w_ref[...], preferred_element_type=F32)
    gamma = g_ref[...]
    beta = b_ref[...]

    def rows(i, carry):
        r0 = pl.multiple_of(i * LN_ROWS, LN_ROWS)
        y = DEEPNORM_ALPHA * x_ref[pl.ds(r0, LN_ROWS), :] + acc_ref[pl.ds(r0, LN_ROWS), :]
        mu = jnp.mean(y, axis=-1, keepdims=True)
        yc = y - mu
        var = jnp.mean(yc * yc, axis=-1, keepdims=True)
        out = yc * lax.rsqrt(var + LN_EPS) * gamma + beta
        of_ref[pl.ds(r0, LN_ROWS), :] = out
        ob_ref[pl.ds(r0, LN_ROWS), :] = out.astype(BF16)
        return carry

    lax.fori_loop(0, acc_ref.shape[0] // LN_ROWS, rows, 0)


def _mm_res_ln(a, w, x, gamma, beta, *, tm=512):
    k, n = w.shape
    m = x.shape[0]
    if a.ndim == 4:
        _, heads, seq, dh = a.shape
        tps = seq // tm
        a_spec = pl.BlockSpec((None, heads, tm, dh), lambda i: (i // tps, 0, i % tps, 0))
    else:
        a_spec = pl.BlockSpec((tm, k), lambda i: (i, 0))
    return pl.pallas_call(
        _mm_res_ln_kernel,
        out_shape=(jax.ShapeDtypeStruct((m, n), F32), jax.ShapeDtypeStruct((m, n), BF16)),
        grid=(m // tm,),
        in_specs=[a_spec,
                  pl.BlockSpec((k, n), lambda i: (0, 0)),
                  pl.BlockSpec((tm, n), lambda i: (i, 0)),
                  pl.BlockSpec((1, n), lambda i: (0, 0)),
                  pl.BlockSpec((1, n), lambda i: (0, 0))],
        out_specs=(pl.BlockSpec((tm, n), lambda i: (i, 0)),
                   pl.BlockSpec((tm, n), lambda i: (i, 0))),
        compiler_params=_params("parallel"),
        name="mm_res_ln",
    )(a, w, x, gamma, beta)


FFN_TM = 1024
FFN_CHUNK = 256
FFN_PIECE = 32
HALO = 8


def _ffn_kernel(x_ref, wg_ref, wu_ref, cwg_ref, cwu_ref, cbg_ref, cbu_ref, wd_ref, xres_ref, g_ref, b_ref,
                of_ref, ob_ref, y0_ref, y1_ref, halo_ref, act_ref, *, nj, tiles_per_seq):
    i = pl.program_id(0)
    j = pl.program_id(1)
    tm = x_ref.shape[0]
    tf = wd_ref.shape[0]
    nchunk = tm // FFN_CHUNK

    def conv(h, prev, cw_ref, cb_ref):
        cw = cw_ref[...]
        win = jnp.concatenate([prev, h], axis=0)
        return (cw[2:3, :] * h + cw[1:2, :] * pltpu.roll(win, 1, axis=0)[HALO:, :]
                + cw[0:1, :] * pltpu.roll(win, 2, axis=0)[HALO:, :] + cb_ref[...])

    def up_steps(y_ref):
        @pl.when(i % tiles_per_seq == 0)
        def _():
            halo_ref[j] = jnp.zeros(halo_ref.shape[1:], F32)

        prev = [halo_ref[j, :, half * tf:(half + 1) * tf] for half in range(2)]
        for c in range(nchunk):
            rows = pl.ds(c * FFN_CHUNK, FFN_CHUNK)
            xs = x_ref[rows, :]
            for half, (w_ref, cw_ref, cb_ref) in enumerate(((wg_ref, cwg_ref, cbg_ref), (wu_ref, cwu_ref, cbu_ref))):
                h = jnp.dot(xs, w_ref[...], preferred_element_type=F32)
                y_ref[rows, half * tf:(half + 1) * tf] = conv(h, prev[half], cw_ref, cb_ref)
                prev[half] = h[FFN_CHUNK - HALO:, :]
            yield c
        for half in range(2):
            halo_ref[j, :, half * tf:(half + 1) * tf] = prev[half]

    def down_chunk(c, y_ref):
        for p in range(FFN_CHUNK // FFN_PIECE):
            rows = pl.ds(c * FFN_CHUNK + p * FFN_PIECE, FFN_PIECE)
            gate = y_ref[rows, :tf]
            act_ref[rows, :] = (gate / (1.0 + jnp.exp2(gate * (-LOG2E))) * y_ref[rows, tf:]).astype(BF16)
        rows = pl.ds(c * FFN_CHUNK, FFN_CHUNK)
        of_ref[rows, :] += jnp.dot(act_ref[rows, :], wd_ref[...], preferred_element_type=F32)

    def first_step(y_ref):
        for c in up_steps(y_ref):
            rows = pl.ds(c * FFN_CHUNK, FFN_CHUNK)
            of_ref[rows, :] = DEEPNORM_ALPHA * xres_ref[...] if c == 0 else jnp.zeros((FFN_CHUNK, of_ref.shape[1]), F32)

    def mid_step(y_up, y_prev):
        @pl.when(j < nchunk)
        def _():
            rows = pl.ds(pl.multiple_of(j * FFN_CHUNK, FFN_CHUNK), FFN_CHUNK)
            of_ref[rows, :] += DEEPNORM_ALPHA * xres_ref[...]

        for c in up_steps(y_up):
            down_chunk(c, y_prev)

    def last_step(y_prev):
        gamma = g_ref[...]
        beta = b_ref[...]
        for c in range(nchunk):
            down_chunk(c, y_prev)
            for p in range(FFN_CHUNK // FFN_PIECE):
                rows = pl.ds(c * FFN_CHUNK + p * FFN_PIECE, FFN_PIECE)
                y = of_ref[rows, :]
                mu = jnp.mean(y, axis=-1, keepdims=True)
                yc = y - mu
                var = jnp.mean(yc * yc, axis=-1, keepdims=True)
                out = yc * lax.rsqrt(var + LN_EPS) * gamma + beta
                of_ref[rows, :] = out
                ob_ref[rows, :] = out.astype(BF16)

    slots = (y0_ref, y1_ref)

    @pl.when(j == 0)
    def _():
        first_step(slots[0])

    for parity in (0, 1):
        @pl.when((j > 0) & (j < nj) & (j % 2 == parity))
        def _():
            mid_step(slots[parity], slots[1 - parity])

    @pl.when(j == nj)
    def _():
        last_step(slots[(nj - 1) % 2])


def _ffn(xb, xf, layer, wg, wu, cwg, cwu, cbg, cbu, wd, gamma, beta, seq):
    m, d = xb.shape
    tf = FF_TILE
    nj = wd.shape[1] // tf
    tm = min(FFN_TM, seq)
    nchunk = tm // FFN_CHUNK
    assert nj > nchunk
    one = pl.Buffered(1)
    up_j = lambda j: jnp.minimum(j, nj - 1)
    down_j = lambda j: jnp.maximum(j - 1, 0)
    up_blk = lambda rows: pl.BlockSpec((None, rows, tf), lambda i, j: (layer, 0, up_j(j)))
    return pl.pallas_call(
        functools.partial(_ffn_kernel, nj=nj, tiles_per_seq=seq // tm),
        out_shape=(jax.ShapeDtypeStruct((m, d), F32), jax.ShapeDtypeStruct((m, d), BF16)),
        grid=(m // tm, nj + 1),
        in_specs=[pl.BlockSpec((tm, d), lambda i, j: (i, 0)),
                  up_blk(d), up_blk(d), up_blk(CONV_WIDTH), up_blk(CONV_WIDTH), up_blk(1), up_blk(1),
                  pl.BlockSpec((None, tf, d), lambda i, j: (layer, down_j(j), 0)),
                  pl.BlockSpec((FFN_CHUNK, d), lambda i, j: (i * nchunk + jnp.minimum(j, nchunk - 1), 0)),
                  pl.BlockSpec((1, d), lambda i, j: (0, 0)),
                  pl.BlockSpec((1, d), lambda i, j: (0, 0))],
        out_specs=(pl.BlockSpec((tm, d), lambda i, j: (i, 0), pipeline_mode=one),
                   pl.BlockSpec((tm, d), lambda i, j: (i, 0))),
        scratch_shapes=[pltpu.VMEM((tm, 2 * tf), F32), pltpu.VMEM((tm, 2 * tf), F32),
                        pltpu.VMEM((nj, HALO, 2 * tf), F32), pltpu.VMEM((tm, tf), BF16)],
        compiler_params=_params("arbitrary", "arbitrary"),
        name="ffn",
    )(xb, wg, wu, cwg, cwu, cbg, cbu, wd, xf, gamma, beta)


DIL_PERMUTE_FROM = 16
DIL_PERMUTE_ROWS = 256


def _dil_proj_kernel(x_ref, w_ref, o_ref, *scratch, dil):
    acc = jnp.dot(x_ref[...], w_ref[...], preferred_element_type=F32)
    dh = DIL_HEAD_DIM
    if dil == 1:
        for h in range(DIL_HEADS):
            o_ref[h, 0] = acc[:, h * dh:(h + 1) * dh].astype(o_ref.dtype)
        return
    if dil < DIL_PERMUTE_FROM:
        (rows_ref,) = scratch
        n = x_ref.shape[0] // dil
        for h in range(DIL_HEADS):
            rows_ref[h] = acc[:, h * dh:(h + 1) * dh]
        for h in range(DIL_HEADS):
            for r in range(dil):
                o_ref[h, r] = rows_ref[h, pl.ds(r, n, stride=dil), :].astype(o_ref.dtype)
        return
    blk = DIL_PERMUTE_ROWS
    n = blk // dil
    dst = lax.broadcasted_iota(jnp.int32, (blk, blk), 0)
    src = lax.broadcasted_iota(jnp.int32, (blk, blk), 1)
    perm = jnp.where((dst // n == src % dil) & (dst % n == src // dil), 1.0, 0.0).astype(BF16)
    y = acc.astype(BF16)
    for q in range(x_ref.shape[0] // blk):
        p = jnp.dot(perm, y[q * blk:(q + 1) * blk, :], preferred_element_type=F32).astype(o_ref.dtype)
        for h in range(DIL_HEADS):
            for r in range(dil):
                o_ref[h, r, q * n:(q + 1) * n, :] = p[r * n:(r + 1) * n, h * dh:(h + 1) * dh]


def _dil_proj(xb, w, dil, batch, seq, *, tm=1024):
    m, k = xb.shape
    H, dh = DIL_HEADS, DIL_HEAD_DIM
    tm = min(tm, seq)
    tps = seq // tm
    scratch = [pltpu.VMEM((H, tm, dh), F32)] if 1 < dil < DIL_PERMUTE_FROM else []
    return pl.pallas_call(
        functools.partial(_dil_proj_kernel, dil=dil),
        out_shape=jax.ShapeDtypeStruct((batch, 3, H, dil, seq // dil, dh), BF16),
        grid=(m // tm, 3),
        in_specs=[pl.BlockSpec((tm, k), lambda i, c: (i, 0)),
                  pl.BlockSpec((k, H * dh), lambda i, c: (0, c))],
        out_specs=pl.BlockSpec((None, None, H, dil, tm // dil, dh), lambda i, c: (i // tps, c, 0, 0, i % tps, 0)),
        scratch_shapes=scratch,
        compiler_params=_params("parallel", "arbitrary"),
        name=f"dil_proj_d{dil}",
    )(xb, w)


NEG = -1e30
EXP2_SCALE = (DIL_HEAD_DIM ** -0.5) * LOG2E
ATTN_ROWS = 256


def _dil_attn_kernel(*refs, seq):
    qkv = refs[:3 * DIL_GROUPS]
    o_ref = refs[3 * DIL_GROUPS]
    m_ref, l_ref, acc_ref = refs[3 * DIL_GROUPS + 1:]
    P = DIL_BLOCK
    dh = DIL_HEAD_DIM
    nt = (((1,), (1,)), ((), ()))
    qi = lax.broadcasted_iota(jnp.int32, (P, P), 0)
    kj = lax.broadcasted_iota(jnp.int32, (P, P), 1)
    for gi, (window, dil) in enumerate(DIL_PATTERNS):
        q_ref, k_ref, v_ref = qkv[3 * gi:3 * gi + 3]
        steps = window // dil
        bias_cur = jnp.where((qi >= kj) & (qi - kj <= steps), 0.0, NEG)
        bias_prev = jnp.where(qi + P - kj <= steps, 0.0, NEG)
        blocks = [(r, lb * P) for r in range(dil) for lb in range(seq // dil // P)]
        bias = {P: bias_cur, 2 * P: jnp.concatenate([bias_prev, bias_cur], axis=1)}
        spans = [(r, lo - P if lo > 0 else lo, lo + P) for r, lo in blocks]
        s = [lax.dot_general(q_ref[r, lo:lo + P, :], k_ref[r, k0:k1, :], nt, preferred_element_type=F32)
             + bias[k1 - k0] for (r, lo), (_, k0, k1) in zip(blocks, spans)]
        m = [jnp.max(sb, axis=-1, keepdims=True) for sb in s]
        p = [jnp.exp2((sb - mb) * EXP2_SCALE).astype(BF16) for sb, mb in zip(s, m)]
        acc = [jnp.dot(pb, jnp.concatenate([v_ref[r, k0:k1, :], jnp.ones((k1 - k0, LANES), BF16)], axis=1),
                       preferred_element_type=F32) for pb, (r, k0, k1) in zip(p, spans)]
        for (r, lo), mb, ab in zip(blocks, m, acc):
            tok = pl.ds(lo * dil + r, P, stride=dil) if dil > 1 else pl.ds(lo, P)
            m_ref[gi, tok, :] = jnp.broadcast_to(mb, (P, LANES))
            l_ref[gi, tok, :] = ab[:, dh:]
            acc_ref[gi, tok, :] = ab[:, :dh]
    for c in range(seq // ATTN_ROWS):
        rows = pl.ds(c * ATTN_ROWS, ATTN_ROWS)
        ms = [m_ref[gi, rows, :] for gi in range(DIL_GROUPS)]
        m_all = functools.reduce(jnp.maximum, ms)
        ws = [jnp.exp2((mg - m_all) * EXP2_SCALE) for mg in ms]
        l_all = sum(w * l_ref[gi, rows, :] for gi, w in enumerate(ws))
        acc_all = sum(w * acc_ref[gi, rows, :] for gi, w in enumerate(ws))
        o_ref[rows, :] = (acc_all / l_all).astype(o_ref.dtype)


def _dil_attn(groups, batch, seq):
    dh = DIL_HEAD_DIM
    in_specs, args = [], []
    for (window, dil), t in zip(DIL_PATTERNS, groups):
        for c in range(3):
            in_specs.append(pl.BlockSpec((None, None, None, dil, seq // dil, dh),
                                         lambda b, h, c=c: (b, c, h, 0, 0, 0)))
            args.append(t)
    return pl.pallas_call(
        functools.partial(_dil_attn_kernel, seq=seq),
        out_shape=jax.ShapeDtypeStruct((batch, DIL_HEADS, seq, dh), BF16),
        grid=(batch, DIL_HEADS),
        in_specs=in_specs,
        out_specs=pl.BlockSpec((None, None, seq, dh), lambda b, h: (b, h, 0, 0)),
        scratch_shapes=[pltpu.VMEM((DIL_GROUPS, seq, LANES), F32), pltpu.VMEM((DIL_GROUPS, seq, LANES), F32),
                        pltpu.VMEM((DIL_GROUPS, seq, dh), F32)],
        compiler_params=_params("parallel", "parallel"),
        name="dil_attn",
    )(*args)


def _pad_cols(w, n):
    return jnp.pad(w, ((0, 0), (0, n - w.shape[1])))


def _gla_layer(xf, xb, w_in, w_gate_up, gate_bias, norm_g, w_out, gamma, beta, batch, seq):
    w_main = w_in[:, :GLA_MAIN].astype(BF16)
    w_gate = _pad_cols(w_in[:, GLA_MAIN:], LANES).astype(BF16)
    proj, g_low = _gla_proj(xf if xb is None else xb, w_main, w_gate)
    wgu = jnp.pad(w_gate_up, ((0, LANES - GLA_GATE_RANK), (0, 0)))
    o = _gla_core(proj.reshape(batch, seq, GLA_MAIN), g_low.reshape(batch, seq, LANES),
                  wgu, gate_bias[None, :], norm_g[None, :], batch, seq)
    return _mm_res_ln(o.reshape(batch * seq, GLA_DV), w_out.astype(BF16), xf, gamma[None, :], beta[None, :])


def _dil_layer(xf, xb, w_in, w_out, gamma, beta, batch, seq):
    gw = 3 * DIL_WIDTH
    w_in = w_in.astype(BF16)
    groups = [_dil_proj(xb, w_in[:, gi * gw:(gi + 1) * gw], dil, batch, seq)
              for gi, (window, dil) in enumerate(DIL_PATTERNS)]
    o = _dil_attn(groups, batch, seq)
    return _mm_res_ln(o, w_out.astype(BF16), xf, gamma[None, :], beta[None, :])


def _ffn_weights(w_up, conv_w, conv_b, w_down):
    pad = D_FF_PAD - D_FF
    halves = lambda t: [jnp.pad(h, ((0, 0), (0, 0), (0, pad))) for h in (t[..., :D_FF], t[..., D_FF:])]
    wg, wu = (h.astype(BF16) for h in halves(w_up))
    cwg, cwu = halves(conv_w)
    cbg, cbu = halves(conv_b[:, None, :])
    wd = jnp.pad(w_down, ((0, 0), (0, pad), (0, 0))).astype(BF16)
    return wg, wu, cwg, cwu, cbg, cbu, wd


def kernel(x, gla_w_in, gla_w_gate_up, gla_gate_bias, gla_norm_g, gla_w_out, dil_w_in, dil_w_out, ffn_w_up,
           ffn_conv_w, ffn_conv_b, ffn_w_down, ln_g, ln_b):
    batch, seq, d = x.shape
    xf = x.reshape(batch * seq, d)
    xb = None
    ffn_weights = _ffn_weights(ffn_w_up, ffn_conv_w, ffn_conv_b, ffn_w_down)
    for i in range(DEPTH):
        j = i // 2
        if i % 2 == 0:
            xf, xb = _gla_layer(xf, xb, gla_w_in[j], gla_w_gate_up[j], gla_gate_bias[j], gla_norm_g[j],
                                gla_w_out[j], ln_g[i, 0], ln_b[i, 0], batch, seq)
        else:
            xf, xb = _dil_layer(xf, xb, dil_w_in[j], dil_w_out[j], ln_g[i, 0], ln_b[i, 0], batch, seq)
        xf, xb = _ffn(xb, xf, i, *ffn_weights, ln_g[i, 1][None, :], ln_b[i, 1][None, :], seq)
    return xf.reshape(batch, seq, d)
```

```python
import functools

import jax
import jax.numpy as jnp
from jax import lax
from jax.experimental import pallas as pl
from jax.experimental.pallas import tpu as pltpu

F32 = jnp.float32
BF16 = jnp.bfloat16

D_MODEL = 2048
DEPTH = 4

GLA_HEADS = 4
GLA_DK = D_MODEL // 2
GLA_DV = D_MODEL
GLA_HEAD_K = GLA_DK // GLA_HEADS
GLA_HEAD_V = GLA_DV // GLA_HEADS
GLA_GATE_RANK = 16
GLA_GATE_NORMALIZER = 16.0
GLA_CHUNK = 64
GLA_MAIN = 2 * GLA_DK + 2 * GLA_DV

DIL_PATTERNS = ((128, 1), (512, 4), (2048, 16))
DIL_GROUPS = len(DIL_PATTERNS)
DIL_HEADS = 8
DIL_HEAD_DIM = 128
DIL_WIDTH = DIL_HEADS * DIL_HEAD_DIM
DIL_BLOCK = 128

D_FF = 5504
CONV_WIDTH = 3

DEEPNORM_ALPHA = (2 * DEPTH) ** 0.25
LN_EPS = 1e-5
RMS_EPS = 1e-6

LANES = 128
LOG2E = 1.4426950408889634
FF_TILE = 512
D_FF_PAD = -(-D_FF // FF_TILE) * FF_TILE
VMEM_LIMIT = 56 * 1024 * 1024


def _params(*sem):
    return pltpu.CompilerParams(dimension_semantics=sem, vmem_limit_bytes=VMEM_LIMIT)


def _gla_proj_kernel(x_ref, w_ref, wg_ref, o_ref, g_ref, *scratch):
    if scratch:
        (xb_ref,) = scratch

        @pl.when(pl.program_id(1) == 0)
        def _():
            xb_ref[...] = x_ref[...].astype(BF16)
    else:
        xb_ref = x_ref
    x = xb_ref[...]
    o_ref[...] = jnp.dot(x, w_ref[...], preferred_element_type=F32).astype(o_ref.dtype)

    @pl.when(pl.program_id(1) == 0)
    def _():
        g_ref[...] = jnp.dot(x, wg_ref[...], preferred_element_type=F32)


def _gla_proj(x, w_main, w_gate, *, tm=1024, tn=1024):
    m, k = x.shape
    n = w_main.shape[1]
    tm = min(tm, m)
    scratch = [] if x.dtype == BF16 else [pltpu.VMEM((tm, k), BF16)]
    return pl.pallas_call(
        _gla_proj_kernel,
        out_shape=(jax.ShapeDtypeStruct((m, n), BF16), jax.ShapeDtypeStruct((m, LANES), F32)),
        grid=(m // tm, n // tn),
        in_specs=[pl.BlockSpec((tm, k), lambda i, j: (i, 0)),
                  pl.BlockSpec((k, tn), lambda i, j: (0, j)),
                  pl.BlockSpec((k, LANES), lambda i, j: (0, 0))],
        out_specs=(pl.BlockSpec((tm, tn), lambda i, j: (i, j)),
                   pl.BlockSpec((tm, LANES), lambda i, j: (i, 0))),
        scratch_shapes=scratch,
        compiler_params=_params("parallel", "arbitrary"),
        name="gla_proj",
    )(x, w_main, w_gate)


GLA_HEADS_PER_STEP = 2
GLA_PAIRS_PER_STEP = 2
GLA_PAIR_ROWS = 8


def _gla_core_kernel(q_ref, k_ref, v_ref, r_ref, gl_ref, wgu_ref, gb_ref, ng_ref, o_ref,
                     st_ref, a_ref, qd_ref, kd_ref, dec_ref, vt_ref, whi_ref, wlo_ref, *, nchunks):
    C, c = GLA_CHUNK, GLA_PAIR_ROWS
    nsub = C // c
    dk, dv = GLA_HEAD_K, GLA_HEAD_V
    heads = range(GLA_HEADS_PER_STEP)

    rows_i = lax.broadcasted_iota(jnp.int32, (C, C), 0)
    cols_i = lax.broadcasted_iota(jnp.int32, (C, C), 1)
    tri = (rows_i >= cols_i).astype(F32)
    sub_rows = lax.broadcasted_iota(jnp.int32, (c, C), 0)
    sub_cols = lax.broadcasted_iota(jnp.int32, (c, C), 1)
    nt = (((1,), (1,)), ((), ()))
    hi = lax.Precision.HIGHEST

    w = wgu_ref[...]
    whi_ref[...] = w.astype(BF16)
    wlo_ref[...] = (w - whi_ref[...].astype(F32)).astype(BF16)

    def prepare(pi, carry):
        NP = GLA_PAIRS_PER_STEP
        items = [(h, u) for u in range(2 * NP) for h in heads]
        rows = [pl.ds(pl.multiple_of((2 * NP * pi + u) * C, C), C) for h, u in items]
        q = [q_ref[rs, h * dk:(h + 1) * dk].astype(F32) * (dk ** -0.5) for (h, u), rs in zip(items, rows)]
        k = [k_ref[rs, h * dk:(h + 1) * dk].astype(F32) for (h, u), rs in zip(items, rows)]
        gl = [gl_ref[rs, :] for rs in rows]
        gl_hi = [t.astype(BF16) for t in gl]
        gl_lo = [(t - th.astype(F32)).astype(BF16) for t, th in zip(gl, gl_hi)]
        z = [(jnp.dot(th, whi_ref[:, h * dk:(h + 1) * dk], preferred_element_type=F32)
              + jnp.dot(tl, whi_ref[:, h * dk:(h + 1) * dk], preferred_element_type=F32)
              + jnp.dot(th, wlo_ref[:, h * dk:(h + 1) * dk], preferred_element_type=F32)
              + gb_ref[:, h * dk:(h + 1) * dk]) * LOG2E for (h, u), th, tl in zip(items, gl_hi, gl_lo)]
        g = [(jnp.minimum(zh, 0.0) - jnp.log2(1.0 + jnp.exp2(-jnp.abs(zh)))) * (1.0 / GLA_GATE_NORMALIZER) for zh in z]
        b = [jnp.dot(tri, gh, preferred_element_type=F32, precision=hi) for gh in g]
        b_last = [bh[C - 1:C, :] for bh in b]
        qd = [qh * jnp.exp2(bh) for qh, bh in zip(q, b)]
        kd = [kh * jnp.exp2(bl - bh) for kh, bh, bl in zip(k, b, b_last)]
        dec = [jnp.exp2(bl) for bl in b_last]

        n = len(items)
        subs = [(t, s) for s in range(nsub) for t in range(n)]
        a_s = {}
        for t, s in subs:
            lo = s * c
            if s == 0:
                a_s[t, s] = jnp.zeros((c, C), F32)
            else:
                b_ref = b[t][lo - 1:lo, :]
                q_ref_s = (q[t][lo:lo + c, :] * jnp.exp2(b[t][lo:lo + c, :] - b_ref)).astype(BF16)
                k_ref_s = (k[t][:lo, :] * jnp.exp2(b_ref - b[t][:lo, :])).astype(BF16)
                k_ref_s = jnp.concatenate([k_ref_s, jnp.zeros((C - lo, dk), BF16)], axis=0)
                a_s[t, s] = lax.dot_general(q_ref_s, k_ref_s, nt, preferred_element_type=F32)
        for j in range(c):
            for t, s in subs:
                lo = s * c
                bs = b[t][lo:lo + c, :]
                e = jnp.exp2(bs - bs[j:j + 1, :])
                col = jnp.sum(q[t][lo:lo + c, :] * k[t][lo + j:lo + j + 1, :] * e, axis=-1, keepdims=True)
                a_s[t, s] = jnp.where((sub_cols == lo + j) & (sub_rows >= j), col, a_s[t, s])
        a = [jnp.concatenate([a_s[t, s] for s in range(nsub)], axis=0) for t in range(n)]

        for h, pp in [(h, pp) for pp in range(NP) for h in heads]:
            t0, t1 = items.index((h, 2 * pp)), items.index((h, 2 * pp + 1))
            pr = NP * pi + pp
            pair_rows = pl.ds(pl.multiple_of(pr * 2 * C, 2 * C), 2 * C)
            qd0, qd1 = qd[t0].astype(BF16), qd[t1].astype(BF16)
            kd0, kd1 = kd[t0].astype(BF16), kd[t1].astype(BF16)
            cross = lax.dot_general(qd1, kd0, nt, preferred_element_type=F32)
            qd_ref[h, rows[t0], :] = qd0
            qd_ref[h, rows[t1], :] = (qd[t1] * dec[t0]).astype(BF16)
            kd_ref[h, rows[t0], :] = (kd[t0] * dec[t1]).astype(BF16)
            kd_ref[h, rows[t1], :] = kd1
            dec_ref[h, pl.ds(pr, 1), :] = dec[t0] * dec[t1]
            a_ref[h, rows[t0], :] = jnp.concatenate([a[t0], jnp.zeros((C, C), F32)], axis=1).astype(BF16)
            a_ref[h, rows[t1], :] = jnp.concatenate([cross, a[t1]], axis=1).astype(BF16)
            vt_ref[h, pr] = v_ref[pair_rows, h * dv:(h + 1) * dv].astype(F32).T.astype(BF16)
        return carry

    lax.fori_loop(0, nchunks // (2 * GLA_PAIRS_PER_STEP), prepare, 0)

    st_ref[...] = jnp.zeros_like(st_ref)
    ng = ng_ref[...]

    def recur(pi, carry):
        rs = pl.ds(pl.multiple_of(pi * 2 * C, 2 * C), 2 * C)
        st = [st_ref[h] for h in heads]
        o = [lax.dot_general(qd_ref[h, rs, :], st[h].astype(BF16), nt, preferred_element_type=F32) for h in heads]
        o = [o[h] + jnp.dot(a_ref[h, rs, :], v_ref[rs, h * dv:(h + 1) * dv], preferred_element_type=F32) for h in heads]
        upd = [jnp.dot(vt_ref[h, pi], kd_ref[h, rs, :], preferred_element_type=F32) for h in heads]
        for h in heads:
            st_ref[h] = st[h] * dec_ref[h, pl.ds(pi, 1), :] + upd[h]
        o = [oh * lax.rsqrt(jnp.mean(oh * oh, axis=-1, keepdims=True) + RMS_EPS) * ng for oh in o]
        r = [r_ref[rs, h * dv:(h + 1) * dv].astype(F32) for h in heads]
        for h in heads:
            o_ref[rs, h * dv:(h + 1) * dv] = (o[h] * (r[h] / (1.0 + jnp.exp2(r[h] * (-LOG2E))))).astype(o_ref.dtype)
        return carry

    lax.fori_loop(0, nchunks // 2, recur, 0)


def _gla_core(proj, g_low, w_gate_up, gate_bias, norm_g, batch, seq):
    dk, dv, hp = GLA_HEAD_K, GLA_HEAD_V, GLA_HEADS_PER_STEP
    nchunks = seq // GLA_CHUNK
    npairs = nchunks // 2
    q_blk = lambda off: pl.BlockSpec((None, seq, hp * dk), lambda b, h: (b, 0, off // (hp * dk) + h))
    v_blk = lambda off: pl.BlockSpec((None, seq, hp * dv), lambda b, h: (b, 0, off // (hp * dv) + h))
    return pl.pallas_call(
        functools.partial(_gla_core_kernel, nchunks=nchunks),
        out_shape=jax.ShapeDtypeStruct((batch, seq, GLA_DV), BF16),
        grid=(batch, GLA_HEADS // hp),
        in_specs=[q_blk(0), q_blk(GLA_DK), v_blk(2 * GLA_DK), v_blk(2 * GLA_DK + GLA_DV),
                  pl.BlockSpec((None, seq, LANES), lambda b, h: (b, 0, 0)),
                  pl.BlockSpec((LANES, hp * dk), lambda b, h: (0, h)),
                  pl.BlockSpec((1, hp * dk), lambda b, h: (0, h)),
                  pl.BlockSpec((1, dv), lambda b, h: (0, 0))],
        out_specs=pl.BlockSpec((None, seq, hp * dv), lambda b, h: (b, 0, h)),
        scratch_shapes=[pltpu.VMEM((hp, dv, dk), F32),
                        pltpu.VMEM((hp, seq, 2 * GLA_CHUNK), BF16),
                        pltpu.VMEM((hp, seq, dk), BF16),
                        pltpu.VMEM((hp, seq, dk), BF16),
                        pltpu.VMEM((hp, npairs, dk), F32),
                        pltpu.VMEM((hp, npairs, dv, 2 * GLA_CHUNK), BF16),
                        pltpu.VMEM((LANES, hp * dk), BF16),
                        pltpu.VMEM((LANES, hp * dk), BF16)],
        compiler_params=_params("parallel", "arbitrary"),
        name="gla_core",
    )(proj, proj, proj, proj, g_low, w_gate_up, gate_bias, norm_g)


LN_ROWS = 128


def _mm_res_ln_kernel(a_ref, w_ref, x_ref, g_ref, b_ref, of_ref, ob_ref):
    acc_ref = of_ref
    if len(a_ref.shape) == 3:
        a = jnp.concatenate([a_ref[h] for h in range(a_ref.shape[0])], axis=1)
    else:
        a = a_ref[...]
    acc_ref[...] = jnp.dot(a, w_ref[...], preferred_element_type=F32)
    gamma = g_ref[...]
    beta = b_ref[...]

    def rows(i, carry):
        r0 = pl.multiple_of(i * LN_ROWS, LN_ROWS)
        y = DEEPNORM_ALPHA * x_ref[pl.ds(r0, LN_ROWS), :] + acc_ref[pl.ds(r0, LN_ROWS), :]
        mu = jnp.mean(y, axis=-1, keepdims=True)
        yc = y - mu
        var = jnp.mean(yc * yc, axis=-1, keepdims=True)
        out = yc * lax.rsqrt(var + LN_EPS) * gamma + beta
        of_ref[pl.ds(r0, LN_ROWS), :] = out
        ob_ref[pl.ds(r0, LN_ROWS), :] = out.astype(BF16)
        return carry

    lax.fori_loop(0, acc_ref.shape[0] // LN_ROWS, rows, 0)


def _mm_res_ln(a, w, x, gamma, beta, *, tm=512):
    k, n = w.shape
    m = x.shape[0]
    if a.ndim == 4:
        _, heads, seq, dh = a.shape
        tps = seq // tm
        a_spec = pl.BlockSpec((None, heads, tm, dh), lambda i: (i // tps, 0, i % tps, 0))
    else:
        a_spec = pl.BlockSpec((tm, k), lambda i: (i, 0))
    return pl.pallas_call(
        _mm_res_ln_kernel,
        out_shape=(jax.ShapeDtypeStruct((m, n), F32), jax.ShapeDtypeStruct((m, n), BF16)),
        grid=(m // tm,),
        in_specs=[a_spec,
                  pl.BlockSpec((k, n), lambda i: (0, 0)),
                  pl.BlockSpec((tm, n), lambda i: (i, 0)),
                  pl.BlockSpec((1, n), lambda i: (0, 0)),
                  pl.BlockSpec((1, n), lambda i: (0, 0))],
        out_specs=(pl.BlockSpec((tm, n), lambda i: (i, 0)),
                   pl.BlockSpec((tm, n), lambda i: (i, 0))),
        compiler_params=_params("parallel"),
        name="mm_res_ln",
    )(a, w, x, gamma, beta)


FFN_TM = 1024
FFN_CHUNK = 128
FFN_PIECE = 32
HALO = 8


def _ffn_kernel(x_ref, wg_ref, wu_ref, cwg_ref, cwu_ref, cbg_ref, cbu_ref, wd_ref, xres_ref, g_ref, b_ref,
                of_ref, ob_ref, y0_ref, y1_ref, halo_ref, act_ref, *, nj, tiles_per_seq):
    i = pl.program_id(0)
    j = pl.program_id(1)
    tm = x_ref.shape[0]
    tf = wd_ref.shape[0]
    nchunk = tm // FFN_CHUNK

    def conv(h, prev, cw_ref, cb_ref):
        cw = cw_ref[...]
        win = jnp.concatenate([prev, h], axis=0)
        return (cw[2:3, :] * h + cw[1:2, :] * pltpu.roll(win, 1, axis=0)[HALO:, :]
                + cw[0:1, :] * pltpu.roll(win, 2, axis=0)[HALO:, :] + cb_ref[...])

    def up_steps(y_ref):
        @pl.when(i % tiles_per_seq == 0)
        def _():
            halo_ref[j] = jnp.zeros(halo_ref.shape[1:], F32)

        prev = [halo_ref[j, :, half * tf:(half + 1) * tf] for half in range(2)]
        for c in range(nchunk):
            rows = pl.ds(c * FFN_CHUNK, FFN_CHUNK)
            xs = x_ref[rows, :]
            for half, (w_ref, cw_ref, cb_ref) in enumerate(((wg_ref, cwg_ref, cbg_ref), (wu_ref, cwu_ref, cbu_ref))):
                h = jnp.dot(xs, w_ref[...], preferred_element_type=F32)
                y_ref[rows, half * tf:(half + 1) * tf] = conv(h, prev[half], cw_ref, cb_ref)
                prev[half] = h[FFN_CHUNK - HALO:, :]
            yield c
        for half in range(2):
            halo_ref[j, :, half * tf:(half + 1) * tf] = prev[half]

    def down_chunk(c, y_ref):
        for p in range(FFN_CHUNK // FFN_PIECE):
            rows = pl.ds(c * FFN_CHUNK + p * FFN_PIECE, FFN_PIECE)
            gate = y_ref[rows, :tf]
            act_ref[rows, :] = (gate / (1.0 + jnp.exp2(gate * (-LOG2E))) * y_ref[rows, tf:]).astype(BF16)
        rows = pl.ds(c * FFN_CHUNK, FFN_CHUNK)
        of_ref[rows, :] += jnp.dot(act_ref[rows, :], wd_ref[...], preferred_element_type=F32)

    def first_step(y_ref):
        for c in up_steps(y_ref):
            rows = pl.ds(c * FFN_CHUNK, FFN_CHUNK)
            of_ref[rows, :] = DEEPNORM_ALPHA * xres_ref[...] if c == 0 else jnp.zeros((FFN_CHUNK, of_ref.shape[1]), F32)

    def mid_step(y_up, y_prev):
        @pl.when(j < nchunk)
        def _():
            rows = pl.ds(pl.multiple_of(j * FFN_CHUNK, FFN_CHUNK), FFN_CHUNK)
            of_ref[rows, :] += DEEPNORM_ALPHA * xres_ref[...]

        for c in up_steps(y_up):
            down_chunk(c, y_prev)

    def last_step(y_prev):
        gamma = g_ref[...]
        beta = b_ref[...]
        for c in range(nchunk):
            down_chunk(c, y_prev)
            for p in range(FFN_CHUNK // FFN_PIECE):
                rows = pl.ds(c * FFN_CHUNK + p * FFN_PIECE, FFN_PIECE)
                y = of_ref[rows, :]
                mu = jnp.mean(y, axis=-1, keepdims=True)
                yc = y - mu
                var = jnp.mean(yc * yc, axis=-1, keepdims=True)
                out = yc * lax.rsqrt(var + LN_EPS) * gamma + beta
                of_ref[rows, :] = out
                ob_ref[rows, :] = out.astype(BF16)

    slots = (y0_ref, y1_ref)

    @pl.when(j == 0)
    def _():
        first_step(slots[0])

    for parity in (0, 1):
        @pl.when((j > 0) & (j < nj) & (j % 2 == parity))
        def _():
            mid_step(slots[parity], slots[1 - parity])

    @pl.when(j == nj)
    def _():
        last_step(slots[(nj - 1) % 2])


def _ffn(xb, xf, layer, wg, wu, cwg, cwu, cbg, cbu, wd, gamma, beta, seq):
    m, d = xb.shape
    tf = FF_TILE
    nj = wd.shape[1] // tf
    tm = min(FFN_TM, seq)
    nchunk = tm // FFN_CHUNK
    assert nj > nchunk
    one = pl.Buffered(1)
    up_j = lambda j: jnp.minimum(j, nj - 1)
    down_j = lambda j: jnp.maximum(j - 1, 0)
    up_blk = lambda rows: pl.BlockSpec((None, rows, tf), lambda i, j: (layer, 0, up_j(j)))
    return pl.pallas_call(
        functools.partial(_ffn_kernel, nj=nj, tiles_per_seq=seq // tm),
        out_shape=(jax.ShapeDtypeStruct((m, d), F32), jax.ShapeDtypeStruct((m, d), BF16)),
        grid=(m // tm, nj + 1),
        in_specs=[pl.BlockSpec((tm, d), lambda i, j: (i, 0)),
                  up_blk(d), up_blk(d), up_blk(CONV_WIDTH), up_blk(CONV_WIDTH), up_blk(1), up_blk(1),
                  pl.BlockSpec((None, tf, d), lambda i, j: (layer, down_j(j), 0)),
                  pl.BlockSpec((FFN_CHUNK, d), lambda i, j: (i * nchunk + jnp.minimum(j, nchunk - 1), 0)),
                  pl.BlockSpec((1, d), lambda i, j: (0, 0)),
                  pl.BlockSpec((1, d), lambda i, j: (0, 0))],
        out_specs=(pl.BlockSpec((tm, d), lambda i, j: (i, 0), pipeline_mode=one),
                   pl.BlockSpec((tm, d), lambda i, j: (i, 0))),
        scratch_shapes=[pltpu.VMEM((tm, 2 * tf), F32), pltpu.VMEM((tm, 2 * tf), F32),
                        pltpu.VMEM((nj, HALO, 2 * tf), F32), pltpu.VMEM((tm, tf), BF16)],
        compiler_params=_params("arbitrary", "arbitrary"),
        name="ffn",
    )(xb, wg, wu, cwg, cwu, cbg, cbu, wd, xf, gamma, beta)


DIL_PERMUTE_FROM = 16
DIL_PERMUTE_ROWS = 256


def _dil_proj_kernel(x_ref, w_ref, o_ref, *scratch, dil):
    acc = jnp.dot(x_ref[...], w_ref[...], preferred_element_type=F32)
    dh = DIL_HEAD_DIM
    if dil == 1:
        for h in range(DIL_HEADS):
            o_ref[h, 0] = acc[:, h * dh:(h + 1) * dh].astype(o_ref.dtype)
        return
    if dil < DIL_PERMUTE_FROM:
        (rows_ref,) = scratch
        n = x_ref.shape[0] // dil
        for h in range(DIL_HEADS):
            rows_ref[h] = acc[:, h * dh:(h + 1) * dh]
        for h in range(DIL_HEADS):
            for r in range(dil):
                o_ref[h, r] = rows_ref[h, pl.ds(r, n, stride=dil), :].astype(o_ref.dtype)
        return
    blk = DIL_PERMUTE_ROWS
    n = blk // dil
    dst = lax.broadcasted_iota(jnp.int32, (blk, blk), 0)
    src = lax.broadcasted_iota(jnp.int32, (blk, blk), 1)
    perm = jnp.where((dst // n == src % dil) & (dst % n == src // dil), 1.0, 0.0).astype(BF16)
    y = acc.astype(BF16)
    for q in range(x_ref.shape[0] // blk):
        p = jnp.dot(perm, y[q * blk:(q + 1) * blk, :], preferred_element_type=F32).astype(o_ref.dtype)
        for h in range(DIL_HEADS):
            for r in range(dil):
                o_ref[h, r, q * n:(q + 1) * n, :] = p[r * n:(r + 1) * n, h * dh:(h + 1) * dh]


def _dil_proj(xb, w, dil, batch, seq, *, tm=1024):
    m, k = xb.shape
    H, dh = DIL_HEADS, DIL_HEAD_DIM
    tm = min(tm, seq)
    tps = seq // tm
    scratch = [pltpu.VMEM((H, tm, dh), F32)] if 1 < dil < DIL_PERMUTE_FROM else []
    return pl.pallas_call(
        functools.partial(_dil_proj_kernel, dil=dil),
        out_shape=jax.ShapeDtypeStruct((batch, 3, H, dil, seq // dil, dh), BF16),
        grid=(m // tm, 3),
        in_specs=[pl.BlockSpec((tm, k), lambda i, c: (i, 0)),
                  pl.BlockSpec((k, H * dh), lambda i, c: (0, c))],
        out_specs=pl.BlockSpec((None, None, H, dil, tm // dil, dh), lambda i, c: (i // tps, c, 0, 0, i % tps, 0)),
        scratch_shapes=scratch,
        compiler_params=_params("parallel", "arbitrary"),
        name=f"dil_proj_d{dil}",
    )(xb, w)


NEG = -1e30
EXP2_SCALE = (DIL_HEAD_DIM ** -0.5) * LOG2E
ATTN_ROWS = 256


def _dil_attn_kernel(*refs, seq):
    qkv = refs[:3 * DIL_GROUPS]
    o_ref = refs[3 * DIL_GROUPS]
    m_ref, l_ref, acc_ref = refs[3 * DIL_GROUPS + 1:]
    P = DIL_BLOCK
    dh = DIL_HEAD_DIM
    nt = (((1,), (1,)), ((), ()))
    qi = lax.broadcasted_iota(jnp.int32, (P, P), 0)
    kj = lax.broadcasted_iota(jnp.int32, (P, P), 1)
    for gi, (window, dil) in enumerate(DIL_PATTERNS):
        q_ref, k_ref, v_ref = qkv[3 * gi:3 * gi + 3]
        steps = window // dil
        bias_cur = jnp.where((qi >= kj) & (qi - kj <= steps), 0.0, NEG)
        bias_prev = jnp.where(qi + P - kj <= steps, 0.0, NEG)
        blocks = [(r, lb * P) for r in range(dil) for lb in range(seq // dil // P)]
        bias = {P: bias_cur, 2 * P: jnp.concatenate([bias_prev, bias_cur], axis=1)}
        spans = [(r, lo - P if lo > 0 else lo, lo + P) for r, lo in blocks]
        s = [lax.dot_general(q_ref[r, lo:lo + P, :], k_ref[r, k0:k1, :], nt, preferred_element_type=F32)
             + bias[k1 - k0] for (r, lo), (_, k0, k1) in zip(blocks, spans)]
        m = [jnp.max(sb, axis=-1, keepdims=True) for sb in s]
        p = [jnp.exp2((sb - mb) * EXP2_SCALE).astype(BF16) for sb, mb in zip(s, m)]
        acc = [jnp.dot(pb, jnp.concatenate([v_ref[r, k0:k1, :], jnp.ones((k1 - k0, LANES), BF16)], axis=1),
                       preferred_element_type=F32) for pb, (r, k0, k1) in zip(p, spans)]
        for (r, lo), mb, ab in zip(blocks, m, acc):
            tok = pl.ds(lo * dil + r, P, stride=dil) if dil > 1 else pl.ds(lo, P)
            m_ref[gi, tok, :] = jnp.broadcast_to(mb, (P, LANES))
            l_ref[gi, tok, :] = ab[:, dh:]
            acc_ref[gi, tok, :] = ab[:, :dh]
    for c in range(seq // ATTN_ROWS):
        rows = pl.ds(c * ATTN_ROWS, ATTN_ROWS)
        ms = [m_ref[gi, rows, :] for gi in range(DIL_GROUPS)]
        m_all = functools.reduce(jnp.maximum, ms)
        ws = [jnp.exp2((mg - m_all) * EXP2_SCALE) for mg in ms]
        l_all = sum(w * l_ref[gi, rows, :] for gi, w in enumerate(ws))
        acc_all = sum(w * acc_ref[gi, rows, :] for gi, w in enumerate(ws))
        o_ref[rows, :] = (acc_all / l_all).astype(o_ref.dtype)


def _dil_attn(groups, batch, seq):
    dh = DIL_HEAD_DIM
    in_specs, args = [], []
    for (window, dil), t in zip(DIL_PATTERNS, groups):
        for c in range(3):
            in_specs.append(pl.BlockSpec((None, None, None, dil, seq // dil, dh),
                                         lambda b, h, c=c: (b, c, h, 0, 0, 0)))
            args.append(t)
    return pl.pallas_call(
        functools.partial(_dil_attn_kernel, seq=seq),
        out_shape=jax.ShapeDtypeStruct((batch, DIL_HEADS, seq, dh), BF16),
        grid=(batch, DIL_HEADS),
        in_specs=in_specs,
        out_specs=pl.BlockSpec((None, None, seq, dh), lambda b, h: (b, h, 0, 0)),
        scratch_shapes=[pltpu.VMEM((DIL_GROUPS, seq, LANES), F32), pltpu.VMEM((DIL_GROUPS, seq, LANES), F32),
                        pltpu.VMEM((DIL_GROUPS, seq, dh), F32)],
        compiler_params=_params("parallel", "parallel"),
        name="dil_attn",
    )(*args)


def _pad_cols(w, n):
    return jnp.pad(w, ((0, 0), (0, n - w.shape[1])))


def _gla_layer(xf, xb, w_in, w_gate_up, gate_bias, norm_g, w_out, gamma, beta, batch, seq):
    w_main = w_in[:, :GLA_MAIN].astype(BF16)
    w_gate = _pad_cols(w_in[:, GLA_MAIN:], LANES).astype(BF16)
    proj, g_low = _gla_proj(xf if xb is None else xb, w_main, w_gate)
    wgu = jnp.pad(w_gate_up, ((0, LANES - GLA_GATE_RANK), (0, 0)))
    o = _gla_core(proj.reshape(batch, seq, GLA_MAIN), g_low.reshape(batch, seq, LANES),
                  wgu, gate_bias[None, :], norm_g[None, :], batch, seq)
    return _mm_res_ln(o.reshape(batch * seq, GLA_DV), w_out.astype(BF16), xf, gamma[None, :], beta[None, :])


def _dil_layer(xf, xb, w_in, w_out, gamma, beta, batch, seq):
    gw = 3 * DIL_WIDTH
    w_in = w_in.astype(BF16)
    groups = [_dil_proj(xb, w_in[:, gi * gw:(gi + 1) * gw], dil, batch, seq)
              for gi, (window, dil) in enumerate(DIL_PATTERNS)]
    o = _dil_attn(groups, batch, seq)
    return _mm_res_ln(o, w_out.astype(BF16), xf, gamma[None, :], beta[None, :])


def _ffn_weights(w_up, conv_w, conv_b, w_down):
    pad = D_FF_PAD - D_FF
    halves = lambda t: [jnp.pad(h, ((0, 0), (0, 0), (0, pad))) for h in (t[..., :D_FF], t[..., D_FF:])]
    wg, wu = (h.astype(BF16) for h in halves(w_up))
    cwg, cwu = halves(conv_w)
    cbg, cbu = halves(conv_b[:, None, :])
    wd = jnp.pad(w_down, ((0, 0), (0, pad), (0, 0))).astype(BF16)
    return wg, wu, cwg, cwu, cbg, cbu, wd


def kernel(x, gla_w_in, gla_w_gate_up, gla_gate_bias, gla_norm_g, gla_w_out, dil_w_in, dil_w_out, ffn_w_up,
           ffn_conv_w, ffn_conv_b, ffn_w_down, ln_g, ln_b):
    batch, seq, d = x.shape
    xf = x.reshape(batch * seq, d)
    xb = None
    ffn_weights = _ffn_weights(ffn_w_up, ffn_conv_w, ffn_conv_b, ffn_w_down)
    for i in range(DEPTH):
        j = i // 2
        if i % 2 == 0:
            xf, xb = _gla_layer(xf, xb, gla_w_in[j], gla_w_gate_up[j], gla_gate_bias[j], gla_norm_g[j],
                                gla_w_out[j], ln_g[i, 0], ln_b[i, 0], batch, seq)
        else:
            xf, xb = _dil_layer(xf, xb, dil_w_in[j], dil_w_out[j], ln_g[i, 0], ln_b[i, 0], batch, seq)
        xf, xb = _ffn(xb, xf, i, *ffn_weights, ln_g[i, 1][None, :], ln_b[i, 1][None, :], seq)
    return xf.reshape(batch, seq, d)
```

```python
import functools

import jax
import jax.numpy as jnp
from jax import lax
from jax.experimental import pallas as pl
from jax.experimental.pallas import tpu as pltpu

F32 = jnp.float32
BF16 = jnp.bfloat16

D_MODEL = 2048
DEPTH = 4

GLA_HEADS = 4
GLA_DK = D_MODEL // 2
GLA_DV = D_MODEL
GLA_HEAD_K = GLA_DK // GLA_HEADS
GLA_HEAD_V = GLA_DV // GLA_HEADS
GLA_GATE_RANK = 16
GLA_GATE_NORMALIZER = 16.0
GLA_CHUNK = 64
GLA_MAIN = 2 * GLA_DK + 2 * GLA_DV

DIL_PATTERNS = ((128, 1), (512, 4), (2048, 16))
DIL_GROUPS = len(DIL_PATTERNS)
DIL_HEADS = 8
DIL_HEAD_DIM = 128
DIL_WIDTH = DIL_HEADS * DIL_HEAD_DIM
DIL_BLOCK = 128

D_FF = 5504
CONV_WIDTH = 3

DEEPNORM_ALPHA = (2 * DEPTH) ** 0.25
LN_EPS = 1e-5
RMS_EPS = 1e-6

LANES = 128
LOG2E = 1.4426950408889634
FF_TILE = 512
D_FF_PAD = -(-D_FF // FF_TILE) * FF_TILE
VMEM_LIMIT = 56 * 1024 * 1024


def _params(*sem):
    return pltpu.CompilerParams(dimension_semantics=sem, vmem_limit_bytes=VMEM_LIMIT)


def _gla_proj_kernel(x_ref, w_ref, wg_ref, o_ref, g_ref, *scratch):
    if scratch:
        (xb_ref,) = scratch

        @pl.when(pl.program_id(1) == 0)
        def _():
            xb_ref[...] = x_ref[...].astype(BF16)
    else:
        xb_ref = x_ref
    x = xb_ref[...]
    o_ref[...] = jnp.dot(x, w_ref[...], preferred_element_type=F32).astype(o_ref.dtype)

    @pl.when(pl.program_id(1) == 0)
    def _():
        g_ref[...] = jnp.dot(x, wg_ref[...], preferred_element_type=F32)


def _gla_proj(x, w_main, w_gate, *, tm=1024, tn=1024):
    m, k = x.shape
    n = w_main.shape[1]
    tm = min(tm, m)
    scratch = [] if x.dtype == BF16 else [pltpu.VMEM((tm, k), BF16)]
    return pl.pallas_call(
        _gla_proj_kernel,
        out_shape=(jax.ShapeDtypeStruct((m, n), BF16), jax.ShapeDtypeStruct((m, LANES), F32)),
        grid=(m // tm, n // tn),
        in_specs=[pl.BlockSpec((tm, k), lambda i, j: (i, 0)),
                  pl.BlockSpec((k, tn), lambda i, j: (0, j)),
                  pl.BlockSpec((k, LANES), lambda i, j: (0, 0))],
        out_specs=(pl.BlockSpec((tm, tn), lambda i, j: (i, j)),
                   pl.BlockSpec((tm, LANES), lambda i, j: (i, 0))),
        scratch_shapes=scratch,
        compiler_params=_params("parallel", "arbitrary"),
        name="gla_proj",
    )(x, w_main, w_gate)


GLA_HEADS_PER_STEP = 2
GLA_PAIRS_PER_STEP = 2
GLA_PAIR_ROWS = 8


def _gla_core_kernel(q_ref, k_ref, v_ref, r_ref, gl_ref, wgu_ref, gb_ref, ng_ref, o_ref,
                     st_ref, a_ref, qd_ref, kd_ref, dec_ref, vt_ref, whi_ref, wlo_ref, *, nchunks):
    C, c = GLA_CHUNK, GLA_PAIR_ROWS
    nsub = C // c
    dk, dv = GLA_HEAD_K, GLA_HEAD_V
    heads = range(GLA_HEADS_PER_STEP)

    rows_i = lax.broadcasted_iota(jnp.int32, (C, C), 0)
    cols_i = lax.broadcasted_iota(jnp.int32, (C, C), 1)
    tri = (rows_i >= cols_i).astype(F32)
    sub_rows = lax.broadcasted_iota(jnp.int32, (c, C), 0)
    sub_cols = lax.broadcasted_iota(jnp.int32, (c, C), 1)
    nt = (((1,), (1,)), ((), ()))
    hi = lax.Precision.HIGHEST

    w = wgu_ref[...]
    whi_ref[...] = w.astype(BF16)
    wlo_ref[...] = (w - whi_ref[...].astype(F32)).astype(BF16)

    def prepare(pi, carry):
        NP = GLA_PAIRS_PER_STEP
        items = [(h, u) for u in range(2 * NP) for h in heads]
        rows = [pl.ds(pl.multiple_of((2 * NP * pi + u) * C, C), C) for h, u in items]
        q = [q_ref[rs, h * dk:(h + 1) * dk].astype(F32) * (dk ** -0.5) for (h, u), rs in zip(items, rows)]
        k = [k_ref[rs, h * dk:(h + 1) * dk].astype(F32) for (h, u), rs in zip(items, rows)]
        gl = [gl_ref[rs, :] for rs in rows]
        gl_hi = [t.astype(BF16) for t in gl]
        gl_lo = [(t - th.astype(F32)).astype(BF16) for t, th in zip(gl, gl_hi)]
        z = [(jnp.dot(th, whi_ref[:, h * dk:(h + 1) * dk], preferred_element_type=F32)
              + jnp.dot(tl, whi_ref[:, h * dk:(h + 1) * dk], preferred_element_type=F32)
              + jnp.dot(th, wlo_ref[:, h * dk:(h + 1) * dk], preferred_element_type=F32)
              + gb_ref[:, h * dk:(h + 1) * dk]) * LOG2E for (h, u), th, tl in zip(items, gl_hi, gl_lo)]
        g = [(jnp.minimum(zh, 0.0) - jnp.log2(1.0 + jnp.exp2(-jnp.abs(zh)))) * (1.0 / GLA_GATE_NORMALIZER) for zh in z]
        b = [jnp.dot(tri, gh, preferred_element_type=F32, precision=hi) for gh in g]
        b_last = [bh[C - 1:C, :] for bh in b]
        qd = [qh * jnp.exp2(bh) for qh, bh in zip(q, b)]
        kd = [kh * jnp.exp2(bl - bh) for kh, bh, bl in zip(k, b, b_last)]
        dec = [jnp.exp2(bl) for bl in b_last]

        n = len(items)
        subs = [(t, s) for s in range(nsub) for t in range(n)]
        a_s = {}
        for t, s in subs:
            lo = s * c
            if s == 0:
                a_s[t, s] = jnp.zeros((c, C), F32)
            else:
                b_ref = b[t][lo - 1:lo, :]
                q_ref_s = (q[t][lo:lo + c, :] * jnp.exp2(b[t][lo:lo + c, :] - b_ref)).astype(BF16)
                k_ref_s = (k[t][:lo, :] * jnp.exp2(b_ref - b[t][:lo, :])).astype(BF16)
                k_ref_s = jnp.concatenate([k_ref_s, jnp.zeros((C - lo, dk), BF16)], axis=0)
                a_s[t, s] = lax.dot_general(q_ref_s, k_ref_s, nt, preferred_element_type=F32)
        for j in range(c):
            for t, s in subs:
                lo = s * c
                bs = b[t][lo:lo + c, :]
                e = jnp.exp2(bs - bs[j:j + 1, :])
                col = jnp.sum(q[t][lo:lo + c, :] * k[t][lo + j:lo + j + 1, :] * e, axis=-1, keepdims=True)
                a_s[t, s] = jnp.where((sub_cols == lo + j) & (sub_rows >= j), col, a_s[t, s])
        a = [jnp.concatenate([a_s[t, s] for s in range(nsub)], axis=0) for t in range(n)]

        for h, pp in [(h, pp) for pp in range(NP) for h in heads]:
            t0, t1 = items.index((h, 2 * pp)), items.index((h, 2 * pp + 1))
            pr = NP * pi + pp
            pair_rows = pl.ds(pl.multiple_of(pr * 2 * C, 2 * C), 2 * C)
            qd0, qd1 = qd[t0].astype(BF16), qd[t1].astype(BF16)
            kd0, kd1 = kd[t0].astype(BF16), kd[t1].astype(BF16)
            cross = lax.dot_general(qd1, kd0, nt, preferred_element_type=F32)
            qd_ref[h, rows[t0], :] = qd0
            qd_ref[h, rows[t1], :] = (qd[t1] * dec[t0]).astype(BF16)
            kd_ref[h, rows[t0], :] = (kd[t0] * dec[t1]).astype(BF16)
            kd_ref[h, rows[t1], :] = kd1
            dec_ref[h, pl.ds(pr, 1), :] = dec[t0] * dec[t1]
            a_ref[h, rows[t0], :] = jnp.concatenate([a[t0], jnp.zeros((C, C), F32)], axis=1).astype(BF16)
            a_ref[h, rows[t1], :] = jnp.concatenate([cross, a[t1]], axis=1).astype(BF16)
            vt_ref[h, pr] = v_ref[pair_rows, h * dv:(h + 1) * dv].astype(F32).T.astype(BF16)
        return carry

    lax.fori_loop(0, nchunks // (2 * GLA_PAIRS_PER_STEP), prepare, 0)

    st_ref[...] = jnp.zeros_like(st_ref)
    ng = ng_ref[...]

    def recur(pi, carry):
        rs = pl.ds(pl.multiple_of(pi * 2 * C, 2 * C), 2 * C)
        st = [st_ref[h] for h in heads]
        o = [lax.dot_general(qd_ref[h, rs, :], st[h].astype(BF16), nt, preferred_element_type=F32) for h in heads]
        o = [o[h] + jnp.dot(a_ref[h, rs, :], v_ref[rs, h * dv:(h + 1) * dv], preferred_element_type=F32) for h in heads]
        upd = [jnp.dot(vt_ref[h, pi], kd_ref[h, rs, :], preferred_element_type=F32) for h in heads]
        for h in heads:
            st_ref[h] = st[h] * dec_ref[h, pl.ds(pi, 1), :] + upd[h]
        o = [oh * lax.rsqrt(jnp.mean(oh * oh, axis=-1, keepdims=True) + RMS_EPS) * ng for oh in o]
        r = [r_ref[rs, h * dv:(h + 1) * dv].astype(F32) for h in heads]
        for h in heads:
            o_ref[rs, h * dv:(h + 1) * dv] = (o[h] * (r[h] / (1.0 + jnp.exp2(r[h] * (-LOG2E))))).astype(o_ref.dtype)
        return carry

    lax.fori_loop(0, nchunks // 2, recur, 0)


def _gla_core(proj, g_low, w_gate_up, gate_bias, norm_g, batch, seq):
    dk, dv, hp = GLA_HEAD_K, GLA_HEAD_V, GLA_HEADS_PER_STEP
    nchunks = seq // GLA_CHUNK
    npairs = nchunks // 2
    q_blk = lambda off: pl.BlockSpec((None, seq, hp * dk), lambda b, h: (b, 0, off // (hp * dk) + h))
    v_blk = lambda off: pl.BlockSpec((None, seq, hp * dv), lambda b, h: (b, 0, off // (hp * dv) + h))
    return pl.pallas_call(
        functools.partial(_gla_core_kernel, nchunks=nchunks),
        out_shape=jax.ShapeDtypeStruct((batch, seq, GLA_DV), BF16),
        grid=(batch, GLA_HEADS // hp),
        in_specs=[q_blk(0), q_blk(GLA_DK), v_blk(2 * GLA_DK), v_blk(2 * GLA_DK + GLA_DV),
                  pl.BlockSpec((None, seq, LANES), lambda b, h: (b, 0, 0)),
                  pl.BlockSpec((LANES, hp * dk), lambda b, h: (0, h)),
                  pl.BlockSpec((1, hp * dk), lambda b, h: (0, h)),
                  pl.BlockSpec((1, dv), lambda b, h: (0, 0))],
        out_specs=pl.BlockSpec((None, seq, hp * dv), lambda b, h: (b, 0, h)),
        scratch_shapes=[pltpu.VMEM((hp, dv, dk), F32),
                        pltpu.VMEM((hp, seq, 2 * GLA_CHUNK), BF16),
                        pltpu.VMEM((hp, seq, dk), BF16),
                        pltpu.VMEM((hp, seq, dk), BF16),
                        pltpu.VMEM((hp, npairs, dk), F32),
                        pltpu.VMEM((hp, npairs, dv, 2 * GLA_CHUNK), BF16),
                        pltpu.VMEM((LANES, hp * dk), BF16),
                        pltpu.VMEM((LANES, hp * dk), BF16)],
        compiler_params=_params("parallel", "arbitrary"),
        name="gla_core",
    )(proj, proj, proj, proj, g_low, w_gate_up, gate_bias, norm_g)


LN_ROWS = 128


def _mm_res_ln_kernel(a_ref, w_ref, x_ref, g_ref, b_ref, of_ref, ob_ref):
    acc_ref = of_ref
    if len(a_ref.shape) == 3:
        a = jnp.concatenate([a_ref[h] for h in range(a_ref.shape[0])], axis=1)
    else:
        a = a_ref[...]
    acc_ref[...] = jnp.dot(a, w_ref[...], preferred_element_type=F32)
    gamma = g_ref[...]
    beta = b_ref[...]

    def rows(i, carry):
        r0 = pl.multiple_of(i * LN_ROWS, LN_ROWS)
        y = DEEPNORM_ALPHA * x_ref[pl.ds(r0, LN_ROWS), :] + acc_ref[pl.ds(r0, LN_ROWS), :]
        mu = jnp.mean(y, axis=-1, keepdims=True)
        yc = y - mu
        var = jnp.mean(yc * yc, axis=-1, keepdims=True)
        out = yc * lax.rsqrt(var + LN_EPS) * gamma + beta
        of_ref[pl.ds(r0, LN_ROWS), :] = out
        ob_ref[pl.ds(r0, LN_ROWS), :] = out.astype(BF16)
        return carry

    lax.fori_loop(0, acc_ref.shape[0] // LN_ROWS, rows, 0)


def _mm_res_ln(a, w, x, gamma, beta, *, tm=512):
    k, n = w.shape
    m = x.shape[0]
    if a.ndim == 4:
        _, heads, seq, dh = a.shape
        tps = seq // tm
        a_spec = pl.BlockSpec((None, heads, tm, dh), lambda i: (i // tps, 0, i % tps, 0))
    else:
        a_spec = pl.BlockSpec((tm, k), lambda i: (i, 0))
    return pl.pallas_call(
        _mm_res_ln_kernel,
        out_shape=(jax.ShapeDtypeStruct((m, n), F32), jax.ShapeDtypeStruct((m, n), BF16)),
        grid=(m // tm,),
        in_specs=[a_spec,
                  pl.BlockSpec((k, n), lambda i: (0, 0)),
                  pl.BlockSpec((tm, n), lambda i: (i, 0)),
                  pl.BlockSpec((1, n), lambda i: (0, 0)),
                  pl.BlockSpec((1, n), lambda i: (0, 0))],
        out_specs=(pl.BlockSpec((tm, n), lambda i: (i, 0)),
                   pl.BlockSpec((tm, n), lambda i: (i, 0))),
        compiler_params=_params("parallel"),
        name="mm_res_ln",
    )(a, w, x, gamma, beta)


FFN_TM = 1024
FFN_XRES = 256
FFN_CHUNK = 512
FFN_PIECE = 32
HALO = 8


def _ffn_kernel(x_ref, wg_ref, wu_ref, cwg_ref, cwu_ref, cbg_ref, cbu_ref, wd_ref, xres_ref, g_ref, b_ref,
                of_ref, ob_ref, y0_ref, y1_ref, halo_ref, act_ref, *, nj, tiles_per_seq):
    i = pl.program_id(0)
    j = pl.program_id(1)
    tm = x_ref.shape[0]
    tf = wd_ref.shape[0]
    nchunk = tm // FFN_CHUNK

    def conv(h, prev, cw_ref, cb_ref):
        cw = cw_ref[...]
        win = jnp.concatenate([prev, h], axis=0)
        return (cw[2:3, :] * h + cw[1:2, :] * pltpu.roll(win, 1, axis=0)[HALO:, :]
                + cw[0:1, :] * pltpu.roll(win, 2, axis=0)[HALO:, :] + cb_ref[...])

    def up_steps(y_ref):
        @pl.when(i % tiles_per_seq == 0)
        def _():
            halo_ref[j] = jnp.zeros(halo_ref.shape[1:], F32)

        prev = [halo_ref[j, :, half * tf:(half + 1) * tf] for half in range(2)]
        for c in range(nchunk):
            rows = pl.ds(c * FFN_CHUNK, FFN_CHUNK)
            xs = x_ref[rows, :]
            for half, (w_ref, cw_ref, cb_ref) in enumerate(((wg_ref, cwg_ref, cbg_ref), (wu_ref, cwu_ref, cbu_ref))):
                h = jnp.dot(xs, w_ref[...], preferred_element_type=F32)
                y_ref[rows, half * tf:(half + 1) * tf] = conv(h, prev[half], cw_ref, cb_ref)
                prev[half] = h[FFN_CHUNK - HALO:, :]
            yield c
        for half in range(2):
            halo_ref[j, :, half * tf:(half + 1) * tf] = prev[half]

    def down_chunk(c, y_ref):
        for p in range(FFN_CHUNK // FFN_PIECE):
            rows = pl.ds(c * FFN_CHUNK + p * FFN_PIECE, FFN_PIECE)
            gate = y_ref[rows, :tf]
            act_ref[rows, :] = (gate / (1.0 + jnp.exp2(gate * (-LOG2E))) * y_ref[rows, tf:]).astype(BF16)
        rows = pl.ds(c * FFN_CHUNK, FFN_CHUNK)
        of_ref[rows, :] += jnp.dot(act_ref[rows, :], wd_ref[...], preferred_element_type=F32)

    nres = tm // FFN_XRES

    def first_step(y_ref):
        for c in up_steps(y_ref):
            pass
        of_ref[pl.ds(0, FFN_XRES), :] = DEEPNORM_ALPHA * xres_ref[...]
        of_ref[pl.ds(FFN_XRES, tm - FFN_XRES), :] = jnp.zeros((tm - FFN_XRES, of_ref.shape[1]), F32)

    def mid_step(y_up, y_prev):
        @pl.when(j < nres)
        def _():
            rows = pl.ds(pl.multiple_of(j * FFN_XRES, FFN_XRES), FFN_XRES)
            of_ref[rows, :] += DEEPNORM_ALPHA * xres_ref[...]

        for c in up_steps(y_up):
            down_chunk(c, y_prev)

    def last_step(y_prev):
        gamma = g_ref[...]
        beta = b_ref[...]
        for c in range(nchunk):
            down_chunk(c, y_prev)
            for p in range(FFN_CHUNK // FFN_PIECE):
                rows = pl.ds(c * FFN_CHUNK + p * FFN_PIECE, FFN_PIECE)
                y = of_ref[rows, :]
                mu = jnp.mean(y, axis=-1, keepdims=True)
                yc = y - mu
                var = jnp.mean(yc * yc, axis=-1, keepdims=True)
                out = yc * lax.rsqrt(var + LN_EPS) * gamma + beta
                of_ref[rows, :] = out
                ob_ref[rows, :] = out.astype(BF16)

    slots = (y0_ref, y1_ref)

    @pl.when(j == 0)
    def _():
        first_step(slots[0])

    for parity in (0, 1):
        @pl.when((j > 0) & (j < nj) & (j % 2 == parity))
        def _():
            mid_step(slots[parity], slots[1 - parity])

    @pl.when(j == nj)
    def _():
        last_step(slots[(nj - 1) % 2])


def _ffn(xb, xf, layer, wg, wu, cwg, cwu, cbg, cbu, wd, gamma, beta, seq):
    m, d = xb.shape
    tf = FF_TILE
    nj = wd.shape[1] // tf
    tm = min(FFN_TM, seq)
    nres = tm // FFN_XRES
    assert nj > nres
    one = pl.Buffered(1)
    up_j = lambda j: jnp.minimum(j, nj - 1)
    down_j = lambda j: jnp.maximum(j - 1, 0)
    up_blk = lambda rows: pl.BlockSpec((None, rows, tf), lambda i, j: (layer, 0, up_j(j)))
    return pl.pallas_call(
        functools.partial(_ffn_kernel, nj=nj, tiles_per_seq=seq // tm),
        out_shape=(jax.ShapeDtypeStruct((m, d), F32), jax.ShapeDtypeStruct((m, d), BF16)),
        grid=(m // tm, nj + 1),
        in_specs=[pl.BlockSpec((tm, d), lambda i, j: (i, 0)),
                  up_blk(d), up_blk(d), up_blk(CONV_WIDTH), up_blk(CONV_WIDTH), up_blk(1), up_blk(1),
                  pl.BlockSpec((None, tf, d), lambda i, j: (layer, down_j(j), 0)),
                  pl.BlockSpec((FFN_XRES, d), lambda i, j: (i * nres + jnp.minimum(j, nres - 1), 0)),
                  pl.BlockSpec((1, d), lambda i, j: (0, 0)),
                  pl.BlockSpec((1, d), lambda i, j: (0, 0))],
        out_specs=(pl.BlockSpec((tm, d), lambda i, j: (i, 0), pipeline_mode=one),
                   pl.BlockSpec((tm, d), lambda i, j: (i, 0))),
        scratch_shapes=[pltpu.VMEM((tm, 2 * tf), F32), pltpu.VMEM((tm, 2 * tf), F32),
                        pltpu.VMEM((nj, HALO, 2 * tf), F32), pltpu.VMEM((tm, tf), BF16)],
        compiler_params=_params("arbitrary", "arbitrary"),
        name="ffn",
    )(xb, wg, wu, cwg, cwu, cbg, cbu, wd, xf, gamma, beta)


DIL_PERMUTE_FROM = 16
DIL_PERMUTE_ROWS = 256


def _dil_proj_kernel(x_ref, w_ref, o_ref, *scratch, dil):
    acc = jnp.dot(x_ref[...], w_ref[...], preferred_element_type=F32)
    dh = DIL_HEAD_DIM
    if dil == 1:
        for h in range(DIL_HEADS):
            o_ref[h, 0] = acc[:, h * dh:(h + 1) * dh].astype(o_ref.dtype)
        return
    if dil < DIL_PERMUTE_FROM:
        (rows_ref,) = scratch
        n = x_ref.shape[0] // dil
        for h in range(DIL_HEADS):
            rows_ref[h] = acc[:, h * dh:(h + 1) * dh]
        for h in range(DIL_HEADS):
            for r in range(dil):
                o_ref[h, r] = rows_ref[h, pl.ds(r, n, stride=dil), :].astype(o_ref.dtype)
        return
    blk = DIL_PERMUTE_ROWS
    n = blk // dil
    dst = lax.broadcasted_iota(jnp.int32, (blk, blk), 0)
    src = lax.broadcasted_iota(jnp.int32, (blk, blk), 1)
    perm = jnp.where((dst // n == src % dil) & (dst % n == src // dil), 1.0, 0.0).astype(BF16)
    y = acc.astype(BF16)
    for q in range(x_ref.shape[0] // blk):
        p = jnp.dot(perm, y[q * blk:(q + 1) * blk, :], preferred_element_type=F32).astype(o_ref.dtype)
        for h in range(DIL_HEADS):
            for r in range(dil):
                o_ref[h, r, q * n:(q + 1) * n, :] = p[r * n:(r + 1) * n, h * dh:(h + 1) * dh]


def _dil_proj(xb, w, dil, batch, seq, *, tm=1024):
    m, k = xb.shape
    H, dh = DIL_HEADS, DIL_HEAD_DIM
    tm = min(tm, seq)
    tps = seq // tm
    scratch = [pltpu.VMEM((H, tm, dh), F32)] if 1 < dil < DIL_PERMUTE_FROM else []
    return pl.pallas_call(
        functools.partial(_dil_proj_kernel, dil=dil),
        out_shape=jax.ShapeDtypeStruct((batch, 3, H, dil, seq // dil, dh), BF16),
        grid=(m // tm, 3),
        in_specs=[pl.BlockSpec((tm, k), lambda i, c: (i, 0)),
                  pl.BlockSpec((k, H * dh), lambda i, c: (0, c))],
        out_specs=pl.BlockSpec((None, None, H, dil, tm // dil, dh), lambda i, c: (i // tps, c, 0, 0, i % tps, 0)),
        scratch_shapes=scratch,
        compiler_params=_params("parallel", "arbitrary"),
        name=f"dil_proj_d{dil}",
    )(xb, w)


NEG = -1e30
EXP2_SCALE = (DIL_HEAD_DIM ** -0.5) * LOG2E
ATTN_ROWS = 256


def _dil_attn_kernel(*refs, seq):
    qkv = refs[:3 * DIL_GROUPS]
    o_ref = refs[3 * DIL_GROUPS]
    m_ref, l_ref, acc_ref = refs[3 * DIL_GROUPS + 1:]
    P = DIL_BLOCK
    dh = DIL_HEAD_DIM
    nt = (((1,), (1,)), ((), ()))
    qi = lax.broadcasted_iota(jnp.int32, (P, P), 0)
    kj = lax.broadcasted_iota(jnp.int32, (P, P), 1)
    for gi, (window, dil) in enumerate(DIL_PATTERNS):
        q_ref, k_ref, v_ref = qkv[3 * gi:3 * gi + 3]
        steps = window // dil
        bias_cur = jnp.where((qi >= kj) & (qi - kj <= steps), 0.0, NEG)
        bias_prev = jnp.where(qi + P - kj <= steps, 0.0, NEG)
        blocks = [(r, lb * P) for r in range(dil) for lb in range(seq // dil // P)]
        bias = {P: bias_cur, 2 * P: jnp.concatenate([bias_prev, bias_cur], axis=1)}
        spans = [(r, lo - P if lo > 0 else lo, lo + P) for r, lo in blocks]
        s = [lax.dot_general(q_ref[r, lo:lo + P, :], k_ref[r, k0:k1, :], nt, preferred_element_type=F32)
             + bias[k1 - k0] for (r, lo), (_, k0, k1) in zip(blocks, spans)]
        m = [jnp.max(sb, axis=-1, keepdims=True) for sb in s]
        p = [jnp.exp2((sb - mb) * EXP2_SCALE).astype(BF16) for sb, mb in zip(s, m)]
        acc = [jnp.dot(pb, jnp.concatenate([v_ref[r, k0:k1, :], jnp.ones((k1 - k0, LANES), BF16)], axis=1),
                       preferred_element_type=F32) for pb, (r, k0, k1) in zip(p, spans)]
        for (r, lo), mb, ab in zip(blocks, m, acc):
            tok = pl.ds(lo * dil + r, P, stride=dil) if dil > 1 else pl.ds(lo, P)
            m_ref[gi, tok, :] = jnp.broadcast_to(mb, (P, LANES))
            l_ref[gi, tok, :] = ab[:, dh:]
            acc_ref[gi, tok, :] = ab[:, :dh]
    for c in range(seq // ATTN_ROWS):
        rows = pl.ds(c * ATTN_ROWS, ATTN_ROWS)
        ms = [m_ref[gi, rows, :] for gi in range(DIL_GROUPS)]
        m_all = functools.reduce(jnp.maximum, ms)
        ws = [jnp.exp2((mg - m_all) * EXP2_SCALE) for mg in ms]
        l_all = sum(w * l_ref[gi, rows, :] for gi, w in enumerate(ws))
        acc_all = sum(w * acc_ref[gi, rows, :] for gi, w in enumerate(ws))
        o_ref[rows, :] = (acc_all / l_all).astype(o_ref.dtype)


def _dil_attn(groups, batch, seq):
    dh = DIL_HEAD_DIM
    in_specs, args = [], []
    for (window, dil), t in zip(DIL_PATTERNS, groups):
        for c in range(3):
            in_specs.append(pl.BlockSpec((None, None, None, dil, seq // dil, dh),
                                         lambda b, h, c=c: (b, c, h, 0, 0, 0)))
            args.append(t)
    return pl.pallas_call(
        functools.partial(_dil_attn_kernel, seq=seq),
        out_shape=jax.ShapeDtypeStruct((batch, DIL_HEADS, seq, dh), BF16),
        grid=(batch, DIL_HEADS),
        in_specs=in_specs,
        out_specs=pl.BlockSpec((None, None, seq, dh), lambda b, h: (b, h, 0, 0)),
        scratch_shapes=[pltpu.VMEM((DIL_GROUPS, seq, LANES), F32), pltpu.VMEM((DIL_GROUPS, seq, LANES), F32),
                        pltpu.VMEM((DIL_GROUPS, seq, dh), F32)],
        compiler_params=_params("parallel", "parallel"),
        name="dil_attn",
    )(*args)


def _pad_cols(w, n):
    return jnp.pad(w, ((0, 0), (0, n - w.shape[1])))


def _gla_layer(xf, xb, w_in, w_gate_up, gate_bias, norm_g, w_out, gamma, beta, batch, seq):
    w_main = w_in[:, :GLA_MAIN].astype(BF16)
    w_gate = _pad_cols(w_in[:, GLA_MAIN:], LANES).astype(BF16)
    proj, g_low = _gla_proj(xf if xb is None else xb, w_main, w_gate)
    wgu = jnp.pad(w_gate_up, ((0, LANES - GLA_GATE_RANK), (0, 0)))
    o = _gla_core(proj.reshape(batch, seq, GLA_MAIN), g_low.reshape(batch, seq, LANES),
                  wgu, gate_bias[None, :], norm_g[None, :], batch, seq)
    return _mm_res_ln(o.reshape(batch * seq, GLA_DV), w_out.astype(BF16), xf, gamma[None, :], beta[None, :])


def _dil_layer(xf, xb, w_in, w_out, gamma, beta, batch, seq):
    gw = 3 * DIL_WIDTH
    w_in = w_in.astype(BF16)
    groups = [_dil_proj(xb, w_in[:, gi * gw:(gi + 1) * gw], dil, batch, seq)
              for gi, (window, dil) in enumerate(DIL_PATTERNS)]
    o = _dil_attn(groups, batch, seq)
    return _mm_res_ln(o, w_out.astype(BF16), xf, gamma[None, :], beta[None, :])


def _ffn_weights(w_up, conv_w, conv_b, w_down):
    pad = D_FF_PAD - D_FF
    halves = lambda t: [jnp.pad(h, ((0, 0), (0, 0), (0, pad))) for h in (t[..., :D_FF], t[..., D_FF:])]
    wg, wu = (h.astype(BF16) for h in halves(w_up))
    cwg, cwu = halves(conv_w)
    cbg, cbu = halves(conv_b[:, None, :])
    wd = jnp.pad(w_down, ((0, 0), (0, pad), (0, 0))).astype(BF16)
    return wg, wu, cwg, cwu, cbg, cbu, wd


def kernel(x, gla_w_in, gla_w_gate_up, gla_gate_bias, gla_norm_g, gla_w_out, dil_w_in, dil_w_out, ffn_w_up,
           ffn_conv_w, ffn_conv_b, ffn_w_down, ln_g, ln_b):
    batch, seq, d = x.shape
    xf = x.reshape(batch * seq, d)
    xb = None
    ffn_weights = _ffn_weights(ffn_w_up, ffn_conv_w, ffn_conv_b, ffn_w_down)
    for i in range(DEPTH):
        j = i // 2
        if i % 2 == 0:
            xf, xb = _gla_layer(xf, xb, gla_w_in[j], gla_w_gate_up[j], gla_gate_bias[j], gla_norm_g[j],
                                gla_w_out[j], ln_g[i, 0], ln_b[i, 0], batch, seq)
        else:
            xf, xb = _dil_layer(xf, xb, dil_w_in[j], dil_w_out[j], ln_g[i, 0], ln_b[i, 0], batch, seq)
        xf, xb = _ffn(xb, xf, i, *ffn_weights, ln_g[i, 1][None, :], ln_b[i, 1][None, :], seq)
    return xf.reshape(batch, seq, d)
```

```python
import functools

import jax
import jax.numpy as jnp
from jax import lax
from jax.experimental import pallas as pl
from jax.experimental.pallas import tpu as pltpu

F32 = jnp.float32
BF16 = jnp.bfloat16

D_MODEL = 2048
DEPTH = 4

GLA_HEADS = 4
GLA_DK = D_MODEL // 2
GLA_DV = D_MODEL
GLA_HEAD_K = GLA_DK // GLA_HEADS
GLA_HEAD_V = GLA_DV // GLA_HEADS
GLA_GATE_RANK = 16
GLA_GATE_NORMALIZER = 16.0
GLA_CHUNK = 64
GLA_MAIN = 2 * GLA_DK + 2 * GLA_DV

DIL_PATTERNS = ((128, 1), (512, 4), (2048, 16))
DIL_GROUPS = len(DIL_PATTERNS)
DIL_HEADS = 8
DIL_HEAD_DIM = 128
DIL_WIDTH = DIL_HEADS * DIL_HEAD_DIM
DIL_BLOCK = 128

D_FF = 5504
CONV_WIDTH = 3

DEEPNORM_ALPHA = (2 * DEPTH) ** 0.25
LN_EPS = 1e-5
RMS_EPS = 1e-6

LANES = 128
LOG2E = 1.4426950408889634
FF_TILE = 512
D_FF_PAD = -(-D_FF // FF_TILE) * FF_TILE
VMEM_LIMIT = 56 * 1024 * 1024


def _params(*sem):
    return pltpu.CompilerParams(dimension_semantics=sem, vmem_limit_bytes=VMEM_LIMIT)


def _gla_proj_kernel(x_ref, w_ref, wg_ref, o_ref, g_ref, *scratch):
    if scratch:
        (xb_ref,) = scratch

        @pl.when(pl.program_id(1) == 0)
        def _():
            xb_ref[...] = x_ref[...].astype(BF16)
    else:
        xb_ref = x_ref
    x = xb_ref[...]
    o_ref[...] = jnp.dot(x, w_ref[...], preferred_element_type=F32).astype(o_ref.dtype)

    @pl.when(pl.program_id(1) == 0)
    def _():
        g_ref[...] = jnp.dot(x, wg_ref[...], preferred_element_type=F32)


def _gla_proj(x, w_main, w_gate, *, tm=1024, tn=1024):
    m, k = x.shape
    n = w_main.shape[1]
    tm = min(tm, m)
    scratch = [] if x.dtype == BF16 else [pltpu.VMEM((tm, k), BF16)]
    return pl.pallas_call(
        _gla_proj_kernel,
        out_shape=(jax.ShapeDtypeStruct((m, n), BF16), jax.ShapeDtypeStruct((m, LANES), F32)),
        grid=(m // tm, n // tn),
        in_specs=[pl.BlockSpec((tm, k), lambda i, j: (i, 0)),
                  pl.BlockSpec((k, tn), lambda i, j: (0, j)),
                  pl.BlockSpec((k, LANES), lambda i, j: (0, 0))],
        out_specs=(pl.BlockSpec((tm, tn), lambda i, j: (i, j)),
                   pl.BlockSpec((tm, LANES), lambda i, j: (i, 0))),
        scratch_shapes=scratch,
        compiler_params=_params("parallel", "arbitrary"),
        name="gla_proj",
    )(x, w_main, w_gate)


GLA_HEADS_PER_STEP = 2
GLA_PAIRS_PER_STEP = 2
GLA_PAIR_ROWS = 8


def _gla_core_kernel(q_ref, k_ref, v_ref, r_ref, gl_ref, wgu_ref, gb_ref, ng_ref, o_ref,
                     st_ref, a_ref, qd_ref, kd_ref, dec_ref, vt_ref, whi_ref, wlo_ref, *, nchunks):
    C, c = GLA_CHUNK, GLA_PAIR_ROWS
    nsub = C // c
    dk, dv = GLA_HEAD_K, GLA_HEAD_V
    heads = range(GLA_HEADS_PER_STEP)

    rows_i = lax.broadcasted_iota(jnp.int32, (C, C), 0)
    cols_i = lax.broadcasted_iota(jnp.int32, (C, C), 1)
    tri = (rows_i >= cols_i).astype(F32)
    sub_rows = lax.broadcasted_iota(jnp.int32, (c, C), 0)
    sub_cols = lax.broadcasted_iota(jnp.int32, (c, C), 1)
    nt = (((1,), (1,)), ((), ()))
    hi = lax.Precision.HIGHEST

    w = wgu_ref[...]
    whi_ref[...] = w.astype(BF16)
    wlo_ref[...] = (w - whi_ref[...].astype(F32)).astype(BF16)

    def prepare(pi, carry):
        NP = GLA_PAIRS_PER_STEP
        items = [(h, u) for u in range(2 * NP) for h in heads]
        rows = [pl.ds(pl.multiple_of((2 * NP * pi + u) * C, C), C) for h, u in items]
        q = [q_ref[rs, h * dk:(h + 1) * dk].astype(F32) * (dk ** -0.5) for (h, u), rs in zip(items, rows)]
        k = [k_ref[rs, h * dk:(h + 1) * dk].astype(F32) for (h, u), rs in zip(items, rows)]
        gl = [gl_ref[rs, :] for rs in rows]
        gl_hi = [t.astype(BF16) for t in gl]
        gl_lo = [(t - th.astype(F32)).astype(BF16) for t, th in zip(gl, gl_hi)]
        z = [(jnp.dot(th, whi_ref[:, h * dk:(h + 1) * dk], preferred_element_type=F32)
              + jnp.dot(tl, whi_ref[:, h * dk:(h + 1) * dk], preferred_element_type=F32)
              + jnp.dot(th, wlo_ref[:, h * dk:(h + 1) * dk], preferred_element_type=F32)
              + gb_ref[:, h * dk:(h + 1) * dk]) * LOG2E for (h, u), th, tl in zip(items, gl_hi, gl_lo)]
        g = [(jnp.minimum(zh, 0.0) - jnp.log2(1.0 + jnp.exp2(-jnp.abs(zh)))) * (1.0 / GLA_GATE_NORMALIZER) for zh in z]
        b = [jnp.dot(tri, gh, preferred_element_type=F32, precision=hi) for gh in g]
        b_last = [bh[C - 1:C, :] for bh in b]
        qd = [qh * jnp.exp2(bh) for qh, bh in zip(q, b)]
        kd = [kh * jnp.exp2(bl - bh) for kh, bh, bl in zip(k, b, b_last)]
        dec = [jnp.exp2(bl) for bl in b_last]

        n = len(items)
        subs = [(t, s) for s in range(nsub) for t in range(n)]
        a_s = {}
        for t, s in subs:
            lo = s * c
            if s == 0:
                a_s[t, s] = jnp.zeros((c, C), F32)
            else:
                b_ref = b[t][lo - 1:lo, :]
                q_ref_s = (q[t][lo:lo + c, :] * jnp.exp2(b[t][lo:lo + c, :] - b_ref)).astype(BF16)
                k_ref_s = (k[t][:lo, :] * jnp.exp2(b_ref - b[t][:lo, :])).astype(BF16)
                k_ref_s = jnp.concatenate([k_ref_s, jnp.zeros((C - lo, dk), BF16)], axis=0)
                a_s[t, s] = lax.dot_general(q_ref_s, k_ref_s, nt, preferred_element_type=F32)
        for j in range(c):
            for t, s in subs:
                lo = s * c
                bs = b[t][lo:lo + c, :]
                e = jnp.exp2(bs - bs[j:j + 1, :])
                col = jnp.sum(q[t][lo:lo + c, :] * k[t][lo + j:lo + j + 1, :] * e, axis=-1, keepdims=True)
                a_s[t, s] = jnp.where((sub_cols == lo + j) & (sub_rows >= j), col, a_s[t, s])
        a = [jnp.concatenate([a_s[t, s] for s in range(nsub)], axis=0) for t in range(n)]

        for h, pp in [(h, pp) for pp in range(NP) for h in heads]:
            t0, t1 = items.index((h, 2 * pp)), items.index((h, 2 * pp + 1))
            pr = NP * pi + pp
            pair_rows = pl.ds(pl.multiple_of(pr * 2 * C, 2 * C), 2 * C)
            qd0, qd1 = qd[t0].astype(BF16), qd[t1].astype(BF16)
            kd0, kd1 = kd[t0].astype(BF16), kd[t1].astype(BF16)
            cross = lax.dot_general(qd1, kd0, nt, preferred_element_type=F32)
            qd_ref[h, rows[t0], :] = qd0
            qd_ref[h, rows[t1], :] = (qd[t1] * dec[t0]).astype(BF16)
            kd_ref[h, rows[t0], :] = (kd[t0] * dec[t1]).astype(BF16)
            kd_ref[h, rows[t1], :] = kd1
            dec_ref[h, pl.ds(pr, 1), :] = dec[t0] * dec[t1]
            a_ref[h, rows[t0], :] = jnp.concatenate([a[t0], jnp.zeros((C, C), F32)], axis=1).astype(BF16)
            a_ref[h, rows[t1], :] = jnp.concatenate([cross, a[t1]], axis=1).astype(BF16)
            vt_ref[h, pr] = v_ref[pair_rows, h * dv:(h + 1) * dv].astype(F32).T.astype(BF16)
        return carry

    lax.fori_loop(0, nchunks // (2 * GLA_PAIRS_PER_STEP), prepare, 0)

    st_ref[...] = jnp.zeros_like(st_ref)
    ng = ng_ref[...]

    def recur(pi, carry):
        rs = pl.ds(pl.multiple_of(pi * 2 * C, 2 * C), 2 * C)
        st = [st_ref[h] for h in heads]
        o = [lax.dot_general(qd_ref[h, rs, :], st[h].astype(BF16), nt, preferred_element_type=F32) for h in heads]
        o = [o[h] + jnp.dot(a_ref[h, rs, :], v_ref[rs, h * dv:(h + 1) * dv], preferred_element_type=F32) for h in heads]
        upd = [jnp.dot(vt_ref[h, pi], kd_ref[h, rs, :], preferred_element_type=F32) for h in heads]
        for h in heads:
            st_ref[h] = st[h] * dec_ref[h, pl.ds(pi, 1), :] + upd[h]
        o = [oh * lax.rsqrt(jnp.mean(oh * oh, axis=-1, keepdims=True) + RMS_EPS) * ng for oh in o]
        r = [r_ref[rs, h * dv:(h + 1) * dv].astype(F32) for h in heads]
        for h in heads:
            o_ref[rs, h * dv:(h + 1) * dv] = (o[h] * (r[h] / (1.0 + jnp.exp2(r[h] * (-LOG2E))))).astype(o_ref.dtype)
        return carry

    lax.fori_loop(0, nchunks // 2, recur, 0)


def _gla_core(proj, g_low, w_gate_up, gate_bias, norm_g, batch, seq):
    dk, dv, hp = GLA_HEAD_K, GLA_HEAD_V, GLA_HEADS_PER_STEP
    nchunks = seq // GLA_CHUNK
    npairs = nchunks // 2
    q_blk = lambda off: pl.BlockSpec((None, seq, hp * dk), lambda b, h: (b, 0, off // (hp * dk) + h))
    v_blk = lambda off: pl.BlockSpec((None, seq, hp * dv), lambda b, h: (b, 0, off // (hp * dv) + h))
    return pl.pallas_call(
        functools.partial(_gla_core_kernel, nchunks=nchunks),
        out_shape=jax.ShapeDtypeStruct((batch, seq, GLA_DV), BF16),
        grid=(batch, GLA_HEADS // hp),
        in_specs=[q_blk(0), q_blk(GLA_DK), v_blk(2 * GLA_DK), v_blk(2 * GLA_DK + GLA_DV),
                  pl.BlockSpec((None, seq, LANES), lambda b, h: (b, 0, 0)),
                  pl.BlockSpec((LANES, hp * dk), lambda b, h: (0, h)),
                  pl.BlockSpec((1, hp * dk), lambda b, h: (0, h)),
                  pl.BlockSpec((1, dv), lambda b, h: (0, 0))],
        out_specs=pl.BlockSpec((None, seq, hp * dv), lambda b, h: (b, 0, h)),
        scratch_shapes=[pltpu.VMEM((hp, dv, dk), F32),
                        pltpu.VMEM((hp, seq, 2 * GLA_CHUNK), BF16),
                        pltpu.VMEM((hp, seq, dk), BF16),
                        pltpu.VMEM((hp, seq, dk), BF16),
                        pltpu.VMEM((hp, npairs, dk), F32),
                        pltpu.VMEM((hp, npairs, dv, 2 * GLA_CHUNK), BF16),
                        pltpu.VMEM((LANES, hp * dk), BF16),
                        pltpu.VMEM((LANES, hp * dk), BF16)],
        compiler_params=_params("parallel", "arbitrary"),
        name="gla_core",
    )(proj, proj, proj, proj, g_low, w_gate_up, gate_bias, norm_g)


LN_ROWS = 128


def _mm_res_ln_kernel(a_ref, w_ref, x_ref, g_ref, b_ref, of_ref, ob_ref):
    acc_ref = of_ref
    if len(a_ref.shape) == 3:
        a = jnp.concatenate([a_ref[h] for h in range(a_ref.shape[0])], axis=1)
    else:
        a = a_ref[...]
    acc_ref[...] = jnp.dot(a, w_ref[...], preferred_element_type=F32)
    gamma = g_ref[...]
    beta = b_ref[...]

    def rows(i, carry):
        r0 = pl.multiple_of(i * LN_ROWS, LN_ROWS)
        y = DEEPNORM_ALPHA * x_ref[pl.ds(r0, LN_ROWS), :] + acc_ref[pl.ds(r0, LN_ROWS), :]
        mu = jnp.mean(y, axis=-1, keepdims=True)
        yc = y - mu
        var = jnp.mean(yc * yc, axis=-1, keepdims=True)
        out = yc * lax.rsqrt(var + LN_EPS) * gamma + beta
        of_ref[pl.ds(r0, LN_ROWS), :] = out
        ob_ref[pl.ds(r0, LN_ROWS), :] = out.astype(BF16)
        return carry

    lax.fori_loop(0, acc_ref.shape[0] // LN_ROWS, rows, 0)


def _mm_res_ln(a, w, x, gamma, beta, *, tm=512):
    k, n = w.shape
    m = x.shape[0]
    if a.ndim == 4:
        _, heads, seq, dh = a.shape
        tps = seq // tm
        a_spec = pl.BlockSpec((None, heads, tm, dh), lambda i: (i // tps, 0, i % tps, 0))
    else:
        a_spec = pl.BlockSpec((tm, k), lambda i: (i, 0))
    return pl.pallas_call(
        _mm_res_ln_kernel,
        out_shape=(jax.ShapeDtypeStruct((m, n), F32), jax.ShapeDtypeStruct((m, n), BF16)),
        grid=(m // tm,),
        in_specs=[a_spec,
                  pl.BlockSpec((k, n), lambda i: (0, 0)),
                  pl.BlockSpec((tm, n), lambda i: (i, 0)),
                  pl.BlockSpec((1, n), lambda i: (0, 0)),
                  pl.BlockSpec((1, n), lambda i: (0, 0))],
        out_specs=(pl.BlockSpec((tm, n), lambda i: (i, 0)),
                   pl.BlockSpec((tm, n), lambda i: (i, 0))),
        compiler_params=_params("parallel"),
        name="mm_res_ln",
    )(a, w, x, gamma, beta)


FFN_TM = 1024
FFN_XRES = 256
FFN_CHUNK = 512
FFN_PIECE = 32
HALO = 8


def _ffn_kernel(x_ref, wg_ref, wu_ref, cwg_ref, cwu_ref, cbg_ref, cbu_ref, wd_ref, xres_ref, g_ref, b_ref,
                of_ref, ob_ref, y0_ref, y1_ref, halo_ref, *, nj, tiles_per_seq):
    i = pl.program_id(0)
    j = pl.program_id(1)
    tm = x_ref.shape[0]
    tf = wd_ref.shape[0]
    nchunk = tm // FFN_CHUNK

    def conv(h, prev, cw_ref, cb_ref):
        cw = cw_ref[...]
        win = jnp.concatenate([prev, h], axis=0)
        return (cw[2:3, :] * h + cw[1:2, :] * pltpu.roll(win, 1, axis=0)[HALO:, :]
                + cw[0:1, :] * pltpu.roll(win, 2, axis=0)[HALO:, :] + cb_ref[...])

    def up_steps(y_ref):
        @pl.when(i % tiles_per_seq == 0)
        def _():
            halo_ref[j] = jnp.zeros(halo_ref.shape[1:], F32)

        prev = [halo_ref[j, :, half * tf:(half + 1) * tf] for half in range(2)]
        for c in range(nchunk):
            rows = pl.ds(c * FFN_CHUNK, FFN_CHUNK)
            xs = x_ref[rows, :]
            ys = []
            for half, (w_ref, cw_ref, cb_ref) in enumerate(((wg_ref, cwg_ref, cbg_ref), (wu_ref, cwu_ref, cbu_ref))):
                h = jnp.dot(xs, w_ref[...], preferred_element_type=F32)
                ys.append(conv(h, prev[half], cw_ref, cb_ref))
                prev[half] = h[FFN_CHUNK - HALO:, :]
            gate, up = ys
            y_ref[rows, :] = (gate / (1.0 + jnp.exp2(gate * (-LOG2E))) * up).astype(BF16)
            yield c
        for half in range(2):
            halo_ref[j, :, half * tf:(half + 1) * tf] = prev[half]

    def down_chunk(c, y_ref):
        rows = pl.ds(c * FFN_CHUNK, FFN_CHUNK)
        of_ref[rows, :] += jnp.dot(y_ref[rows, :], wd_ref[...], preferred_element_type=F32)

    nres = tm // FFN_XRES

    def first_step(y_ref):
        for c in up_steps(y_ref):
            pass
        of_ref[pl.ds(0, FFN_XRES), :] = DEEPNORM_ALPHA * xres_ref[...]
        of_ref[pl.ds(FFN_XRES, tm - FFN_XRES), :] = jnp.zeros((tm - FFN_XRES, of_ref.shape[1]), F32)

    def mid_step(y_up, y_prev):
        @pl.when(j < nres)
        def _():
            rows = pl.ds(pl.multiple_of(j * FFN_XRES, FFN_XRES), FFN_XRES)
            of_ref[rows, :] += DEEPNORM_ALPHA * xres_ref[...]

        for c in up_steps(y_up):
            down_chunk(c, y_prev)

    def last_step(y_prev):
        gamma = g_ref[...]
        beta = b_ref[...]
        for c in range(nchunk):
            down_chunk(c, y_prev)
            for p in range(FFN_CHUNK // FFN_PIECE):
                rows = pl.ds(c * FFN_CHUNK + p * FFN_PIECE, FFN_PIECE)
                y = of_ref[rows, :]
                mu = jnp.mean(y, axis=-1, keepdims=True)
                yc = y - mu
                var = jnp.mean(yc * yc, axis=-1, keepdims=True)
                out = yc * lax.rsqrt(var + LN_EPS) * gamma + beta
                of_ref[rows, :] = out
                ob_ref[rows, :] = out.astype(BF16)

    slots = (y0_ref, y1_ref)

    @pl.when(j == 0)
    def _():
        first_step(slots[0])

    for parity in (0, 1):
        @pl.when((j > 0) & (j < nj) & (j % 2 == parity))
        def _():
            mid_step(slots[parity], slots[1 - parity])

    @pl.when(j == nj)
    def _():
        last_step(slots[(nj - 1) % 2])


def _ffn(xb, xf, layer, wg, wu, cwg, cwu, cbg, cbu, wd, gamma, beta, seq):
    m, d = xb.shape
    tf = FF_TILE
    nj = wd.shape[1] // tf
    tm = min(FFN_TM, seq)
    nres = tm // FFN_XRES
    assert nj > nres
    up_j = lambda j: jnp.minimum(j, nj - 1)
    down_j = lambda j: jnp.maximum(j - 1, 0)
    up_blk = lambda rows: pl.BlockSpec((None, rows, tf), lambda i, j: (layer, 0, up_j(j)))
    return pl.pallas_call(
        functools.partial(_ffn_kernel, nj=nj, tiles_per_seq=seq // tm),
        out_shape=(jax.ShapeDtypeStruct((m, d), F32), jax.ShapeDtypeStruct((m, d), BF16)),
        grid=(m // tm, nj + 1),
        in_specs=[pl.BlockSpec((tm, d), lambda i, j: (i, 0)),
                  up_blk(d), up_blk(d), up_blk(CONV_WIDTH), up_blk(CONV_WIDTH), up_blk(1), up_blk(1),
                  pl.BlockSpec((None, tf, d), lambda i, j: (layer, down_j(j), 0)),
                  pl.BlockSpec((FFN_XRES, d), lambda i, j: (i * nres + jnp.minimum(j, nres - 1), 0)),
                  pl.BlockSpec((1, d), lambda i, j: (0, 0)),
                  pl.BlockSpec((1, d), lambda i, j: (0, 0))],
        out_specs=(pl.BlockSpec((tm, d), lambda i, j: (i, 0)),
                   pl.BlockSpec((tm, d), lambda i, j: (i, 0))),
        scratch_shapes=[pltpu.VMEM((tm, tf), BF16), pltpu.VMEM((tm, tf), BF16),
                        pltpu.VMEM((nj, HALO, 2 * tf), F32)],
        compiler_params=_params("arbitrary", "arbitrary"),
        name="ffn",
    )(xb, wg, wu, cwg, cwu, cbg, cbu, wd, xf, gamma, beta)


DIL_PERMUTE_FROM = 16
DIL_PERMUTE_ROWS = 256


def _dil_proj_kernel(x_ref, w_ref, o_ref, *scratch, dil):
    acc = jnp.dot(x_ref[...], w_ref[...], preferred_element_type=F32)
    dh = DIL_HEAD_DIM
    if dil == 1:
        for h in range(DIL_HEADS):
            o_ref[h, 0] = acc[:, h * dh:(h + 1) * dh].astype(o_ref.dtype)
        return
    if dil < DIL_PERMUTE_FROM:
        (rows_ref,) = scratch
        n = x_ref.shape[0] // dil
        for h in range(DIL_HEADS):
            rows_ref[h] = acc[:, h * dh:(h + 1) * dh]
        for h in range(DIL_HEADS):
            for r in range(dil):
                o_ref[h, r] = rows_ref[h, pl.ds(r, n, stride=dil), :].astype(o_ref.dtype)
        return
    blk = DIL_PERMUTE_ROWS
    n = blk // dil
    dst = lax.broadcasted_iota(jnp.int32, (blk, blk), 0)
    src = lax.broadcasted_iota(jnp.int32, (blk, blk), 1)
    perm = jnp.where((dst // n == src % dil) & (dst % n == src // dil), 1.0, 0.0).astype(BF16)
    y = acc.astype(BF16)
    for q in range(x_ref.shape[0] // blk):
        p = jnp.dot(perm, y[q * blk:(q + 1) * blk, :], preferred_element_type=F32).astype(o_ref.dtype)
        for h in range(DIL_HEADS):
            for r in range(dil):
                o_ref[h, r, q * n:(q + 1) * n, :] = p[r * n:(r + 1) * n, h * dh:(h + 1) * dh]


def _dil_proj(xb, w, dil, batch, seq, *, tm=1024):
    m, k = xb.shape
    H, dh = DIL_HEADS, DIL_HEAD_DIM
    tm = min(tm, seq)
    tps = seq // tm
    scratch = [pltpu.VMEM((H, tm, dh), F32)] if 1 < dil < DIL_PERMUTE_FROM else []
    return pl.pallas_call(
        functools.partial(_dil_proj_kernel, dil=dil),
        out_shape=jax.ShapeDtypeStruct((batch, 3, H, dil, seq // dil, dh), BF16),
        grid=(m // tm, 3),
        in_specs=[pl.BlockSpec((tm, k), lambda i, c: (i, 0)),
                  pl.BlockSpec((k, H * dh), lambda i, c: (0, c))],
        out_specs=pl.BlockSpec((None, None, H, dil, tm // dil, dh), lambda i, c: (i // tps, c, 0, 0, i % tps, 0)),
        scratch_shapes=scratch,
        compiler_params=_params("parallel", "arbitrary"),
        name=f"dil_proj_d{dil}",
    )(xb, w)


NEG = -1e30
EXP2_SCALE = (DIL_HEAD_DIM ** -0.5) * LOG2E
ATTN_ROWS = 256


def _dil_attn_kernel(*refs, seq):
    qkv = refs[:3 * DIL_GROUPS]
    o_ref = refs[3 * DIL_GROUPS]
    m_ref, l_ref, acc_ref = refs[3 * DIL_GROUPS + 1:]
    P = DIL_BLOCK
    dh = DIL_HEAD_DIM
    nt = (((1,), (1,)), ((), ()))
    qi = lax.broadcasted_iota(jnp.int32, (P, P), 0)
    kj = lax.broadcasted_iota(jnp.int32, (P, P), 1)
    for gi, (window, dil) in enumerate(DIL_PATTERNS):
        q_ref, k_ref, v_ref = qkv[3 * gi:3 * gi + 3]
        steps = window // dil
        bias_cur = jnp.where((qi >= kj) & (qi - kj <= steps), 0.0, NEG)
        bias_prev = jnp.where(qi + P - kj <= steps, 0.0, NEG)
        blocks = [(r, lb * P) for r in range(dil) for lb in range(seq // dil // P)]
        bias = {P: bias_cur, 2 * P: jnp.concatenate([bias_prev, bias_cur], axis=1)}
        spans = [(r, lo - P if lo > 0 else lo, lo + P) for r, lo in blocks]
        s = [lax.dot_general(q_ref[r, lo:lo + P, :], k_ref[r, k0:k1, :], nt, preferred_element_type=F32)
             + bias[k1 - k0] for (r, lo), (_, k0, k1) in zip(blocks, spans)]
        m = [jnp.max(sb, axis=-1, keepdims=True) for sb in s]
        p = [jnp.exp2((sb - mb) * EXP2_SCALE).astype(BF16) for sb, mb in zip(s, m)]
        acc = [jnp.dot(pb, jnp.concatenate([v_ref[r, k0:k1, :], jnp.ones((k1 - k0, LANES), BF16)], axis=1),
                       preferred_element_type=F32) for pb, (r, k0, k1) in zip(p, spans)]
        for (r, lo), mb, ab in zip(blocks, m, acc):
            tok = pl.ds(lo * dil + r, P, stride=dil) if dil > 1 else pl.ds(lo, P)
            m_ref[gi, tok, :] = jnp.broadcast_to(mb, (P, LANES))
            l_ref[gi, tok, :] = ab[:, dh:]
            acc_ref[gi, tok, :] = ab[:, :dh]
    for c in range(seq // ATTN_ROWS):
        rows = pl.ds(c * ATTN_ROWS, ATTN_ROWS)
        ms = [m_ref[gi, rows, :] for gi in range(DIL_GROUPS)]
        m_all = functools.reduce(jnp.maximum, ms)
        ws = [jnp.exp2((mg - m_all) * EXP2_SCALE) for mg in ms]
        l_all = sum(w * l_ref[gi, rows, :] for gi, w in enumerate(ws))
        acc_all = sum(w * acc_ref[gi, rows, :] for gi, w in enumerate(ws))
        o_ref[rows, :] = (acc_all / l_all).astype(o_ref.dtype)


def _dil_attn(groups, batch, seq):
    dh = DIL_HEAD_DIM
    in_specs, args = [], []
    for (window, dil), t in zip(DIL_PATTERNS, groups):
        for c in range(3):
            in_specs.append(pl.BlockSpec((None, None, None, dil, seq // dil, dh),
                                         lambda b, h, c=c: (b, c, h, 0, 0, 0)))
            args.append(t)
    return pl.pallas_call(
        functools.partial(_dil_attn_kernel, seq=seq),
        out_shape=jax.ShapeDtypeStruct((batch, DIL_HEADS, seq, dh), BF16),
        grid=(batch, DIL_HEADS),
        in_specs=in_specs,
        out_specs=pl.BlockSpec((None, None, seq, dh), lambda b, h: (b, h, 0, 0)),
        scratch_shapes=[pltpu.VMEM((DIL_GROUPS, seq, LANES), F32), pltpu.VMEM((DIL_GROUPS, seq, LANES), F32),
                        pltpu.VMEM((DIL_GROUPS, seq, dh), F32)],
        compiler_params=_params("parallel", "parallel"),
        name="dil_attn",
    )(*args)


def _pad_cols(w, n):
    return jnp.pad(w, ((0, 0), (0, n - w.shape[1])))


def _gla_layer(xf, xb, w_in, w_gate_up, gate_bias, norm_g, w_out, gamma, beta, batch, seq):
    w_main = w_in[:, :GLA_MAIN].astype(BF16)
    w_gate = _pad_cols(w_in[:, GLA_MAIN:], LANES).astype(BF16)
    proj, g_low = _gla_proj(xf if xb is None else xb, w_main, w_gate)
    wgu = jnp.pad(w_gate_up, ((0, LANES - GLA_GATE_RANK), (0, 0)))
    o = _gla_core(proj.reshape(batch, seq, GLA_MAIN), g_low.reshape(batch, seq, LANES),
                  wgu, gate_bias[None, :], norm_g[None, :], batch, seq)
    return _mm_res_ln(o.reshape(batch * seq, GLA_DV), w_out.astype(BF16), xf, gamma[None, :], beta[None, :])


def _dil_layer(xf, xb, w_in, w_out, gamma, beta, batch, seq):
    gw = 3 * DIL_WIDTH
    w_in = w_in.astype(BF16)
    groups = [_dil_proj(xb, w_in[:, gi * gw:(gi + 1) * gw], dil, batch, seq)
              for gi, (window, dil) in enumerate(DIL_PATTERNS)]
    o = _dil_attn(groups, batch, seq)
    return _mm_res_ln(o, w_out.astype(BF16), xf, gamma[None, :], beta[None, :])


def _ffn_weights(w_up, conv_w, conv_b, w_down):
    pad = D_FF_PAD - D_FF
    halves = lambda t: [jnp.pad(h, ((0, 0), (0, 0), (0, pad))) for h in (t[..., :D_FF], t[..., D_FF:])]
    wg, wu = (h.astype(BF16) for h in halves(w_up))
    cwg, cwu = halves(conv_w)
    cbg, cbu = halves(conv_b[:, None, :])
    wd = jnp.pad(w_down, ((0, 0), (0, pad), (0, 0))).astype(BF16)
    return wg, wu, cwg, cwu, cbg, cbu, wd


def kernel(x, gla_w_in, gla_w_gate_up, gla_gate_bias, gla_norm_g, gla_w_out, dil_w_in, dil_w_out, ffn_w_up,
           ffn_conv_w, ffn_conv_b, ffn_w_down, ln_g, ln_b):
    batch, seq, d = x.shape
    xf = x.reshape(batch * seq, d)
    xb = None
    ffn_weights = _ffn_weights(ffn_w_up, ffn_conv_w, ffn_conv_b, ffn_w_down)
    for i in range(DEPTH):
        j = i // 2
        if i % 2 == 0:
            xf, xb = _gla_layer(xf, xb, gla_w_in[j], gla_w_gate_up[j], gla_gate_bias[j], gla_norm_g[j],
                                gla_w_out[j], ln_g[i, 0], ln_b[i, 0], batch, seq)
        else:
            xf, xb = _dil_layer(xf, xb, dil_w_in[j], dil_w_out[j], ln_g[i, 0], ln_b[i, 0], batch, seq)
        xf, xb = _ffn(xb, xf, i, *ffn_weights, ln_g[i, 1][None, :], ln_b[i, 1][None, :], seq)
    return xf.reshape(batch, seq, d)
```

```python
import functools

import jax
import jax.numpy as jnp
from jax import lax
from jax.experimental import pallas as pl
from jax.experimental.pallas import tpu as pltpu

F32 = jnp.float32
BF16 = jnp.bfloat16

D_MODEL = 2048
DEPTH = 4

GLA_HEADS = 4
GLA_DK = D_MODEL // 2
GLA_DV = D_MODEL
GLA_HEAD_K = GLA_DK // GLA_HEADS
GLA_HEAD_V = GLA_DV // GLA_HEADS
GLA_GATE_RANK = 16
GLA_GATE_NORMALIZER = 16.0
GLA_CHUNK = 64
GLA_MAIN = 2 * GLA_DK + 2 * GLA_DV

DIL_PATTERNS = ((128, 1), (512, 4), (2048, 16))
DIL_GROUPS = len(DIL_PATTERNS)
DIL_HEADS = 8
DIL_HEAD_DIM = 128
DIL_WIDTH = DIL_HEADS * DIL_HEAD_DIM
DIL_BLOCK = 128

D_FF = 5504
CONV_WIDTH = 3

DEEPNORM_ALPHA = (2 * DEPTH) ** 0.25
LN_EPS = 1e-5
RMS_EPS = 1e-6

LANES = 128
LOG2E = 1.4426950408889634
FF_TILE = 512
D_FF_PAD = -(-D_FF // FF_TILE) * FF_TILE
VMEM_LIMIT = 56 * 1024 * 1024


def _params(*sem):
    return pltpu.CompilerParams(dimension_semantics=sem, vmem_limit_bytes=VMEM_LIMIT)


def _gla_proj_kernel(x_ref, w_ref, wg_ref, o_ref, g_ref, *scratch):
    if scratch:
        (xb_ref,) = scratch

        @pl.when(pl.program_id(1) == 0)
        def _():
            xb_ref[...] = x_ref[...].astype(BF16)
    else:
        xb_ref = x_ref
    x = xb_ref[...]
    o_ref[...] = jnp.dot(x, w_ref[...], preferred_element_type=F32).astype(o_ref.dtype)

    @pl.when(pl.program_id(1) == 0)
    def _():
        g_ref[...] = jnp.dot(x, wg_ref[...], preferred_element_type=F32)


def _gla_proj(x, w_main, w_gate, *, tm=1024, tn=1024):
    m, k = x.shape
    n = w_main.shape[1]
    tm = min(tm, m)
    scratch = [] if x.dtype == BF16 else [pltpu.VMEM((tm, k), BF16)]
    return pl.pallas_call(
        _gla_proj_kernel,
        out_shape=(jax.ShapeDtypeStruct((m, n), BF16), jax.ShapeDtypeStruct((m, LANES), F32)),
        grid=(m // tm, n // tn),
        in_specs=[pl.BlockSpec((tm, k), lambda i, j: (i, 0)),
                  pl.BlockSpec((k, tn), lambda i, j: (0, j)),
                  pl.BlockSpec((k, LANES), lambda i, j: (0, 0))],
        out_specs=(pl.BlockSpec((tm, tn), lambda i, j: (i, j)),
                   pl.BlockSpec((tm, LANES), lambda i, j: (i, 0))),
        scratch_shapes=scratch,
        compiler_params=_params("parallel", "arbitrary"),
        name="gla_proj",
    )(x, w_main, w_gate)


GLA_HEADS_PER_STEP = 2
GLA_PAIRS_PER_STEP = 2
GLA_PAIR_ROWS = 8


def _gla_core_kernel(q_ref, k_ref, v_ref, r_ref, gl_ref, wgu_ref, gb_ref, ng_ref, o_ref,
                     st_ref, a_ref, qd_ref, kd_ref, dec_ref, vt_ref, whi_ref, wlo_ref, *, nchunks):
    C, c = GLA_CHUNK, GLA_PAIR_ROWS
    nsub = C // c
    dk, dv = GLA_HEAD_K, GLA_HEAD_V
    heads = range(GLA_HEADS_PER_STEP)

    rows_i = lax.broadcasted_iota(jnp.int32, (C, C), 0)
    cols_i = lax.broadcasted_iota(jnp.int32, (C, C), 1)
    tri = (rows_i >= cols_i).astype(F32)
    sub_rows = lax.broadcasted_iota(jnp.int32, (c, C), 0)
    sub_cols = lax.broadcasted_iota(jnp.int32, (c, C), 1)
    nt = (((1,), (1,)), ((), ()))
    hi = lax.Precision.HIGHEST

    w = wgu_ref[...]
    whi_ref[...] = w.astype(BF16)
    wlo_ref[...] = (w - whi_ref[...].astype(F32)).astype(BF16)

    def prepare(pi, carry):
        NP = GLA_PAIRS_PER_STEP
        items = [(h, u) for u in range(2 * NP) for h in heads]
        rows = [pl.ds(pl.multiple_of((2 * NP * pi + u) * C, C), C) for h, u in items]
        q = [q_ref[rs, h * dk:(h + 1) * dk].astype(F32) * (dk ** -0.5) for (h, u), rs in zip(items, rows)]
        k = [k_ref[rs, h * dk:(h + 1) * dk].astype(F32) for (h, u), rs in zip(items, rows)]
        gl = [gl_ref[rs, :] for rs in rows]
        gl_hi = [t.astype(BF16) for t in gl]
        gl_lo = [(t - th.astype(F32)).astype(BF16) for t, th in zip(gl, gl_hi)]
        z = [(jnp.dot(th, whi_ref[:, h * dk:(h + 1) * dk], preferred_element_type=F32)
              + jnp.dot(tl, whi_ref[:, h * dk:(h + 1) * dk], preferred_element_type=F32)
              + jnp.dot(th, wlo_ref[:, h * dk:(h + 1) * dk], preferred_element_type=F32)
              + gb_ref[:, h * dk:(h + 1) * dk]) * LOG2E for (h, u), th, tl in zip(items, gl_hi, gl_lo)]
        g = [(jnp.minimum(zh, 0.0) - jnp.log2(1.0 + jnp.exp2(-jnp.abs(zh)))) * (1.0 / GLA_GATE_NORMALIZER) for zh in z]
        b = [jnp.dot(tri, gh, preferred_element_type=F32, precision=hi) for gh in g]
        b_last = [bh[C - 1:C, :] for bh in b]
        qd = [qh * jnp.exp2(bh) for qh, bh in zip(q, b)]
        kd = [kh * jnp.exp2(bl - bh) for kh, bh, bl in zip(k, b, b_last)]
        dec = [jnp.exp2(bl) for bl in b_last]

        n = len(items)
        subs = [(t, s) for s in range(nsub) for t in range(n)]
        a_s = {}
        for t, s in subs:
            lo = s * c
            if s == 0:
                a_s[t, s] = jnp.zeros((c, C), F32)
            else:
                b_ref = b[t][lo - 1:lo, :]
                q_ref_s = (q[t][lo:lo + c, :] * jnp.exp2(b[t][lo:lo + c, :] - b_ref)).astype(BF16)
                k_ref_s = (k[t][:lo, :] * jnp.exp2(b_ref - b[t][:lo, :])).astype(BF16)
                k_ref_s = jnp.concatenate([k_ref_s, jnp.zeros((C - lo, dk), BF16)], axis=0)
                a_s[t, s] = lax.dot_general(q_ref_s, k_ref_s, nt, preferred_element_type=F32)
        for j in range(c):
            for t, s in subs:
                lo = s * c
                bs = b[t][lo:lo + c, :]
                e = jnp.exp2(bs - bs[j:j + 1, :])
                col = jnp.sum(q[t][lo:lo + c, :] * k[t][lo + j:lo + j + 1, :] * e, axis=-1, keepdims=True)
                a_s[t, s] = jnp.where((sub_cols == lo + j) & (sub_rows >= j), col, a_s[t, s])
        a = [jnp.concatenate([a_s[t, s] for s in range(nsub)], axis=0) for t in range(n)]

        for h, pp in [(h, pp) for pp in range(NP) for h in heads]:
            t0, t1 = items.index((h, 2 * pp)), items.index((h, 2 * pp + 1))
            pr = NP * pi + pp
            pair_rows = pl.ds(pl.multiple_of(pr * 2 * C, 2 * C), 2 * C)
            qd0, qd1 = qd[t0].astype(BF16), qd[t1].astype(BF16)
            kd0, kd1 = kd[t0].astype(BF16), kd[t1].astype(BF16)
            cross = lax.dot_general(qd1, kd0, nt, preferred_element_type=F32)
            qd_ref[h, rows[t0], :] = qd0
            qd_ref[h, rows[t1], :] = (qd[t1] * dec[t0]).astype(BF16)
            kd_ref[h, rows[t0], :] = (kd[t0] * dec[t1]).astype(BF16)
            kd_ref[h, rows[t1], :] = kd1
            dec_ref[h, pl.ds(pr, 1), :] = dec[t0] * dec[t1]
            a_ref[h, rows[t0], :] = jnp.concatenate([a[t0], jnp.zeros((C, C), F32)], axis=1).astype(BF16)
            a_ref[h, rows[t1], :] = jnp.concatenate([cross, a[t1]], axis=1).astype(BF16)
            vt_ref[h, pr] = v_ref[pair_rows, h * dv:(h + 1) * dv].astype(F32).T.astype(BF16)
        return carry

    lax.fori_loop(0, nchunks // (2 * GLA_PAIRS_PER_STEP), prepare, 0)

    st_ref[...] = jnp.zeros_like(st_ref)
    ng = ng_ref[...]

    def recur(pi, carry):
        rs = pl.ds(pl.multiple_of(pi * 2 * C, 2 * C), 2 * C)
        st = [st_ref[h] for h in heads]
        o = [lax.dot_general(qd_ref[h, rs, :], st[h].astype(BF16), nt, preferred_element_type=F32) for h in heads]
        o = [o[h] + jnp.dot(a_ref[h, rs, :], v_ref[rs, h * dv:(h + 1) * dv], preferred_element_type=F32) for h in heads]
        upd = [jnp.dot(vt_ref[h, pi], kd_ref[h, rs, :], preferred_element_type=F32) for h in heads]
        for h in heads:
            st_ref[h] = st[h] * dec_ref[h, pl.ds(pi, 1), :] + upd[h]
        o = [oh * lax.rsqrt(jnp.mean(oh * oh, axis=-1, keepdims=True) + RMS_EPS) * ng for oh in o]
        r = [r_ref[rs, h * dv:(h + 1) * dv].astype(F32) for h in heads]
        for h in heads:
            o_ref[rs, h * dv:(h + 1) * dv] = (o[h] * (r[h] / (1.0 + jnp.exp2(r[h] * (-LOG2E))))).astype(o_ref.dtype)
        return carry

    lax.fori_loop(0, nchunks // 2, recur, 0)


def _gla_core(proj, g_low, w_gate_up, gate_bias, norm_g, batch, seq):
    dk, dv, hp = GLA_HEAD_K, GLA_HEAD_V, GLA_HEADS_PER_STEP
    nchunks = seq // GLA_CHUNK
    npairs = nchunks // 2
    q_blk = lambda off: pl.BlockSpec((None, seq, hp * dk), lambda b, h: (b, 0, off // (hp * dk) + h))
    v_blk = lambda off: pl.BlockSpec((None, seq, hp * dv), lambda b, h: (b, 0, off // (hp * dv) + h))
    return pl.pallas_call(
        functools.partial(_gla_core_kernel, nchunks=nchunks),
        out_shape=jax.ShapeDtypeStruct((batch, seq, GLA_DV), BF16),
        grid=(batch, GLA_HEADS // hp),
        in_specs=[q_blk(0), q_blk(GLA_DK), v_blk(2 * GLA_DK), v_blk(2 * GLA_DK + GLA_DV),
                  pl.BlockSpec((None, seq, LANES), lambda b, h: (b, 0, 0)),
                  pl.BlockSpec((LANES, hp * dk), lambda b, h: (0, h)),
                  pl.BlockSpec((1, hp * dk), lambda b, h: (0, h)),
                  pl.BlockSpec((1, dv), lambda b, h: (0, 0))],
        out_specs=pl.BlockSpec((None, seq, hp * dv), lambda b, h: (b, 0, h)),
        scratch_shapes=[pltpu.VMEM((hp, dv, dk), F32),
                        pltpu.VMEM((hp, seq, 2 * GLA_CHUNK), BF16),
                        pltpu.VMEM((hp, seq, dk), BF16),
                        pltpu.VMEM((hp, seq, dk), BF16),
                        pltpu.VMEM((hp, npairs, dk), F32),
                        pltpu.VMEM((hp, npairs, dv, 2 * GLA_CHUNK), BF16),
                        pltpu.VMEM((LANES, hp * dk), BF16),
                        pltpu.VMEM((LANES, hp * dk), BF16)],
        compiler_params=_params("parallel", "arbitrary"),
        name="gla_core",
    )(proj, proj, proj, proj, g_low, w_gate_up, gate_bias, norm_g)


LN_ROWS = 128


def _mm_res_ln_kernel(a_ref, w_ref, x_ref, g_ref, b_ref, of_ref, ob_ref):
    acc_ref = of_ref
    if len(a_ref.shape) == 3:
        a = jnp.concatenate([a_ref[h] for h in range(a_ref.shape[0])], axis=1)
    else:
        a = a_ref[...]
    acc_ref[...] = jnp.dot(a, w_ref[...], preferred_element_type=F32)
    gamma = g_ref[...]
    beta = b_ref[...]

    def rows(i, carry):
        r0 = pl.multiple_of(i * LN_ROWS, LN_ROWS)
        y = DEEPNORM_ALPHA * x_ref[pl.ds(r0, LN_ROWS), :] + acc_ref[pl.ds(r0, LN_ROWS), :]
        mu = jnp.mean(y, axis=-1, keepdims=True)
        yc = y - mu
        var = jnp.mean(yc * yc, axis=-1, keepdims=True)
        out = yc * lax.rsqrt(var + LN_EPS) * gamma + beta
        of_ref[pl.ds(r0, LN_ROWS), :] = out
        ob_ref[pl.ds(r0, LN_ROWS), :] = out.astype(BF16)
        return carry

    lax.fori_loop(0, acc_ref.shape[0] // LN_ROWS, rows, 0)


def _mm_res_ln(a, w, x, gamma, beta, *, tm=512):
    k, n = w.shape
    m = x.shape[0]
    if a.ndim == 4:
        _, heads, seq, dh = a.shape
        tps = seq // tm
        a_spec = pl.BlockSpec((None, heads, tm, dh), lambda i: (i // tps, 0, i % tps, 0))
    else:
        a_spec = pl.BlockSpec((tm, k), lambda i: (i, 0))
    return pl.pallas_call(
        _mm_res_ln_kernel,
        out_shape=(jax.ShapeDtypeStruct((m, n), F32), jax.ShapeDtypeStruct((m, n), BF16)),
        grid=(m // tm,),
        in_specs=[a_spec,
                  pl.BlockSpec((k, n), lambda i: (0, 0)),
                  pl.BlockSpec((tm, n), lambda i: (i, 0)),
                  pl.BlockSpec((1, n), lambda i: (0, 0)),
                  pl.BlockSpec((1, n), lambda i: (0, 0))],
        out_specs=(pl.BlockSpec((tm, n), lambda i: (i, 0)),
                   pl.BlockSpec((tm, n), lambda i: (i, 0))),
        compiler_params=_params("parallel"),
        name="mm_res_ln",
    )(a, w, x, gamma, beta)


FFN_TM = 1024
FFN_XRES = 256
FFN_CHUNK = 256
FFN_PIECE = 32
HALO = 8


def _ffn_kernel(x_ref, wg_ref, wu_ref, cwg_ref, cwu_ref, cbg_ref, cbu_ref, wd_ref, xres_ref, g_ref, b_ref,
                of_ref, ob_ref, y0_ref, y1_ref, halo_ref, *, nj, tiles_per_seq):
    i = pl.program_id(0)
    j = pl.program_id(1)
    tm = x_ref.shape[0]
    tf = wd_ref.shape[0]
    nchunk = tm // FFN_CHUNK

    def conv(h, prev, cw_ref, cb_ref):
        cw = cw_ref[...]
        win = jnp.concatenate([prev, h], axis=0)
        return (cw[2:3, :] * h + cw[1:2, :] * pltpu.roll(win, 1, axis=0)[HALO:, :]
                + cw[0:1, :] * pltpu.roll(win, 2, axis=0)[HALO:, :] + cb_ref[...])

    def up_steps(y_ref):
        @pl.when(i % tiles_per_seq == 0)
        def _():
            halo_ref[j] = jnp.zeros(halo_ref.shape[1:], F32)

        prev = [halo_ref[j, :, half * tf:(half + 1) * tf] for half in range(2)]
        for c in range(nchunk):
            rows = pl.ds(c * FFN_CHUNK, FFN_CHUNK)
            xs = x_ref[rows, :]
            ys = []
            for half, (w_ref, cw_ref, cb_ref) in enumerate(((wg_ref, cwg_ref, cbg_ref), (wu_ref, cwu_ref, cbu_ref))):
                h = jnp.dot(xs, w_ref[...], preferred_element_type=F32)
                ys.append(conv(h, prev[half], cw_ref, cb_ref))
                prev[half] = h[FFN_CHUNK - HALO:, :]
            gate, up = ys
            y_ref[rows, :] = (gate / (1.0 + jnp.exp2(gate * (-LOG2E))) * up).astype(BF16)
            yield c
        for half in range(2):
            halo_ref[j, :, half * tf:(half + 1) * tf] = prev[half]

    def down_chunk(c, y_ref):
        rows = pl.ds(c * FFN_CHUNK, FFN_CHUNK)
        of_ref[rows, :] += jnp.dot(y_ref[rows, :], wd_ref[...], preferred_element_type=F32)

    nres = tm // FFN_XRES

    def first_step(y_ref):
        for c in up_steps(y_ref):
            pass
        of_ref[pl.ds(0, FFN_XRES), :] = DEEPNORM_ALPHA * xres_ref[...]
        of_ref[pl.ds(FFN_XRES, tm - FFN_XRES), :] = jnp.zeros((tm - FFN_XRES, of_ref.shape[1]), F32)

    def mid_step(y_up, y_prev):
        @pl.when(j < nres)
        def _():
            rows = pl.ds(pl.multiple_of(j * FFN_XRES, FFN_XRES), FFN_XRES)
            of_ref[rows, :] += DEEPNORM_ALPHA * xres_ref[...]

        for c in up_steps(y_up):
            down_chunk(c, y_prev)

    def last_step(y_prev):
        gamma = g_ref[...]
        beta = b_ref[...]
        for c in range(nchunk):
            down_chunk(c, y_prev)
            for p in range(FFN_CHUNK // FFN_PIECE):
                rows = pl.ds(c * FFN_CHUNK + p * FFN_PIECE, FFN_PIECE)
                y = of_ref[rows, :]
                mu = jnp.mean(y, axis=-1, keepdims=True)
                yc = y - mu
                var = jnp.mean(yc * yc, axis=-1, keepdims=True)
                out = yc * lax.rsqrt(var + LN_EPS) * gamma + beta
                of_ref[rows, :] = out
                ob_ref[rows, :] = out.astype(BF16)

    slots = (y0_ref, y1_ref)

    @pl.when(j == 0)
    def _():
        first_step(slots[0])

    for parity in (0, 1):
        @pl.when((j > 0) & (j < nj) & (j % 2 == parity))
        def _():
            mid_step(slots[parity], slots[1 - parity])

    @pl.when(j == nj)
    def _():
        last_step(slots[(nj - 1) % 2])


def _ffn(xb, xf, layer, wg, wu, cwg, cwu, cbg, cbu, wd, gamma, beta, seq):
    m, d = xb.shape
    tf = FF_TILE
    nj = wd.shape[1] // tf
    tm = min(FFN_TM, seq)
    nres = tm // FFN_XRES
    assert nj > nres
    up_j = lambda j: jnp.minimum(j, nj - 1)
    down_j = lambda j: jnp.maximum(j - 1, 0)
    up_blk = lambda rows: pl.BlockSpec((None, rows, tf), lambda i, j: (layer, 0, up_j(j)))
    return pl.pallas_call(
        functools.partial(_ffn_kernel, nj=nj, tiles_per_seq=seq // tm),
        out_shape=(jax.ShapeDtypeStruct((m, d), F32), jax.ShapeDtypeStruct((m, d), BF16)),
        grid=(m // tm, nj + 1),
        in_specs=[pl.BlockSpec((tm, d), lambda i, j: (i, 0)),
                  up_blk(d), up_blk(d), up_blk(CONV_WIDTH), up_blk(CONV_WIDTH), up_blk(1), up_blk(1),
                  pl.BlockSpec((None, tf, d), lambda i, j: (layer, down_j(j), 0)),
                  pl.BlockSpec((FFN_XRES, d), lambda i, j: (i * nres + jnp.minimum(j, nres - 1), 0)),
                  pl.BlockSpec((1, d), lambda i, j: (0, 0)),
                  pl.BlockSpec((1, d), lambda i, j: (0, 0))],
        out_specs=(pl.BlockSpec((tm, d), lambda i, j: (i, 0)),
                   pl.BlockSpec((tm, d), lambda i, j: (i, 0))),
        scratch_shapes=[pltpu.VMEM((tm, tf), BF16), pltpu.VMEM((tm, tf), BF16),
                        pltpu.VMEM((nj, HALO, 2 * tf), F32)],
        compiler_params=_params("arbitrary", "arbitrary"),
        name="ffn",
    )(xb, wg, wu, cwg, cwu, cbg, cbu, wd, xf, gamma, beta)


DIL_PERMUTE_FROM = 16
DIL_PERMUTE_ROWS = 256


def _dil_proj_kernel(x_ref, w_ref, o_ref, *scratch, dil):
    acc = jnp.dot(x_ref[...], w_ref[...], preferred_element_type=F32)
    dh = DIL_HEAD_DIM
    if dil == 1:
        for h in range(DIL_HEADS):
            o_ref[h, 0] = acc[:, h * dh:(h + 1) * dh].astype(o_ref.dtype)
        return
    if dil < DIL_PERMUTE_FROM:
        (rows_ref,) = scratch
        n = x_ref.shape[0] // dil
        for h in range(DIL_HEADS):
            rows_ref[h] = acc[:, h * dh:(h + 1) * dh]
        for h in range(DIL_HEADS):
            for r in range(dil):
                o_ref[h, r] = rows_ref[h, pl.ds(r, n, stride=dil), :].astype(o_ref.dtype)
        return
    blk = DIL_PERMUTE_ROWS
    n = blk // dil
    dst = lax.broadcasted_iota(jnp.int32, (blk, blk), 0)
    src = lax.broadcasted_iota(jnp.int32, (blk, blk), 1)
    perm = jnp.where((dst // n == src % dil) & (dst % n == src // dil), 1.0, 0.0).astype(BF16)
    y = acc.astype(BF16)
    for q in range(x_ref.shape[0] // blk):
        p = jnp.dot(perm, y[q * blk:(q + 1) * blk, :], preferred_element_type=F32).astype(o_ref.dtype)
        for h in range(DIL_HEADS):
            for r in range(dil):
                o_ref[h, r, q * n:(q + 1) * n, :] = p[r * n:(r + 1) * n, h * dh:(h + 1) * dh]


def _dil_proj(xb, w, dil, batch, seq, *, tm=1024):
    m, k = xb.shape
    H, dh = DIL_HEADS, DIL_HEAD_DIM
    tm = min(tm, seq)
    tps = seq // tm
    scratch = [pltpu.VMEM((H, tm, dh), F32)] if 1 < dil < DIL_PERMUTE_FROM else []
    return pl.pallas_call(
        functools.partial(_dil_proj_kernel, dil=dil),
        out_shape=jax.ShapeDtypeStruct((batch, 3, H, dil, seq // dil, dh), BF16),
        grid=(m // tm, 3),
        in_specs=[pl.BlockSpec((tm, k), lambda i, c: (i, 0)),
                  pl.BlockSpec((k, H * dh), lambda i, c: (0, c))],
        out_specs=pl.BlockSpec((None, None, H, dil, tm // dil, dh), lambda i, c: (i // tps, c, 0, 0, i % tps, 0)),
        scratch_shapes=scratch,
        compiler_params=_params("parallel", "arbitrary"),
        name=f"dil_proj_d{dil}",
    )(xb, w)


NEG = -1e30
EXP2_SCALE = (DIL_HEAD_DIM ** -0.5) * LOG2E
ATTN_ROWS = 256


def _dil_attn_kernel(*refs, seq):
    qkv = refs[:3 * DIL_GROUPS]
    o_ref = refs[3 * DIL_GROUPS]
    m_ref, l_ref, acc_ref = refs[3 * DIL_GROUPS + 1:]
    P = DIL_BLOCK
    dh = DIL_HEAD_DIM
    nt = (((1,), (1,)), ((), ()))
    qi = lax.broadcasted_iota(jnp.int32, (P, P), 0)
    kj = lax.broadcasted_iota(jnp.int32, (P, P), 1)
    for gi, (window, dil) in enumerate(DIL_PATTERNS):
        q_ref, k_ref, v_ref = qkv[3 * gi:3 * gi + 3]
        steps = window // dil
        bias_cur = jnp.where((qi >= kj) & (qi - kj <= steps), 0.0, NEG)
        bias_prev = jnp.where(qi + P - kj <= steps, 0.0, NEG)
        blocks = [(r, lb * P) for r in range(dil) for lb in range(seq // dil // P)]
        bias = {P: bias_cur, 2 * P: jnp.concatenate([bias_prev, bias_cur], axis=1)}
        spans = [(r, lo - P if lo > 0 else lo, lo + P) for r, lo in blocks]
        s = [lax.dot_general(q_ref[r, lo:lo + P, :], k_ref[r, k0:k1, :], nt, preferred_element_type=F32)
             + bias[k1 - k0] for (r, lo), (_, k0, k1) in zip(blocks, spans)]
        m = [jnp.max(sb, axis=-1, keepdims=True) for sb in s]
        p = [jnp.exp2((sb - mb) * EXP2_SCALE).astype(BF16) for sb, mb in zip(s, m)]
        acc = [jnp.dot(pb, jnp.concatenate([v_ref[r, k0:k1, :], jnp.ones((k1 - k0, LANES), BF16)], axis=1),
                       preferred_element_type=F32) for pb, (r, k0, k1) in zip(p, spans)]
        for (r, lo), mb, ab in zip(blocks, m, acc):
            tok = pl.ds(lo * dil + r, P, stride=dil) if dil > 1 else pl.ds(lo, P)
            m_ref[gi, tok, :] = jnp.broadcast_to(mb, (P, LANES))
            l_ref[gi, tok, :] = ab[:, dh:]
            acc_ref[gi, tok, :] = ab[:, :dh]
    for c in range(seq // ATTN_ROWS):
        rows = pl.ds(c * ATTN_ROWS, ATTN_ROWS)
        ms = [m_ref[gi, rows, :] for gi in range(DIL_GROUPS)]
        m_all = functools.reduce(jnp.maximum, ms)
        ws = [jnp.exp2((mg - m_all) * EXP2_SCALE) for mg in ms]
        l_all = sum(w * l_ref[gi, rows, :] for gi, w in enumerate(ws))
        acc_all = sum(w * acc_ref[gi, rows, :] for gi, w in enumerate(ws))
        o_ref[rows, :] = (acc_all / l_all).astype(o_ref.dtype)


def _dil_attn(groups, batch, seq):
    dh = DIL_HEAD_DIM
    in_specs, args = [], []
    for (window, dil), t in zip(DIL_PATTERNS, groups):
        for c in range(3):
            in_specs.append(pl.BlockSpec((None, None, None, dil, seq // dil, dh),
                                         lambda b, h, c=c: (b, c, h, 0, 0, 0)))
            args.append(t)
    return pl.pallas_call(
        functools.partial(_dil_attn_kernel, seq=seq),
        out_shape=jax.ShapeDtypeStruct((batch, DIL_HEADS, seq, dh), BF16),
        grid=(batch, DIL_HEADS),
        in_specs=in_specs,
        out_specs=pl.BlockSpec((None, None, seq, dh), lambda b, h: (b, h, 0, 0)),
        scratch_shapes=[pltpu.VMEM((DIL_GROUPS, seq, LANES), F32), pltpu.VMEM((DIL_GROUPS, seq, LANES), F32),
                        pltpu.VMEM((DIL_GROUPS, seq, dh), F32)],
        compiler_params=_params("parallel", "parallel"),
        name="dil_attn",
    )(*args)


def _pad_cols(w, n):
    return jnp.pad(w, ((0, 0), (0, n - w.shape[1])))


def _gla_layer(xf, xb, w_in, w_gate_up, gate_bias, norm_g, w_out, gamma, beta, batch, seq):
    w_main = w_in[:, :GLA_MAIN].astype(BF16)
    w_gate = _pad_cols(w_in[:, GLA_MAIN:], LANES).astype(BF16)
    proj, g_low = _gla_proj(xf if xb is None else xb, w_main, w_gate)
    wgu = jnp.pad(w_gate_up, ((0, LANES - GLA_GATE_RANK), (0, 0)))
    o = _gla_core(proj.reshape(batch, seq, GLA_MAIN), g_low.reshape(batch, seq, LANES),
                  wgu, gate_bias[None, :], norm_g[None, :], batch, seq)
    return _mm_res_ln(o.reshape(batch * seq, GLA_DV), w_out.astype(BF16), xf, gamma[None, :], beta[None, :])


def _dil_layer(xf, xb, w_in, w_out, gamma, beta, batch, seq):
    gw = 3 * DIL_WIDTH
    w_in = w_in.astype(BF16)
    groups = [_dil_proj(xb, w_in[:, gi * gw:(gi + 1) * gw], dil, batch, seq)
              for gi, (window, dil) in enumerate(DIL_PATTERNS)]
    o = _dil_attn(groups, batch, seq)
    return _mm_res_ln(o, w_out.astype(BF16), xf, gamma[None, :], beta[None, :])


def _ffn_weights(w_up, conv_w, conv_b, w_down):
    pad = D_FF_PAD - D_FF
    halves = lambda t: [jnp.pad(h, ((0, 0), (0, 0), (0, pad))) for h in (t[..., :D_FF], t[..., D_FF:])]
    wg, wu = (h.astype(BF16) for h in halves(w_up))
    cwg, cwu = halves(conv_w)
    cbg, cbu = halves(conv_b[:, None, :])
    wd = jnp.pad(w_down, ((0, 0), (0, pad), (0, 0))).astype(BF16)
    return wg, wu, cwg, cwu, cbg, cbu, wd


def kernel(x, gla_w_in, gla_w_gate_up, gla_gate_bias, gla_norm_g, gla_w_out, dil_w_in, dil_w_out, ffn_w_up,
           ffn_conv_w, ffn_conv_b, ffn_w_down, ln_g, ln_b):
    batch, seq, d = x.shape
    xf = x.reshape(batch * seq, d)
    xb = None
    ffn_weights = _ffn_weights(ffn_w_up, ffn_conv_w, ffn_conv_b, ffn_w_down)
    for i in range(DEPTH):
        j = i // 2
        if i % 2 == 0:
            xf, xb = _gla_layer(xf, xb, gla_w_in[j], gla_w_gate_up[j], gla_gate_bias[j], gla_norm_g[j],
                                gla_w_out[j], ln_g[i, 0], ln_b[i, 0], batch, seq)
        else:
            xf, xb = _dil_layer(xf, xb, dil_w_in[j], dil_w_out[j], ln_g[i, 0], ln_b[i, 0], batch, seq)
        xf, xb = _ffn(xb, xf, i, *ffn_weights, ln_g[i, 1][None, :], ln_b[i, 1][None, :], seq)
    return xf.reshape(batch, seq, d)
```

```python
import functools

import jax
import jax.numpy as jnp
from jax import lax
from jax.experimental import pallas as pl
from jax.experimental.pallas import tpu as pltpu

F32 = jnp.float32
BF16 = jnp.bfloat16

D_MODEL = 2048
DEPTH = 4

GLA_HEADS = 4
GLA_DK = D_MODEL // 2
GLA_DV = D_MODEL
GLA_HEAD_K = GLA_DK // GLA_HEADS
GLA_HEAD_V = GLA_DV // GLA_HEADS
GLA_GATE_RANK = 16
GLA_GATE_NORMALIZER = 16.0
GLA_CHUNK = 64
GLA_MAIN = 2 * GLA_DK + 2 * GLA_DV

DIL_PATTERNS = ((128, 1), (512, 4), (2048, 16))
DIL_GROUPS = len(DIL_PATTERNS)
DIL_HEADS = 8
DIL_HEAD_DIM = 128
DIL_WIDTH = DIL_HEADS * DIL_HEAD_DIM
DIL_BLOCK = 128

D_FF = 5504
CONV_WIDTH = 3

DEEPNORM_ALPHA = (2 * DEPTH) ** 0.25
LN_EPS = 1e-5
RMS_EPS = 1e-6

LANES = 128
LOG2E = 1.4426950408889634
FF_TILE = 512
D_FF_PAD = -(-D_FF // FF_TILE) * FF_TILE
VMEM_LIMIT = 56 * 1024 * 1024
FFN_VMEM_LIMIT = 62 * 1024 * 1024


def _params(*sem, vmem_limit=VMEM_LIMIT):
    return pltpu.CompilerParams(dimension_semantics=sem, vmem_limit_bytes=vmem_limit)


def _gla_proj_kernel(x_ref, w_ref, wg_ref, o_ref, g_ref, *scratch):
    if scratch:
        (xb_ref,) = scratch

        @pl.when(pl.program_id(1) == 0)
        def _():
            xb_ref[...] = x_ref[...].astype(BF16)
    else:
        xb_ref = x_ref
    x = xb_ref[...]
    o_ref[...] = jnp.dot(x, w_ref[...], preferred_element_type=F32).astype(o_ref.dtype)

    @pl.when(pl.program_id(1) == 0)
    def _():
        g_ref[...] = jnp.dot(x, wg_ref[...], preferred_element_type=F32)


def _gla_proj(x, w_main, w_gate, *, tm=1024, tn=1024):
    m, k = x.shape
    n = w_main.shape[1]
    tm = min(tm, m)
    scratch = [] if x.dtype == BF16 else [pltpu.VMEM((tm, k), BF16)]
    return pl.pallas_call(
        _gla_proj_kernel,
        out_shape=(jax.ShapeDtypeStruct((m, n), BF16), jax.ShapeDtypeStruct((m, LANES), F32)),
        grid=(m // tm, n // tn),
        in_specs=[pl.BlockSpec((tm, k), lambda i, j: (i, 0)),
                  pl.BlockSpec((k, tn), lambda i, j: (0, j)),
                  pl.BlockSpec((k, LANES), lambda i, j: (0, 0))],
        out_specs=(pl.BlockSpec((tm, tn), lambda i, j: (i, j)),
                   pl.BlockSpec((tm, LANES), lambda i, j: (i, 0))),
        scratch_shapes=scratch,
        compiler_params=_params("parallel", "arbitrary"),
        name="gla_proj",
    )(x, w_main, w_gate)


GLA_HEADS_PER_STEP = 2
GLA_PAIRS_PER_STEP = 2
GLA_PAIR_ROWS = 8


def _gla_core_kernel(q_ref, k_ref, v_ref, r_ref, gl_ref, wgu_ref, gb_ref, ng_ref, o_ref,
                     st_ref, a_ref, qd_ref, kd_ref, dec_ref, vt_ref, whi_ref, wlo_ref, *, nchunks):
    C, c = GLA_CHUNK, GLA_PAIR_ROWS
    nsub = C // c
    dk, dv = GLA_HEAD_K, GLA_HEAD_V
    heads = range(GLA_HEADS_PER_STEP)

    rows_i = lax.broadcasted_iota(jnp.int32, (C, C), 0)
    cols_i = lax.broadcasted_iota(jnp.int32, (C, C), 1)
    tri = (rows_i >= cols_i).astype(F32)
    sub_rows = lax.broadcasted_iota(jnp.int32, (c, C), 0)
    sub_cols = lax.broadcasted_iota(jnp.int32, (c, C), 1)
    nt = (((1,), (1,)), ((), ()))
    hi = lax.Precision.HIGHEST

    w = wgu_ref[...]
    whi_ref[...] = w.astype(BF16)
    wlo_ref[...] = (w - whi_ref[...].astype(F32)).astype(BF16)

    def prepare(pi, carry):
        NP = GLA_PAIRS_PER_STEP
        items = [(h, u) for u in range(2 * NP) for h in heads]
        rows = [pl.ds(pl.multiple_of((2 * NP * pi + u) * C, C), C) for h, u in items]
        q = [q_ref[rs, h * dk:(h + 1) * dk].astype(F32) * (dk ** -0.5) for (h, u), rs in zip(items, rows)]
        k = [k_ref[rs, h * dk:(h + 1) * dk].astype(F32) for (h, u), rs in zip(items, rows)]
        gl = [gl_ref[rs, :] for rs in rows]
        gl_hi = [t.astype(BF16) for t in gl]
        gl_lo = [(t - th.astype(F32)).astype(BF16) for t, th in zip(gl, gl_hi)]
        z = [(jnp.dot(th, whi_ref[:, h * dk:(h + 1) * dk], preferred_element_type=F32)
              + jnp.dot(tl, whi_ref[:, h * dk:(h + 1) * dk], preferred_element_type=F32)
              + jnp.dot(th, wlo_ref[:, h * dk:(h + 1) * dk], preferred_element_type=F32)
              + gb_ref[:, h * dk:(h + 1) * dk]) * LOG2E for (h, u), th, tl in zip(items, gl_hi, gl_lo)]
        g = [(jnp.minimum(zh, 0.0) - jnp.log2(1.0 + jnp.exp2(-jnp.abs(zh)))) * (1.0 / GLA_GATE_NORMALIZER) for zh in z]
        b = [jnp.dot(tri, gh, preferred_element_type=F32, precision=hi) for gh in g]
        b_last = [bh[C - 1:C, :] for bh in b]
        qd = [qh * jnp.exp2(bh) for qh, bh in zip(q, b)]
        kd = [kh * jnp.exp2(bl - bh) for kh, bh, bl in zip(k, b, b_last)]
        dec = [jnp.exp2(bl) for bl in b_last]

        n = len(items)
        subs = [(t, s) for s in range(nsub) for t in range(n)]
        a_s = {}
        for t, s in subs:
            lo = s * c
            if s == 0:
                a_s[t, s] = jnp.zeros((c, C), F32)
            else:
                b_ref = b[t][lo - 1:lo, :]
                q_ref_s = (q[t][lo:lo + c, :] * jnp.exp2(b[t][lo:lo + c, :] - b_ref)).astype(BF16)
                k_ref_s = (k[t][:lo, :] * jnp.exp2(b_ref - b[t][:lo, :])).astype(BF16)
                k_ref_s = jnp.concatenate([k_ref_s, jnp.zeros((C - lo, dk), BF16)], axis=0)
                a_s[t, s] = lax.dot_general(q_ref_s, k_ref_s, nt, preferred_element_type=F32)
        for j in range(c):
            for t, s in subs:
                lo = s * c
                bs = b[t][lo:lo + c, :]
                e = jnp.exp2(bs - bs[j:j + 1, :])
                col = jnp.sum(q[t][lo:lo + c, :] * k[t][lo + j:lo + j + 1, :] * e, axis=-1, keepdims=True)
                a_s[t, s] = jnp.where((sub_cols == lo + j) & (sub_rows >= j), col, a_s[t, s])
        a = [jnp.concatenate([a_s[t, s] for s in range(nsub)], axis=0) for t in range(n)]

        for h, pp in [(h, pp) for pp in range(NP) for h in heads]:
            t0, t1 = items.index((h, 2 * pp)), items.index((h, 2 * pp + 1))
            pr = NP * pi + pp
            pair_rows = pl.ds(pl.multiple_of(pr * 2 * C, 2 * C), 2 * C)
            qd0, qd1 = qd[t0].astype(BF16), qd[t1].astype(BF16)
            kd0, kd1 = kd[t0].astype(BF16), kd[t1].astype(BF16)
            cross = lax.dot_general(qd1, kd0, nt, preferred_element_type=F32)
            qd_ref[h, rows[t0], :] = qd0
            qd_ref[h, rows[t1], :] = (qd[t1] * dec[t0]).astype(BF16)
            kd_ref[h, rows[t0], :] = (kd[t0] * dec[t1]).astype(BF16)
            kd_ref[h, rows[t1], :] = kd1
            dec_ref[h, pl.ds(pr, 1), :] = dec[t0] * dec[t1]
            a_ref[h, rows[t0], :] = jnp.concatenate([a[t0], jnp.zeros((C, C), F32)], axis=1).astype(BF16)
            a_ref[h, rows[t1], :] = jnp.concatenate([cross, a[t1]], axis=1).astype(BF16)
            vt_ref[h, pr] = v_ref[pair_rows, h * dv:(h + 1) * dv].astype(F32).T.astype(BF16)
        return carry

    lax.fori_loop(0, nchunks // (2 * GLA_PAIRS_PER_STEP), prepare, 0)

    st_ref[...] = jnp.zeros_like(st_ref)
    ng = ng_ref[...]

    def recur(pi, carry):
        rs = pl.ds(pl.multiple_of(pi * 2 * C, 2 * C), 2 * C)
        st = [st_ref[h] for h in heads]
        o = [lax.dot_general(qd_ref[h, rs, :], st[h].astype(BF16), nt, preferred_element_type=F32) for h in heads]
        o = [o[h] + jnp.dot(a_ref[h, rs, :], v_ref[rs, h * dv:(h + 1) * dv], preferred_element_type=F32) for h in heads]
        upd = [jnp.dot(vt_ref[h, pi], kd_ref[h, rs, :], preferred_element_type=F32) for h in heads]
        for h in heads:
            st_ref[h] = st[h] * dec_ref[h, pl.ds(pi, 1), :] + upd[h]
        o = [oh * lax.rsqrt(jnp.mean(oh * oh, axis=-1, keepdims=True) + RMS_EPS) * ng for oh in o]
        r = [r_ref[rs, h * dv:(h + 1) * dv].astype(F32) for h in heads]
        for h in heads:
            o_ref[rs, h * dv:(h + 1) * dv] = (o[h] * (r[h] / (1.0 + jnp.exp2(r[h] * (-LOG2E))))).astype(o_ref.dtype)
        return carry

    lax.fori_loop(0, nchunks // 2, recur, 0)


def _gla_core(proj, g_low, w_gate_up, gate_bias, norm_g, batch, seq):
    dk, dv, hp = GLA_HEAD_K, GLA_HEAD_V, GLA_HEADS_PER_STEP
    nchunks = seq // GLA_CHUNK
    npairs = nchunks // 2
    q_blk = lambda off: pl.BlockSpec((None, seq, hp * dk), lambda b, h: (b, 0, off // (hp * dk) + h))
    v_blk = lambda off: pl.BlockSpec((None, seq, hp * dv), lambda b, h: (b, 0, off // (hp * dv) + h))
    return pl.pallas_call(
        functools.partial(_gla_core_kernel, nchunks=nchunks),
        out_shape=jax.ShapeDtypeStruct((batch, seq, GLA_DV), BF16),
        grid=(batch, GLA_HEADS // hp),
        in_specs=[q_blk(0), q_blk(GLA_DK), v_blk(2 * GLA_DK), v_blk(2 * GLA_DK + GLA_DV),
                  pl.BlockSpec((None, seq, LANES), lambda b, h: (b, 0, 0)),
                  pl.BlockSpec((LANES, hp * dk), lambda b, h: (0, h)),
                  pl.BlockSpec((1, hp * dk), lambda b, h: (0, h)),
                  pl.BlockSpec((1, dv), lambda b, h: (0, 0))],
        out_specs=pl.BlockSpec((None, seq, hp * dv), lambda b, h: (b, 0, h)),
        scratch_shapes=[pltpu.VMEM((hp, dv, dk), F32),
                        pltpu.VMEM((hp, seq, 2 * GLA_CHUNK), BF16),
                        pltpu.VMEM((hp, seq, dk), BF16),
                        pltpu.VMEM((hp, seq, dk), BF16),
                        pltpu.VMEM((hp, npairs, dk), F32),
                        pltpu.VMEM((hp, npairs, dv, 2 * GLA_CHUNK), BF16),
                        pltpu.VMEM((LANES, hp * dk), BF16),
                        pltpu.VMEM((LANES, hp * dk), BF16)],
        compiler_params=_params("parallel", "arbitrary"),
        name="gla_core",
    )(proj, proj, proj, proj, g_low, w_gate_up, gate_bias, norm_g)


LN_ROWS = 128


def _mm_res_ln_kernel(a_ref, w_ref, x_ref, g_ref, b_ref, of_ref, ob_ref):
    acc_ref = of_ref
    if len(a_ref.shape) == 3:
        a = jnp.concatenate([a_ref[h] for h in range(a_ref.shape[0])], axis=1)
    else:
        a = a_ref[...]
    acc_ref[...] = jnp.dot(a, w_ref[...], preferred_element_type=F32)
    gamma = g_ref[...]
    beta = b_ref[...]

    def rows(i, carry):
        r0 = pl.multiple_of(i * LN_ROWS, LN_ROWS)
        y = DEEPNORM_ALPHA * x_ref[pl.ds(r0, LN_ROWS), :] + acc_ref[pl.ds(r0, LN_ROWS), :]
        mu = jnp.mean(y, axis=-1, keepdims=True)
        yc = y - mu
        var = jnp.mean(yc * yc, axis=-1, keepdims=True)
        out = yc * lax.rsqrt(var + LN_EPS) * gamma + beta
        of_ref[pl.ds(r0, LN_ROWS), :] = out
        ob_ref[pl.ds(r0, LN_ROWS), :] = out.astype(BF16)
        return carry

    lax.fori_loop(0, acc_ref.shape[0] // LN_ROWS, rows, 0)


def _mm_res_ln(a, w, x, gamma, beta, *, tm=512):
    k, n = w.shape
    m = x.shape[0]
    if a.ndim == 4:
        _, heads, seq, dh = a.shape
        tps = seq // tm
        a_spec = pl.BlockSpec((None, heads, tm, dh), lambda i: (i // tps, 0, i % tps, 0))
    else:
        a_spec = pl.BlockSpec((tm, k), lambda i: (i, 0))
    return pl.pallas_call(
        _mm_res_ln_kernel,
        out_shape=(jax.ShapeDtypeStruct((m, n), F32), jax.ShapeDtypeStruct((m, n), BF16)),
        grid=(m // tm,),
        in_specs=[a_spec,
                  pl.BlockSpec((k, n), lambda i: (0, 0)),
                  pl.BlockSpec((tm, n), lambda i: (i, 0)),
                  pl.BlockSpec((1, n), lambda i: (0, 0)),
                  pl.BlockSpec((1, n), lambda i: (0, 0))],
        out_specs=(pl.BlockSpec((tm, n), lambda i: (i, 0)),
                   pl.BlockSpec((tm, n), lambda i: (i, 0))),
        compiler_params=_params("parallel"),
        name="mm_res_ln",
    )(a, w, x, gamma, beta)


FFN_TM = 1024
FFN_XRES = 256
FFN_CHUNK = 1024
FFN_PIECE = 32
HALO = 8


def _ffn_kernel(x_ref, wg_ref, wu_ref, cwg_ref, cwu_ref, cbg_ref, cbu_ref, wd_ref, xres_ref, g_ref, b_ref,
                of_ref, ob_ref, y0_ref, y1_ref, halo_ref, *, nj, tiles_per_seq):
    i = pl.program_id(0)
    j = pl.program_id(1)
    tm = x_ref.shape[0]
    tf = wd_ref.shape[0]
    nchunk = tm // FFN_CHUNK

    def conv(h, prev, cw_ref, cb_ref):
        cw = cw_ref[...]
        win = jnp.concatenate([prev, h], axis=0)
        return (cw[2:3, :] * h + cw[1:2, :] * pltpu.roll(win, 1, axis=0)[HALO:, :]
                + cw[0:1, :] * pltpu.roll(win, 2, axis=0)[HALO:, :] + cb_ref[...])

    def up_steps(y_ref):
        @pl.when(i % tiles_per_seq == 0)
        def _():
            halo_ref[j] = jnp.zeros(halo_ref.shape[1:], F32)

        prev = [halo_ref[j, :, half * tf:(half + 1) * tf] for half in range(2)]
        for c in range(nchunk):
            rows = pl.ds(c * FFN_CHUNK, FFN_CHUNK)
            xs = x_ref[rows, :]
            ys = []
            for half, (w_ref, cw_ref, cb_ref) in enumerate(((wg_ref, cwg_ref, cbg_ref), (wu_ref, cwu_ref, cbu_ref))):
                h = jnp.dot(xs, w_ref[...], preferred_element_type=F32)
                ys.append(conv(h, prev[half], cw_ref, cb_ref))
                prev[half] = h[FFN_CHUNK - HALO:, :]
            gate, up = ys
            y_ref[rows, :] = (gate / (1.0 + jnp.exp2(gate * (-LOG2E))) * up).astype(BF16)
            yield c
        for half in range(2):
            halo_ref[j, :, half * tf:(half + 1) * tf] = prev[half]

    def down_chunk(c, y_ref):
        rows = pl.ds(c * FFN_CHUNK, FFN_CHUNK)
        of_ref[rows, :] += jnp.dot(y_ref[rows, :], wd_ref[...], preferred_element_type=F32)

    nres = tm // FFN_XRES

    def first_step(y_ref):
        for c in up_steps(y_ref):
            pass
        of_ref[pl.ds(0, FFN_XRES), :] = DEEPNORM_ALPHA * xres_ref[...]
        of_ref[pl.ds(FFN_XRES, tm - FFN_XRES), :] = jnp.zeros((tm - FFN_XRES, of_ref.shape[1]), F32)

    def mid_step(y_up, y_prev):
        @pl.when(j < nres)
        def _():
            rows = pl.ds(pl.multiple_of(j * FFN_XRES, FFN_XRES), FFN_XRES)
            of_ref[rows, :] += DEEPNORM_ALPHA * xres_ref[...]

        for c in up_steps(y_up):
            down_chunk(c, y_prev)

    def last_step(y_prev):
        gamma = g_ref[...]
        beta = b_ref[...]
        for c in range(nchunk):
            down_chunk(c, y_prev)
            for p in range(FFN_CHUNK // FFN_PIECE):
                rows = pl.ds(c * FFN_CHUNK + p * FFN_PIECE, FFN_PIECE)
                y = of_ref[rows, :]
                mu = jnp.mean(y, axis=-1, keepdims=True)
                yc = y - mu
                var = jnp.mean(yc * yc, axis=-1, keepdims=True)
                out = yc * lax.rsqrt(var + LN_EPS) * gamma + beta
                of_ref[rows, :] = out
                ob_ref[rows, :] = out.astype(BF16)

    slots = (y0_ref, y1_ref)

    @pl.when(j == 0)
    def _():
        first_step(slots[0])

    for parity in (0, 1):
        @pl.when((j > 0) & (j < nj) & (j % 2 == parity))
        def _():
            mid_step(slots[parity], slots[1 - parity])

    @pl.when(j == nj)
    def _():
        last_step(slots[(nj - 1) % 2])


def _ffn(xb, xf, layer, wg, wu, cwg, cwu, cbg, cbu, wd, gamma, beta, seq):
    m, d = xb.shape
    tf = FF_TILE
    nj = wd.shape[1] // tf
    tm = min(FFN_TM, seq)
    nres = tm // FFN_XRES
    assert nj > nres
    up_j = lambda j: jnp.minimum(j, nj - 1)
    down_j = lambda j: jnp.maximum(j - 1, 0)
    up_blk = lambda rows: pl.BlockSpec((None, rows, tf), lambda i, j: (layer, 0, up_j(j)))
    return pl.pallas_call(
        functools.partial(_ffn_kernel, nj=nj, tiles_per_seq=seq // tm),
        out_shape=(jax.ShapeDtypeStruct((m, d), F32), jax.ShapeDtypeStruct((m, d), BF16)),
        grid=(m // tm, nj + 1),
        in_specs=[pl.BlockSpec((tm, d), lambda i, j: (i, 0)),
                  up_blk(d), up_blk(d), up_blk(CONV_WIDTH), up_blk(CONV_WIDTH), up_blk(1), up_blk(1),
                  pl.BlockSpec((None, tf, d), lambda i, j: (layer, down_j(j), 0)),
                  pl.BlockSpec((FFN_XRES, d), lambda i, j: (i * nres + jnp.minimum(j, nres - 1), 0)),
                  pl.BlockSpec((1, d), lambda i, j: (0, 0)),
                  pl.BlockSpec((1, d), lambda i, j: (0, 0))],
        out_specs=(pl.BlockSpec((tm, d), lambda i, j: (i, 0)),
                   pl.BlockSpec((tm, d), lambda i, j: (i, 0))),
        scratch_shapes=[pltpu.VMEM((tm, tf), BF16), pltpu.VMEM((tm, tf), BF16),
                        pltpu.VMEM((nj, HALO, 2 * tf), F32)],
        compiler_params=_params("arbitrary", "arbitrary", vmem_limit=FFN_VMEM_LIMIT),
        name="ffn",
    )(xb, wg, wu, cwg, cwu, cbg, cbu, wd, xf, gamma, beta)


DIL_PERMUTE_FROM = 16
DIL_PERMUTE_ROWS = 256


def _dil_proj_kernel(x_ref, w_ref, o_ref, *scratch, dil):
    acc = jnp.dot(x_ref[...], w_ref[...], preferred_element_type=F32)
    dh = DIL_HEAD_DIM
    if dil == 1:
        for h in range(DIL_HEADS):
            o_ref[h, 0] = acc[:, h * dh:(h + 1) * dh].astype(o_ref.dtype)
        return
    if dil < DIL_PERMUTE_FROM:
        (rows_ref,) = scratch
        n = x_ref.shape[0] // dil
        for h in range(DIL_HEADS):
            rows_ref[h] = acc[:, h * dh:(h + 1) * dh]
        for h in range(DIL_HEADS):
            for r in range(dil):
                o_ref[h, r] = rows_ref[h, pl.ds(r, n, stride=dil), :].astype(o_ref.dtype)
        return
    blk = DIL_PERMUTE_ROWS
    n = blk // dil
    dst = lax.broadcasted_iota(jnp.int32, (blk, blk), 0)
    src = lax.broadcasted_iota(jnp.int32, (blk, blk), 1)
    perm = jnp.where((dst // n == src % dil) & (dst % n == src // dil), 1.0, 0.0).astype(BF16)
    y = acc.astype(BF16)
    for q in range(x_ref.shape[0] // blk):
        p = jnp.dot(perm, y[q * blk:(q + 1) * blk, :], preferred_element_type=F32).astype(o_ref.dtype)
        for h in range(DIL_HEADS):
            for r in range(dil):
                o_ref[h, r, q * n:(q + 1) * n, :] = p[r * n:(r + 1) * n, h * dh:(h + 1) * dh]


def _dil_proj(xb, w, dil, batch, seq, *, tm=1024):
    m, k = xb.shape
    H, dh = DIL_HEADS, DIL_HEAD_DIM
    tm = min(tm, seq)
    tps = seq // tm
    scratch = [pltpu.VMEM((H, tm, dh), F32)] if 1 < dil < DIL_PERMUTE_FROM else []
    return pl.pallas_call(
        functools.partial(_dil_proj_kernel, dil=dil),
        out_shape=jax.ShapeDtypeStruct((batch, 3, H, dil, seq // dil, dh), BF16),
        grid=(m // tm, 3),
        in_specs=[pl.BlockSpec((tm, k), lambda i, c: (i, 0)),
                  pl.BlockSpec((k, H * dh), lambda i, c: (0, c))],
        out_specs=pl.BlockSpec((None, None, H, dil, tm // dil, dh), lambda i, c: (i // tps, c, 0, 0, i % tps, 0)),
        scratch_shapes=scratch,
        compiler_params=_params("parallel", "arbitrary"),
        name=f"dil_proj_d{dil}",
    )(xb, w)


NEG = -1e30
EXP2_SCALE = (DIL_HEAD_DIM ** -0.5) * LOG2E
ATTN_ROWS = 256


def _dil_attn_kernel(*refs, seq):
    qkv = refs[:3 * DIL_GROUPS]
    o_ref = refs[3 * DIL_GROUPS]
    m_ref, l_ref, acc_ref = refs[3 * DIL_GROUPS + 1:]
    P = DIL_BLOCK
    dh = DIL_HEAD_DIM
    nt = (((1,), (1,)), ((), ()))
    qi = lax.broadcasted_iota(jnp.int32, (P, P), 0)
    kj = lax.broadcasted_iota(jnp.int32, (P, P), 1)
    for gi, (window, dil) in enumerate(DIL_PATTERNS):
        q_ref, k_ref, v_ref = qkv[3 * gi:3 * gi + 3]
        steps = window // dil
        bias_cur = jnp.where((qi >= kj) & (qi - kj <= steps), 0.0, NEG)
        bias_prev = jnp.where(qi + P - kj <= steps, 0.0, NEG)
        blocks = [(r, lb * P) for r in range(dil) for lb in range(seq // dil // P)]
        bias = {P: bias_cur, 2 * P: jnp.concatenate([bias_prev, bias_cur], axis=1)}
        spans = [(r, lo - P if lo > 0 else lo, lo + P) for r, lo in blocks]
        s = [lax.dot_general(q_ref[r, lo:lo + P, :], k_ref[r, k0:k1, :], nt, preferred_element_type=F32)
             + bias[k1 - k0] for (r, lo), (_, k0, k1) in zip(blocks, spans)]
        m = [jnp.max(sb, axis=-1, keepdims=True) for sb in s]
        p = [jnp.exp2((sb - mb) * EXP2_SCALE).astype(BF16) for sb, mb in zip(s, m)]
        acc = [jnp.dot(pb, jnp.concatenate([v_ref[r, k0:k1, :], jnp.ones((k1 - k0, LANES), BF16)], axis=1),
                       preferred_element_type=F32) for pb, (r, k0, k1) in zip(p, spans)]
        for (r, lo), mb, ab in zip(blocks, m, acc):
            tok = pl.ds(lo * dil + r, P, stride=dil) if dil > 1 else pl.ds(lo, P)
            m_ref[gi, tok, :] = jnp.broadcast_to(mb, (P, LANES))
            l_ref[gi, tok, :] = ab[:, dh:]
            acc_ref[gi, tok, :] = ab[:, :dh]
    for c in range(seq // ATTN_ROWS):
        rows = pl.ds(c * ATTN_ROWS, ATTN_ROWS)
        ms = [m_ref[gi, rows, :] for gi in range(DIL_GROUPS)]
        m_all = functools.reduce(jnp.maximum, ms)
        ws = [jnp.exp2((mg - m_all) * EXP2_SCALE) for mg in ms]
        l_all = sum(w * l_ref[gi, rows, :] for gi, w in enumerate(ws))
        acc_all = sum(w * acc_ref[gi, rows, :] for gi, w in enumerate(ws))
        o_ref[rows, :] = (acc_all / l_all).astype(o_ref.dtype)


def _dil_attn(groups, batch, seq):
    dh = DIL_HEAD_DIM
    in_specs, args = [], []
    for (window, dil), t in zip(DIL_PATTERNS, groups):
        for c in range(3):
            in_specs.append(pl.BlockSpec((None, None, None, dil, seq // dil, dh),
                                         lambda b, h, c=c: (b, c, h, 0, 0, 0)))
            args.append(t)
    return pl.pallas_call(
        functools.partial(_dil_attn_kernel, seq=seq),
        out_shape=jax.ShapeDtypeStruct((batch, DIL_HEADS, seq, dh), BF16),
        grid=(batch, DIL_HEADS),
        in_specs=in_specs,
        out_specs=pl.BlockSpec((None, None, seq, dh), lambda b, h: (b, h, 0, 0)),
        scratch_shapes=[pltpu.VMEM((DIL_GROUPS, seq, LANES), F32), pltpu.VMEM((DIL_GROUPS, seq, LANES), F32),
                        pltpu.VMEM((DIL_GROUPS, seq, dh), F32)],
        compiler_params=_params("parallel", "parallel"),
        name="dil_attn",
    )(*args)


def _pad_cols(w, n):
    return jnp.pad(w, ((0, 0), (0, n - w.shape[1])))


def _gla_layer(xf, xb, w_in, w_gate_up, gate_bias, norm_g, w_out, gamma, beta, batch, seq):
    w_main = w_in[:, :GLA_MAIN].astype(BF16)
    w_gate = _pad_cols(w_in[:, GLA_MAIN:], LANES).astype(BF16)
    proj, g_low = _gla_proj(xf if xb is None else xb, w_main, w_gate)
    wgu = jnp.pad(w_gate_up, ((0, LANES - GLA_GATE_RANK), (0, 0)))
    o = _gla_core(proj.reshape(batch, seq, GLA_MAIN), g_low.reshape(batch, seq, LANES),
                  wgu, gate_bias[None, :], norm_g[None, :], batch, seq)
    return _mm_res_ln(o.reshape(batch * seq, GLA_DV), w_out.astype(BF16), xf, gamma[None, :], beta[None, :])


def _dil_layer(xf, xb, w_in, w_out, gamma, beta, batch, seq):
    gw = 3 * DIL_WIDTH
    w_in = w_in.astype(BF16)
    groups = [_dil_proj(xb, w_in[:, gi * gw:(gi + 1) * gw], dil, batch, seq)
              for gi, (window, dil) in enumerate(DIL_PATTERNS)]
    o = _dil_attn(groups, batch, seq)
    return _mm_res_ln(o, w_out.astype(BF16), xf, gamma[None, :], beta[None, :])


def _ffn_weights(w_up, conv_w, conv_b, w_down):
    pad = D_FF_PAD - D_FF
    halves = lambda t: [jnp.pad(h, ((0, 0), (0, 0), (0, pad))) for h in (t[..., :D_FF], t[..., D_FF:])]
    wg, wu = (h.astype(BF16) for h in halves(w_up))
    cwg, cwu = halves(conv_w)
    cbg, cbu = halves(conv_b[:, None, :])
    wd = jnp.pad(w_down, ((0, 0), (0, pad), (0, 0))).astype(BF16)
    return wg, wu, cwg, cwu, cbg, cbu, wd


def kernel(x, gla_w_in, gla_w_gate_up, gla_gate_bias, gla_norm_g, gla_w_out, dil_w_in, dil_w_out, ffn_w_up,
           ffn_conv_w, ffn_conv_b, ffn_w_down, ln_g, ln_b):
    batch, seq, d = x.shape
    xf = x.reshape(batch * seq, d)
    xb = None
    ffn_weights = _ffn_weights(ffn_w_up, ffn_conv_w, ffn_conv_b, ffn_w_down)
    for i in range(DEPTH):
        j = i // 2
        if i % 2 == 0:
            xf, xb = _gla_layer(xf, xb, gla_w_in[j], gla_w_gate_up[j], gla_gate_bias[j], gla_norm_g[j],
                                gla_w_out[j], ln_g[i, 0], ln_b[i, 0], batch, seq)
        else:
            xf, xb = _dil_layer(xf, xb, dil_w_in[j], dil_w_out[j], ln_g[i, 0], ln_b[i, 0], batch, seq)
        xf, xb = _ffn(xb, xf, i, *ffn_weights, ln_g[i, 1][None, :], ln_b[i, 1][None, :], seq)
    return xf.reshape(batch, seq, d)
```

```python
import functools

import jax
import jax.numpy as jnp
from jax import lax
from jax.experimental import pallas as pl
from jax.experimental.pallas import tpu as pltpu

F32 = jnp.float32
BF16 = jnp.bfloat16

D_MODEL = 2048
DEPTH = 4

GLA_HEADS = 4
GLA_DK = D_MODEL // 2
GLA_DV = D_MODEL
GLA_HEAD_K = GLA_DK // GLA_HEADS
GLA_HEAD_V = GLA_DV // GLA_HEADS
GLA_GATE_RANK = 16
GLA_GATE_NORMALIZER = 16.0
GLA_CHUNK = 64
GLA_MAIN = 2 * GLA_DK + 2 * GLA_DV

DIL_PATTERNS = ((128, 1), (512, 4), (2048, 16))
DIL_GROUPS = len(DIL_PATTERNS)
DIL_HEADS = 8
DIL_HEAD_DIM = 128
DIL_WIDTH = DIL_HEADS * DIL_HEAD_DIM
DIL_BLOCK = 128

D_FF = 5504
CONV_WIDTH = 3

DEEPNORM_ALPHA = (2 * DEPTH) ** 0.25
LN_EPS = 1e-5
RMS_EPS = 1e-6

LANES = 128
LOG2E = 1.4426950408889634
FF_TILE = 512
D_FF_PAD = -(-D_FF // FF_TILE) * FF_TILE
VMEM_LIMIT = 56 * 1024 * 1024
FFN_VMEM_LIMIT = 62 * 1024 * 1024


def _params(*sem, vmem_limit=VMEM_LIMIT):
    return pltpu.CompilerParams(dimension_semantics=sem, vmem_limit_bytes=vmem_limit)


def _gla_proj_kernel(x_ref, w_ref, wg_ref, o_ref, g_ref, *scratch):
    if scratch:
        (xb_ref,) = scratch

        @pl.when(pl.program_id(1) == 0)
        def _():
            xb_ref[...] = x_ref[...].astype(BF16)
    else:
        xb_ref = x_ref
    x = xb_ref[...]
    o_ref[...] = jnp.dot(x, w_ref[...], preferred_element_type=F32).astype(o_ref.dtype)

    @pl.when(pl.program_id(1) == 0)
    def _():
        g_ref[...] = jnp.dot(x, wg_ref[...], preferred_element_type=F32)


def _gla_proj(x, w_main, w_gate, *, tm=1024, tn=1024):
    m, k = x.shape
    n = w_main.shape[1]
    tm = min(tm, m)
    scratch = [] if x.dtype == BF16 else [pltpu.VMEM((tm, k), BF16)]
    return pl.pallas_call(
        _gla_proj_kernel,
        out_shape=(jax.ShapeDtypeStruct((m, n), BF16), jax.ShapeDtypeStruct((m, LANES), F32)),
        grid=(m // tm, n // tn),
        in_specs=[pl.BlockSpec((tm, k), lambda i, j: (i, 0)),
                  pl.BlockSpec((k, tn), lambda i, j: (0, j)),
                  pl.BlockSpec((k, LANES), lambda i, j: (0, 0))],
        out_specs=(pl.BlockSpec((tm, tn), lambda i, j: (i, j)),
                   pl.BlockSpec((tm, LANES), lambda i, j: (i, 0))),
        scratch_shapes=scratch,
        compiler_params=_params("parallel", "arbitrary"),
        name="gla_proj",
    )(x, w_main, w_gate)


GLA_HEADS_PER_STEP = 2
GLA_PAIRS_PER_STEP = 2
GLA_PAIR_ROWS = 8


def _gla_core_kernel(q_ref, k_ref, v_ref, r_ref, gl_ref, wgu_ref, gb_ref, ng_ref, o_ref,
                     st_ref, a_ref, qd_ref, kd_ref, dec_ref, vt_ref, whi_ref, wlo_ref, *, nchunks):
    C, c = GLA_CHUNK, GLA_PAIR_ROWS
    nsub = C // c
    dk, dv = GLA_HEAD_K, GLA_HEAD_V
    heads = range(GLA_HEADS_PER_STEP)

    rows_i = lax.broadcasted_iota(jnp.int32, (C, C), 0)
    cols_i = lax.broadcasted_iota(jnp.int32, (C, C), 1)
    tri = (rows_i >= cols_i).astype(F32)
    sub_rows = lax.broadcasted_iota(jnp.int32, (c, C), 0)
    sub_cols = lax.broadcasted_iota(jnp.int32, (c, C), 1)
    nt = (((1,), (1,)), ((), ()))
    hi = lax.Precision.HIGHEST

    w = wgu_ref[...]
    whi_ref[...] = w.astype(BF16)
    wlo_ref[...] = (w - whi_ref[...].astype(F32)).astype(BF16)

    def prepare(pi, carry):
        NP = GLA_PAIRS_PER_STEP
        items = [(h, u) for u in range(2 * NP) for h in heads]
        rows = [pl.ds(pl.multiple_of((2 * NP * pi + u) * C, C), C) for h, u in items]
        q = [q_ref[rs, h * dk:(h + 1) * dk].astype(F32) * (dk ** -0.5) for (h, u), rs in zip(items, rows)]
        k = [k_ref[rs, h * dk:(h + 1) * dk].astype(F32) for (h, u), rs in zip(items, rows)]
        gl = [gl_ref[rs, :] for rs in rows]
        gl_hi = [t.astype(BF16) for t in gl]
        gl_lo = [(t - th.astype(F32)).astype(BF16) for t, th in zip(gl, gl_hi)]
        z = [(jnp.dot(th, whi_ref[:, h * dk:(h + 1) * dk], preferred_element_type=F32)
              + jnp.dot(tl, whi_ref[:, h * dk:(h + 1) * dk], preferred_element_type=F32)
              + jnp.dot(th, wlo_ref[:, h * dk:(h + 1) * dk], preferred_element_type=F32)
              + gb_ref[:, h * dk:(h + 1) * dk]) * LOG2E for (h, u), th, tl in zip(items, gl_hi, gl_lo)]
        g = [(jnp.minimum(zh, 0.0) - jnp.log2(1.0 + jnp.exp2(-jnp.abs(zh)))) * (1.0 / GLA_GATE_NORMALIZER) for zh in z]
        b = [jnp.dot(tri, gh, preferred_element_type=F32, precision=hi) for gh in g]
        b_last = [bh[C - 1:C, :] for bh in b]
        qd = [qh * jnp.exp2(bh) for qh, bh in zip(q, b)]
        kd = [kh * jnp.exp2(bl - bh) for kh, bh, bl in zip(k, b, b_last)]
        dec = [jnp.exp2(bl) for bl in b_last]

        n = len(items)
        subs = [(t, s) for s in range(nsub) for t in range(n)]
        a_s = {}
        for t, s in subs:
            lo = s * c
            if s == 0:
                a_s[t, s] = jnp.zeros((c, C), F32)
            else:
                b_ref = b[t][lo - 1:lo, :]
                q_ref_s = (q[t][lo:lo + c, :] * jnp.exp2(b[t][lo:lo + c, :] - b_ref)).astype(BF16)
                k_ref_s = (k[t][:lo, :] * jnp.exp2(b_ref - b[t][:lo, :])).astype(BF16)
                k_ref_s = jnp.concatenate([k_ref_s, jnp.zeros((C - lo, dk), BF16)], axis=0)
                a_s[t, s] = lax.dot_general(q_ref_s, k_ref_s, nt, preferred_element_type=F32)
        for j in range(c):
            for t, s in subs:
                lo = s * c
                bs = b[t][lo:lo + c, :]
                e = jnp.exp2(bs - bs[j:j + 1, :])
                col = jnp.sum(q[t][lo:lo + c, :] * k[t][lo + j:lo + j + 1, :] * e, axis=-1, keepdims=True)
                a_s[t, s] = jnp.where((sub_cols == lo + j) & (sub_rows >= j), col, a_s[t, s])
        a = [jnp.concatenate([a_s[t, s] for s in range(nsub)], axis=0) for t in range(n)]

        for h, pp in [(h, pp) for pp in range(NP) for h in heads]:
            t0, t1 = items.index((h, 2 * pp)), items.index((h, 2 * pp + 1))
            pr = NP * pi + pp
            pair_rows = pl.ds(pl.multiple_of(pr * 2 * C, 2 * C), 2 * C)
            qd0, qd1 = qd[t0].astype(BF16), qd[t1].astype(BF16)
            kd0, kd1 = kd[t0].astype(BF16), kd[t1].astype(BF16)
            cross = lax.dot_general(qd1, kd0, nt, preferred_element_type=F32)
            qd_ref[h, rows[t0], :] = qd0
            qd_ref[h, rows[t1], :] = (qd[t1] * dec[t0]).astype(BF16)
            kd_ref[h, rows[t0], :] = (kd[t0] * dec[t1]).astype(BF16)
            kd_ref[h, rows[t1], :] = kd1
            dec_ref[h, pl.ds(pr, 1), :] = dec[t0] * dec[t1]
            a_ref[h, rows[t0], :] = jnp.concatenate([a[t0], jnp.zeros((C, C), F32)], axis=1).astype(BF16)
            a_ref[h, rows[t1], :] = jnp.concatenate([cross, a[t1]], axis=1).astype(BF16)
            vt_ref[h, pr] = v_ref[pair_rows, h * dv:(h + 1) * dv].astype(F32).T.astype(BF16)
        return carry

    lax.fori_loop(0, nchunks // (2 * GLA_PAIRS_PER_STEP), prepare, 0)

    st_ref[...] = jnp.zeros_like(st_ref)
    ng = ng_ref[...]

    def recur(pi, carry):
        rs = pl.ds(pl.multiple_of(pi * 2 * C, 2 * C), 2 * C)
        st = [st_ref[h] for h in heads]
        o = [lax.dot_general(qd_ref[h, rs, :], st[h].astype(BF16), nt, preferred_element_type=F32) for h in heads]
        o = [o[h] + jnp.dot(a_ref[h, rs, :], v_ref[rs, h * dv:(h + 1) * dv], preferred_element_type=F32) for h in heads]
        upd = [jnp.dot(vt_ref[h, pi], kd_ref[h, rs, :], preferred_element_type=F32) for h in heads]
        for h in heads:
            st_ref[h] = st[h] * dec_ref[h, pl.ds(pi, 1), :] + upd[h]
        o = [oh * lax.rsqrt(jnp.mean(oh * oh, axis=-1, keepdims=True) + RMS_EPS) * ng for oh in o]
        r = [r_ref[rs, h * dv:(h + 1) * dv].astype(F32) for h in heads]
        for h in heads:
            o_ref[rs, h * dv:(h + 1) * dv] = (o[h] * (r[h] / (1.0 + jnp.exp2(r[h] * (-LOG2E))))).astype(o_ref.dtype)
        return carry

    lax.fori_loop(0, nchunks // 2, recur, 0)


def _gla_core(proj, g_low, w_gate_up, gate_bias, norm_g, batch, seq):
    dk, dv, hp = GLA_HEAD_K, GLA_HEAD_V, GLA_HEADS_PER_STEP
    nchunks = seq // GLA_CHUNK
    npairs = nchunks // 2
    q_blk = lambda off: pl.BlockSpec((None, seq, hp * dk), lambda b, h: (b, 0, off // (hp * dk) + h))
    v_blk = lambda off: pl.BlockSpec((None, seq, hp * dv), lambda b, h: (b, 0, off // (hp * dv) + h))
    return pl.pallas_call(
        functools.partial(_gla_core_kernel, nchunks=nchunks),
        out_shape=jax.ShapeDtypeStruct((batch, seq, GLA_DV), BF16),
        grid=(batch, GLA_HEADS // hp),
        in_specs=[q_blk(0), q_blk(GLA_DK), v_blk(2 * GLA_DK), v_blk(2 * GLA_DK + GLA_DV),
                  pl.BlockSpec((None, seq, LANES), lambda b, h: (b, 0, 0)),
                  pl.BlockSpec((LANES, hp * dk), lambda b, h: (0, h)),
                  pl.BlockSpec((1, hp * dk), lambda b, h: (0, h)),
                  pl.BlockSpec((1, dv), lambda b, h: (0, 0))],
        out_specs=pl.BlockSpec((None, seq, hp * dv), lambda b, h: (b, 0, h)),
        scratch_shapes=[pltpu.VMEM((hp, dv, dk), F32),
                        pltpu.VMEM((hp, seq, 2 * GLA_CHUNK), BF16),
                        pltpu.VMEM((hp, seq, dk), BF16),
                        pltpu.VMEM((hp, seq, dk), BF16),
                        pltpu.VMEM((hp, npairs, dk), F32),
                        pltpu.VMEM((hp, npairs, dv, 2 * GLA_CHUNK), BF16),
                        pltpu.VMEM((LANES, hp * dk), BF16),
                        pltpu.VMEM((LANES, hp * dk), BF16)],
        compiler_params=_params("parallel", "arbitrary"),
        name="gla_core",
    )(proj, proj, proj, proj, g_low, w_gate_up, gate_bias, norm_g)


LN_ROWS = 128


def _mm_res_ln_kernel(a_ref, w_ref, x_ref, g_ref, b_ref, of_ref, ob_ref):
    acc_ref = of_ref
    if len(a_ref.shape) == 3:
        a = jnp.concatenate([a_ref[h] for h in range(a_ref.shape[0])], axis=1)
    else:
        a = a_ref[...]
    acc_ref[...] = jnp.dot(a, w_ref[...], preferred_element_type=F32)
    gamma = g_ref[...]
    beta = b_ref[...]

    def rows(i, carry):
        r0 = pl.multiple_of(i * LN_ROWS, LN_ROWS)
        y = DEEPNORM_ALPHA * x_ref[pl.ds(r0, LN_ROWS), :] + acc_ref[pl.ds(r0, LN_ROWS), :]
        mu = jnp.mean(y, axis=-1, keepdims=True)
        yc = y - mu
        var = jnp.mean(yc * yc, axis=-1, keepdims=True)
        out = yc * lax.rsqrt(var + LN_EPS) * gamma + beta
        of_ref[pl.ds(r0, LN_ROWS), :] = out
        ob_ref[pl.ds(r0, LN_ROWS), :] = out.astype(BF16)
        return carry

    lax.fori_loop(0, acc_ref.shape[0] // LN_ROWS, rows, 0)


def _mm_res_ln(a, w, x, gamma, beta, *, tm=512):
    k, n = w.shape
    m = x.shape[0]
    if a.ndim == 4:
        _, heads, seq, dh = a.shape
        tps = seq // tm
        a_spec = pl.BlockSpec((None, heads, tm, dh), lambda i: (i // tps, 0, i % tps, 0))
    else:
        a_spec = pl.BlockSpec((tm, k), lambda i: (i, 0))
    return pl.pallas_call(
        _mm_res_ln_kernel,
        out_shape=(jax.ShapeDtypeStruct((m, n), F32), jax.ShapeDtypeStruct((m, n), BF16)),
        grid=(m // tm,),
        in_specs=[a_spec,
                  pl.BlockSpec((k, n), lambda i: (0, 0)),
                  pl.BlockSpec((tm, n), lambda i: (i, 0)),
                  pl.BlockSpec((1, n), lambda i: (0, 0)),
                  pl.BlockSpec((1, n), lambda i: (0, 0))],
        out_specs=(pl.BlockSpec((tm, n), lambda i: (i, 0)),
                   pl.BlockSpec((tm, n), lambda i: (i, 0))),
        compiler_params=_params("parallel"),
        name="mm_res_ln",
    )(a, w, x, gamma, beta)


FFN_TM = 1024
FFN_XRES = 256
FFN_CHUNK = 1024
FFN_PIECE = 32
HALO = 8


def _ffn_kernel(x_ref, wg_ref, wu_ref, cwg_ref, cwu_ref, cbg_ref, cbu_ref, wd_ref, xres_ref, g_ref, b_ref,
                of_ref, ob_ref, y0_ref, y1_ref, halo_ref, *, nj, tiles_per_seq):
    i = pl.program_id(0)
    j = pl.program_id(1)
    tm = x_ref.shape[0]
    tf = wd_ref.shape[0]
    nchunk = tm // FFN_CHUNK

    def conv(h, prev, cw_ref, cb_ref):
        cw = cw_ref[...]
        win = jnp.concatenate([prev, h], axis=0)
        return (cw[2:3, :] * h + cw[1:2, :] * pltpu.roll(win, 1, axis=0)[HALO:, :]
                + cw[0:1, :] * pltpu.roll(win, 2, axis=0)[HALO:, :] + cb_ref[...])

    def up_steps(y_ref):
        @pl.when(i % tiles_per_seq == 0)
        def _():
            halo_ref[j] = jnp.zeros(halo_ref.shape[1:], F32)

        prev = [halo_ref[j, :, half * tf:(half + 1) * tf] for half in range(2)]
        for c in range(nchunk):
            rows = pl.ds(c * FFN_CHUNK, FFN_CHUNK)
            xs = x_ref[rows, :]
            ys = []
            for half, (w_ref, cw_ref, cb_ref) in enumerate(((wg_ref, cwg_ref, cbg_ref), (wu_ref, cwu_ref, cbu_ref))):
                h = jnp.dot(xs, w_ref[...], preferred_element_type=F32)
                ys.append(conv(h, prev[half], cw_ref, cb_ref))
                prev[half] = h[FFN_CHUNK - HALO:, :]
            gate, up = ys
            y_ref[rows, :] = (gate / (1.0 + jnp.exp2(gate * (-LOG2E))) * up).astype(BF16)
            yield c
        for half in range(2):
            halo_ref[j, :, half * tf:(half + 1) * tf] = prev[half]

    def down_chunk(c, y_ref):
        rows = pl.ds(c * FFN_CHUNK, FFN_CHUNK)
        of_ref[rows, :] += jnp.dot(y_ref[rows, :], wd_ref[...], preferred_element_type=F32)

    nres = tm // FFN_XRES

    def first_step(y_ref):
        for c in up_steps(y_ref):
            pass
        of_ref[pl.ds(0, FFN_XRES), :] = DEEPNORM_ALPHA * xres_ref[...]
        of_ref[pl.ds(FFN_XRES, tm - FFN_XRES), :] = jnp.zeros((tm - FFN_XRES, of_ref.shape[1]), F32)

    def mid_step(y_up, y_prev):
        @pl.when(j < nres)
        def _():
            rows = pl.ds(pl.multiple_of(j * FFN_XRES, FFN_XRES), FFN_XRES)
            of_ref[rows, :] += DEEPNORM_ALPHA * xres_ref[...]

        for c in up_steps(y_up):
            down_chunk(c, y_prev)

    def last_step(y_prev):
        gamma = g_ref[...]
        beta = b_ref[...]
        for c in range(nchunk):
            down_chunk(c, y_prev)
            for p in range(FFN_CHUNK // FFN_PIECE):
                rows = pl.ds(c * FFN_CHUNK + p * FFN_PIECE, FFN_PIECE)
                y = of_ref[rows, :]
                mu = jnp.mean(y, axis=-1, keepdims=True)
                yc = y - mu
                var = jnp.mean(yc * yc, axis=-1, keepdims=True)
                out = yc * lax.rsqrt(var + LN_EPS) * gamma + beta
                of_ref[rows, :] = out
                ob_ref[rows, :] = out.astype(BF16)

    slots = (y0_ref, y1_ref)

    @pl.when(j == 0)
    def _():
        first_step(slots[0])

    for parity in (0, 1):
        @pl.when((j > 0) & (j < nj) & (j % 2 == parity))
        def _():
            mid_step(slots[parity], slots[1 - parity])

    @pl.when(j == nj)
    def _():
        last_step(slots[(nj - 1) % 2])


def _ffn(xb, xf, layer, wg, wu, cwg, cwu, cbg, cbu, wd, gamma, beta, seq):
    m, d = xb.shape
    tf = FF_TILE
    nj = wd.shape[1] // tf
    tm = min(FFN_TM, seq)
    nres = tm // FFN_XRES
    assert nj > nres
    up_j = lambda j: jnp.minimum(j, nj - 1)
    down_j = lambda j: jnp.maximum(j - 1, 0)
    up_blk = lambda rows: pl.BlockSpec((None, rows, tf), lambda i, j: (layer, 0, up_j(j)))
    return pl.pallas_call(
        functools.partial(_ffn_kernel, nj=nj, tiles_per_seq=seq // tm),
        out_shape=(jax.ShapeDtypeStruct((m, d), F32), jax.ShapeDtypeStruct((m, d), BF16)),
        grid=(m // tm, nj + 1),
        in_specs=[pl.BlockSpec((tm, d), lambda i, j: (i, 0)),
                  up_blk(d), up_blk(d), up_blk(CONV_WIDTH), up_blk(CONV_WIDTH), up_blk(1), up_blk(1),
                  pl.BlockSpec((None, tf, d), lambda i, j: (layer, down_j(j), 0)),
                  pl.BlockSpec((FFN_XRES, d), lambda i, j: (i * nres + jnp.minimum(j, nres - 1), 0)),
                  pl.BlockSpec((1, d), lambda i, j: (0, 0)),
                  pl.BlockSpec((1, d), lambda i, j: (0, 0))],
        out_specs=(pl.BlockSpec((tm, d), lambda i, j: (i, 0)),
                   pl.BlockSpec((tm, d), lambda i, j: (i, 0))),
        scratch_shapes=[pltpu.VMEM((tm, tf), BF16), pltpu.VMEM((tm, tf), BF16),
                        pltpu.VMEM((nj, HALO, 2 * tf), F32)],
        compiler_params=_params("arbitrary", "arbitrary", vmem_limit=FFN_VMEM_LIMIT),
        name="ffn",
    )(xb, wg, wu, cwg, cwu, cbg, cbu, wd, xf, gamma, beta)


DIL_PERMUTE_FROM = 16
DIL_PERMUTE_ROWS = 256


def _dil_proj_kernel(x_ref, w_ref, o_ref, *scratch, dil):
    acc = jnp.dot(x_ref[...], w_ref[...], preferred_element_type=F32)
    dh = DIL_HEAD_DIM
    if dil == 1:
        for h in range(DIL_HEADS):
            o_ref[h, 0] = acc[:, h * dh:(h + 1) * dh].astype(o_ref.dtype)
        return
    if dil < DIL_PERMUTE_FROM:
        (rows_ref,) = scratch
        n = x_ref.shape[0] // dil
        for h in range(DIL_HEADS):
            rows_ref[h] = acc[:, h * dh:(h + 1) * dh]
        for h in range(DIL_HEADS):
            for r in range(dil):
                o_ref[h, r] = rows_ref[h, pl.ds(r, n, stride=dil), :].astype(o_ref.dtype)
        return
    blk = DIL_PERMUTE_ROWS
    n = blk // dil
    dst = lax.broadcasted_iota(jnp.int32, (blk, blk), 0)
    src = lax.broadcasted_iota(jnp.int32, (blk, blk), 1)
    perm = jnp.where((dst // n == src % dil) & (dst % n == src // dil), 1.0, 0.0).astype(BF16)
    y = acc.astype(BF16)
    for q in range(x_ref.shape[0] // blk):
        p = jnp.dot(perm, y[q * blk:(q + 1) * blk, :], preferred_element_type=F32).astype(o_ref.dtype)
        for h in range(DIL_HEADS):
            for r in range(dil):
                o_ref[h, r, q * n:(q + 1) * n, :] = p[r * n:(r + 1) * n, h * dh:(h + 1) * dh]


def _dil_proj(xb, w, dil, batch, seq, *, tm=2048):
    m, k = xb.shape
    H, dh = DIL_HEADS, DIL_HEAD_DIM
    tm = min(tm, seq)
    tps = seq // tm
    scratch = [pltpu.VMEM((H, tm, dh), F32)] if 1 < dil < DIL_PERMUTE_FROM else []
    return pl.pallas_call(
        functools.partial(_dil_proj_kernel, dil=dil),
        out_shape=jax.ShapeDtypeStruct((batch, 3, H, dil, seq // dil, dh), BF16),
        grid=(m // tm, 3),
        in_specs=[pl.BlockSpec((tm, k), lambda i, c: (i, 0)),
                  pl.BlockSpec((k, H * dh), lambda i, c: (0, c))],
        out_specs=pl.BlockSpec((None, None, H, dil, tm // dil, dh), lambda i, c: (i // tps, c, 0, 0, i % tps, 0)),
        scratch_shapes=scratch,
        compiler_params=_params("parallel", "arbitrary"),
        name=f"dil_proj_d{dil}",
    )(xb, w)


NEG = -1e30
EXP2_SCALE = (DIL_HEAD_DIM ** -0.5) * LOG2E
ATTN_ROWS = 256


def _dil_attn_kernel(*refs, seq):
    qkv = refs[:3 * DIL_GROUPS]
    o_ref = refs[3 * DIL_GROUPS]
    m_ref, l_ref, acc_ref = refs[3 * DIL_GROUPS + 1:]
    P = DIL_BLOCK
    dh = DIL_HEAD_DIM
    nt = (((1,), (1,)), ((), ()))
    qi = lax.broadcasted_iota(jnp.int32, (P, P), 0)
    kj = lax.broadcasted_iota(jnp.int32, (P, P), 1)
    for gi, (window, dil) in enumerate(DIL_PATTERNS):
        q_ref, k_ref, v_ref = qkv[3 * gi:3 * gi + 3]
        steps = window // dil
        bias_cur = jnp.where((qi >= kj) & (qi - kj <= steps), 0.0, NEG)
        bias_prev = jnp.where(qi + P - kj <= steps, 0.0, NEG)
        blocks = [(r, lb * P) for r in range(dil) for lb in range(seq // dil // P)]
        bias = {P: bias_cur, 2 * P: jnp.concatenate([bias_prev, bias_cur], axis=1)}
        spans = [(r, lo - P if lo > 0 else lo, lo + P) for r, lo in blocks]
        s = [lax.dot_general(q_ref[r, lo:lo + P, :], k_ref[r, k0:k1, :], nt, preferred_element_type=F32)
             + bias[k1 - k0] for (r, lo), (_, k0, k1) in zip(blocks, spans)]
        m = [jnp.max(sb, axis=-1, keepdims=True) for sb in s]
        p = [jnp.exp2((sb - mb) * EXP2_SCALE).astype(BF16) for sb, mb in zip(s, m)]
        acc = [jnp.dot(pb, jnp.concatenate([v_ref[r, k0:k1, :], jnp.ones((k1 - k0, LANES), BF16)], axis=1),
                       preferred_element_type=F32) for pb, (r, k0, k1) in zip(p, spans)]
        for (r, lo), mb, ab in zip(blocks, m, acc):
            tok = pl.ds(lo * dil + r, P, stride=dil) if dil > 1 else pl.ds(lo, P)
            m_ref[gi, tok, :] = jnp.broadcast_to(mb, (P, LANES))
            l_ref[gi, tok, :] = ab[:, dh:]
            acc_ref[gi, tok, :] = ab[:, :dh]
    for c in range(seq // ATTN_ROWS):
        rows = pl.ds(c * ATTN_ROWS, ATTN_ROWS)
        ms = [m_ref[gi, rows, :] for gi in range(DIL_GROUPS)]
        m_all = functools.reduce(jnp.maximum, ms)
        ws = [jnp.exp2((mg - m_all) * EXP2_SCALE) for mg in ms]
        l_all = sum(w * l_ref[gi, rows, :] for gi, w in enumerate(ws))
        acc_all = sum(w * acc_ref[gi, rows, :] for gi, w in enumerate(ws))
        o_ref[rows, :] = (acc_all / l_all).astype(o_ref.dtype)


def _dil_attn(groups, batch, seq):
    dh = DIL_HEAD_DIM
    in_specs, args = [], []
    for (window, dil), t in zip(DIL_PATTERNS, groups):
        for c in range(3):
            in_specs.append(pl.BlockSpec((None, None, None, dil, seq // dil, dh),
                                         lambda b, h, c=c: (b, c, h, 0, 0, 0)))
            args.append(t)
    return pl.pallas_call(
        functools.partial(_dil_attn_kernel, seq=seq),
        out_shape=jax.ShapeDtypeStruct((batch, DIL_HEADS, seq, dh), BF16),
        grid=(batch, DIL_HEADS),
        in_specs=in_specs,
        out_specs=pl.BlockSpec((None, None, seq, dh), lambda b, h: (b, h, 0, 0)),
        scratch_shapes=[pltpu.VMEM((DIL_GROUPS, seq, LANES), F32), pltpu.VMEM((DIL_GROUPS, seq, LANES), F32),
                        pltpu.VMEM((DIL_GROUPS, seq, dh), F32)],
        compiler_params=_params("parallel", "parallel"),
        name="dil_attn",
    )(*args)


def _pad_cols(w, n):
    return jnp.pad(w, ((0, 0), (0, n - w.shape[1])))


def _gla_layer(xf, xb, w_in, w_gate_up, gate_bias, norm_g, w_out, gamma, beta, batch, seq):
    w_main = w_in[:, :GLA_MAIN].astype(BF16)
    w_gate = _pad_cols(w_in[:, GLA_MAIN:], LANES).astype(BF16)
    proj, g_low = _gla_proj(xf if xb is None else xb, w_main, w_gate)
    wgu = jnp.pad(w_gate_up, ((0, LANES - GLA_GATE_RANK), (0, 0)))
    o = _gla_core(proj.reshape(batch, seq, GLA_MAIN), g_low.reshape(batch, seq, LANES),
                  wgu, gate_bias[None, :], norm_g[None, :], batch, seq)
    return _mm_res_ln(o.reshape(batch * seq, GLA_DV), w_out.astype(BF16), xf, gamma[None, :], beta[None, :])


def _dil_layer(xf, xb, w_in, w_out, gamma, beta, batch, seq):
    gw = 3 * DIL_WIDTH
    w_in = w_in.astype(BF16)
    groups = [_dil_proj(xb, w_in[:, gi * gw:(gi + 1) * gw], dil, batch, seq)
              for gi, (window, dil) in enumerate(DIL_PATTERNS)]
    o = _dil_attn(groups, batch, seq)
    return _mm_res_ln(o, w_out.astype(BF16), xf, gamma[None, :], beta[None, :])


def _ffn_weights(w_up, conv_w, conv_b, w_down):
    pad = D_FF_PAD - D_FF
    halves = lambda t: [jnp.pad(h, ((0, 0), (0, 0), (0, pad))) for h in (t[..., :D_FF], t[..., D_FF:])]
    wg, wu = (h.astype(BF16) for h in halves(w_up))
    cwg, cwu = halves(conv_w)
    cbg, cbu = halves(conv_b[:, None, :])
    wd = jnp.pad(w_down, ((0, 0), (0, pad), (0, 0))).astype(BF16)
    return wg, wu, cwg, cwu, cbg, cbu, wd


def kernel(x, gla_w_in, gla_w_gate_up, gla_gate_bias, gla_norm_g, gla_w_out, dil_w_in, dil_w_out, ffn_w_up,
           ffn_conv_w, ffn_conv_b, ffn_w_down, ln_g, ln_b):
    batch, seq, d = x.shape
    xf = x.reshape(batch * seq, d)
    xb = None
    ffn_weights = _ffn_weights(ffn_w_up, ffn_conv_w, ffn_conv_b, ffn_w_down)
    for i in range(DEPTH):
        j = i // 2
        if i % 2 == 0:
            xf, xb = _gla_layer(xf, xb, gla_w_in[j], gla_w_gate_up[j], gla_gate_bias[j], gla_norm_g[j],
                                gla_w_out[j], ln_g[i, 0], ln_b[i, 0], batch, seq)
        else:
            xf, xb = _dil_layer(xf, xb, dil_w_in[j], dil_w_out[j], ln_g[i, 0], ln_b[i, 0], batch, seq)
        xf, xb = _ffn(xb, xf, i, *ffn_weights, ln_g[i, 1][None, :], ln_b[i, 1][None, :], seq)
    return xf.reshape(batch, seq, d)
```
